```python
import math
import numpy as np
import jax
import jax.numpy as jnp
from jax import lax

D_MODEL = 1024
BATCH = 2
SEQ = 8192
DEPTH = 1

EPS = 1e-6
ROPE_THETA = 10000.0
NEG = -1e30

SSM_HEADS = 16
SSM_HEAD_DIM = 64
SSM_D_INNER = SSM_HEADS * SSM_HEAD_DIM
SSM_GROUPS = 4
SSM_STATE = 128
SSM_CONV = 4
SSM_CHUNK = 256
SSM_CONV_DIM = SSM_D_INNER + 2 * SSM_GROUPS * SSM_STATE

ATTN_HEADS = 16
ATTN_KV_GROUPS = 4
ATTN_HEAD_DIM = 64
ATTN_WIDTH = ATTN_HEADS * ATTN_HEAD_DIM
KV_WIDTH = ATTN_KV_GROUPS * ATTN_HEAD_DIM
CMP_BLOCK = 32
CMP_STRIDE = 16
CMP_HIDDEN = 256
SLC_BLOCK = 64
SLC_TOP_N = 16
WINDOW = 512
Q_BLOCK = 128

IN_SPLITS = (SSM_D_INNER, SSM_CONV_DIM, SSM_HEADS, ATTN_WIDTH, 6 * KV_WIDTH,
             3 * ATTN_HEADS, ATTN_WIDTH, D_MODEL, D_MODEL)
IN_TOTAL = sum(IN_SPLITS)

kernel_name = "hybrid_ssd_nsa_gated_merge"


def rms_norm(x, w):
    xf = x.astype(jnp.float32)
    y = xf * lax.rsqrt(jnp.mean(xf * xf, axis=-1, keepdims=True) + EPS)
    return (y * w.astype(jnp.float32)).astype(x.dtype)


def rope(u, pos):
    d = u.shape[-1]
    half = d // 2
    inv_freq = 1.0 / (ROPE_THETA ** (jnp.arange(half, dtype=jnp.float32) * 2.0 / d))
    ang = pos[:, None] * inv_freq[None, :]
    cos = jnp.cos(ang)[None, :, None, :]
    sin = jnp.sin(ang)[None, :, None, :]
    uf = u.astype(jnp.float32)
    u1, u2 = uf[..., :half], uf[..., half:]
    return jnp.concatenate([u1 * cos - u2 * sin, u2 * cos + u1 * sin], axis=-1).astype(u.dtype)


def causal_depthwise_conv(u, w, b):
    k, c = w.shape
    y = lax.conv_general_dilated(u, w[:, None, :].astype(u.dtype), window_strides=(1,),
                                 padding=[(k - 1, 0)], dimension_numbers=("NWC", "WIO", "NWC"),
                                 feature_group_count=c)
    return y + b.astype(u.dtype)


def segsum(a):
    t = a.shape[-1]
    cs = jnp.cumsum(a, axis=-1)
    seg = cs[..., :, None] - cs[..., None, :]
    tril = jnp.tril(jnp.ones((t, t), dtype=bool))
    return jnp.where(tril, seg, -jnp.inf)


def ssd_chunked(xh, dt, a_head, bm, cm, chunk):
    bsz, s, h, p = xh.shape
    g, n = bm.shape[2], bm.shape[3]
    r = h // g
    c = s // chunk
    x = xh.reshape(bsz, c, chunk, g, r, p)
    dtc = dt.reshape(bsz, c, chunk, g, r)
    bc = bm.reshape(bsz, c, chunk, g, n)
    cc = cm.reshape(bsz, c, chunk, g, n)
    xdt = x * dtc[..., None]
    a = jnp.moveaxis(dtc * a_head.reshape(g, r), 2, -1)
    a_cs = jnp.cumsum(a, axis=-1)
    decay = jnp.exp(segsum(a))
    cb = jnp.einsum("bclgn,bcsgn->bcgls", cc, bc)
    y_diag = jnp.einsum("bcgrls,bcsgrp->bclgrp", cb[:, :, :, None] * decay, xdt)
    decay_states = jnp.exp(a_cs[..., -1:] - a_cs)
    states = jnp.einsum("bclgn,bcgrl,bclgrp->bcgrpn", bc, decay_states, xdt)
    chunk_decay = jnp.exp(a_cs[..., -1])

    def step(carry, inp):
        st, dec = inp
        return dec[..., None, None] * carry + st, carry

    init = jnp.zeros((bsz, g, r, p, n), jnp.float32)
    _, prev = lax.scan(step, init, (jnp.moveaxis(states, 1, 0), jnp.moveaxis(chunk_decay, 1, 0)))
    prev = jnp.moveaxis(prev, 0, 1)
    y_off = jnp.einsum("bclgn,bcgrpn,bcgrl->bclgrp", cc, prev, jnp.exp(a_cs))
    return (y_diag + y_off).reshape(bsz, s, h, p)


def mamba2_branch(z, xbc, dt_raw, conv_w, conv_b, dt_bias, a_log, d_skip, norm_w):
    bsz, s, _ = xbc.shape
    xbc = jax.nn.silu(causal_depthwise_conv(xbc, conv_w, conv_b))
    xs, bm, cm = jnp.split(xbc, [SSM_D_INNER, SSM_D_INNER + SSM_GROUPS * SSM_STATE], axis=-1)
    xh = xs.reshape(bsz, s, SSM_HEADS, SSM_HEAD_DIM).astype(jnp.float32)
    bm = bm.reshape(bsz, s, SSM_GROUPS, SSM_STATE).astype(jnp.float32)
    cm = cm.reshape(bsz, s, SSM_GROUPS, SSM_STATE).astype(jnp.float32)
    dt = jax.nn.softplus(dt_raw.astype(jnp.float32) + dt_bias.astype(jnp.float32))
    a_head = -jnp.exp(a_log.astype(jnp.float32))
    chunk = math.gcd(SSM_CHUNK, s)
    y = ssd_chunked(xh, dt, a_head, bm, cm, chunk) + d_skip.astype(jnp.float32)[:, None] * xh
    y = y.reshape(bsz, s, SSM_D_INNER).astype(z.dtype)
    return rms_norm(y * jax.nn.silu(z), norm_w)


def compress_blocks(u, pos_emb, w1, b1, w2):
    bsz, s = u.shape[:2]
    nc = (s - CMP_BLOCK) // CMP_STRIDE + 1
    idx = jnp.arange(nc)[:, None] * CMP_STRIDE + jnp.arange(CMP_BLOCK)[None, :]
    blk = u[:, idx] + pos_emb[None, None, :, None, :]
    blk = blk.transpose(0, 1, 3, 2, 4).reshape(bsz, nc, ATTN_KV_GROUPS, CMP_BLOCK * ATTN_HEAD_DIM)
    return jax.nn.gelu(blk @ w1 + b1) @ w2


def overlap_matrix(nc, ns):
    c0 = np.arange(nc) * CMP_STRIDE
    c1 = c0 + CMP_BLOCK
    s0 = np.arange(ns) * SLC_BLOCK
    s1 = s0 + SLC_BLOCK
    return jnp.asarray(((c0[:, None] < s1[None, :]) & (c1[:, None] > s0[None, :])).astype(np.float32))


def masked_softmax(s, mask):
    sf = jnp.where(mask, s.astype(jnp.float32), NEG)
    return jax.nn.softmax(sf, axis=-1) * mask


def nsa_branch(q, kv, gate_raw, cmp_pos_k, cmp_pos_v, ck_w1, ck_b1, ck_w2, cv_w1, cv_b1, cv_w2):
    bsz, s = q.shape[:2]
    g, dk = ATTN_KV_GROUPS, ATTN_HEAD_DIM
    r = ATTN_HEADS // g
    pos = jnp.arange(s, dtype=jnp.float32)
    q = rope(q.reshape(bsz, s, ATTN_HEADS, dk), pos) * (dk ** -0.5)
    q = q.reshape(bsz, s, g, r, dk)
    k_cmp, v_cmp, k_slc, v_slc, k_win, v_win = [t.reshape(bsz, s, g, dk) for t in jnp.split(kv, 6, axis=-1)]
    k_cmp, k_slc, k_win = rope(k_cmp, pos), rope(k_slc, pos), rope(k_win, pos)

    k_c = compress_blocks(k_cmp, cmp_pos_k, ck_w1, ck_b1, ck_w2)
    v_c = compress_blocks(v_cmp, cmp_pos_v, cv_w1, cv_b1, cv_w2)
    nc = k_c.shape[1]
    cmp_end = jnp.arange(nc) * CMP_STRIDE + CMP_BLOCK - 1

    ns = s // SLC_BLOCK
    top_n = min(SLC_TOP_N, ns)
    ks_blk = k_slc.reshape(bsz, ns, SLC_BLOCK, g, dk).transpose(0, 3, 1, 2, 4)
    vs_blk = v_slc.reshape(bsz, ns, SLC_BLOCK, g, dk).transpose(0, 3, 1, 2, 4)
    cmp_to_slc = overlap_matrix(nc, ns)
    blk_ids = jnp.arange(ns)
    bi = jnp.arange(bsz)[:, None, None, None]
    gi = jnp.arange(g)[None, None, :, None]

    k_win_pad = jnp.pad(k_win, ((0, 0), (WINDOW, 0), (0, 0), (0, 0)))
    v_win_pad = jnp.pad(v_win, ((0, 0), (WINDOW, 0), (0, 0), (0, 0)))

    gates = jax.nn.sigmoid(gate_raw.astype(jnp.float32)).reshape(bsz, s, g, r, 3)

    def attend_block(qb_idx):
        s0 = qb_idx * Q_BLOCK
        qb = lax.dynamic_slice_in_dim(q, s0, Q_BLOCK, axis=1)
        gb = lax.dynamic_slice_in_dim(gates, s0, Q_BLOCK, axis=1)
        t = s0 + jnp.arange(Q_BLOCK)

        m_c = (cmp_end[None, :] <= t[:, None])[None, :, None, None, :]
        p_c = masked_softmax(jnp.einsum("bqgrd,bkgd->bqgrk", qb, k_c), m_c)
        o_c = jnp.einsum("bqgrk,bkgd->bqgrd", p_c.astype(v_c.dtype), v_c)

        imp = jnp.einsum("bqgk,kj->bqgj", p_c.sum(axis=3), cmp_to_slc)
        cur = t // SLC_BLOCK
        future = (blk_ids[None, :] * SLC_BLOCK > t[:, None])[None, :, None, :]
        forced = ((blk_ids[None, :] == 0) | (blk_ids[None, :] == cur[:, None])
                  | (blk_ids[None, :] == cur[:, None] - 1))[None, :, None, :]
        imp = jnp.where(future, -jnp.inf, imp)
        imp = jnp.where(forced, jnp.inf, imp)
        _, sel = lax.top_k(imp, top_n)
        ks = ks_blk[bi, gi, sel].reshape(bsz, Q_BLOCK, g, top_n * SLC_BLOCK, dk)
        vs = vs_blk[bi, gi, sel].reshape(bsz, Q_BLOCK, g, top_n * SLC_BLOCK, dk)
        pos_s = (sel[..., None] * SLC_BLOCK + jnp.arange(SLC_BLOCK)).reshape(bsz, Q_BLOCK, g, top_n * SLC_BLOCK)
        m_s = (pos_s <= t[None, :, None, None])[:, :, :, None, :]
        p_s = masked_softmax(jnp.einsum("bqgrd,bqgmd->bqgrm", qb, ks), m_s)
        o_s = jnp.einsum("bqgrm,bqgmd->bqgrd", p_s.astype(vs.dtype), vs)

        kw = lax.dynamic_slice_in_dim(k_win_pad, s0, Q_BLOCK + WINDOW, axis=1)
        vw = lax.dynamic_slice_in_dim(v_win_pad, s0, Q_BLOCK + WINDOW, axis=1)
        pos_w = s0 - WINDOW + jnp.arange(Q_BLOCK + WINDOW)
        m_w = ((pos_w[None, :] <= t[:, None]) & (pos_w[None, :] > t[:, None] - WINDOW)
               & (pos_w[None, :] >= 0))[None, :, None, None, :]
        p_w = masked_softmax(jnp.einsum("bqgrd,bkgd->bqgrk", qb, kw), m_w)
        o_w = jnp.einsum("bqgrk,bkgd->bqgrd", p_w.astype(vw.dtype), vw)

        o = gb[..., 0:1] * o_c + gb[..., 1:2] * o_s + gb[..., 2:3] * o_w
        return o.astype(q.dtype)

    out = lax.map(attend_block, jnp.arange(s // Q_BLOCK))
    return jnp.moveaxis(out, 0, 1).reshape(bsz, s, ATTN_WIDTH)


def setup_inputs(seed: int = 0) -> dict:
    key = jax.random.key(seed)
    ks = jax.random.split(key, 24)
    f32 = jnp.float32

    def nrm(k, shape, scale):
        return jax.random.normal(k, shape, f32) * scale

    fan_cmp = CMP_BLOCK * ATTN_HEAD_DIM
    dt0 = jnp.exp(jax.random.uniform(ks[5], (DEPTH, SSM_HEADS), f32, math.log(1e-3), math.log(1e-1)))
    return {
        "x": nrm(ks[0], (BATCH, SEQ, D_MODEL), 1.0),
        "norm_w": 1.0 + nrm(ks[1], (DEPTH, D_MODEL), 0.02),
        "w_in": nrm(ks[2], (DEPTH, D_MODEL, IN_TOTAL), D_MODEL ** -0.5),
        "conv_w": nrm(ks[3], (DEPTH, SSM_CONV, SSM_CONV_DIM), SSM_CONV ** -0.5),
        "conv_b": nrm(ks[4], (DEPTH, SSM_CONV_DIM), 0.01),
        "dt_bias": dt0 + jnp.log(-jnp.expm1(-dt0)),
        "a_log": jnp.log(jax.random.uniform(ks[6], (DEPTH, SSM_HEADS), f32, 1.0, 16.0)),
        "d_skip": 1.0 + nrm(ks[7], (DEPTH, SSM_HEADS), 0.1),
        "ssm_norm_w": 1.0 + nrm(ks[8], (DEPTH, SSM_D_INNER), 0.02),
        "cmp_pos_k": nrm(ks[9], (DEPTH, CMP_BLOCK, ATTN_HEAD_DIM), 0.1),
        "cmp_pos_v": nrm(ks[10], (DEPTH, CMP_BLOCK, ATTN_HEAD_DIM), 0.1),
        "cmp_k_w1": nrm(ks[11], (DEPTH, fan_cmp, CMP_HIDDEN), fan_cmp ** -0.5),
        "cmp_k_b1": nrm(ks[12], (DEPTH, CMP_HIDDEN), 0.01),
        "cmp_k_w2": nrm(ks[13], (DEPTH, CMP_HIDDEN, ATTN_HEAD_DIM), CMP_HIDDEN ** -0.5),
        "cmp_v_w1": nrm(ks[14], (DEPTH, fan_cmp, CMP_HIDDEN), fan_cmp ** -0.5),
        "cmp_v_b1": nrm(ks[15], (DEPTH, CMP_HIDDEN), 0.01),
        "cmp_v_w2": nrm(ks[16], (DEPTH, CMP_HIDDEN, ATTN_HEAD_DIM), CMP_HIDDEN ** -0.5),
        "w_branch_ssm": nrm(ks[17], (DEPTH, SSM_D_INNER, D_MODEL), SSM_D_INNER ** -0.5),
        "w_branch_attn": nrm(ks[18], (DEPTH, ATTN_WIDTH, D_MODEL), ATTN_WIDTH ** -0.5),
        "w_out": nrm(ks[19], (DEPTH, D_MODEL, D_MODEL), D_MODEL ** -0.5),
        "final_norm_w": 1.0 + nrm(ks[20], (D_MODEL,), 0.02),
    }


def reference(x, norm_w, w_in, conv_w, conv_b, dt_bias, a_log, d_skip, ssm_norm_w,
              cmp_pos_k, cmp_pos_v, cmp_k_w1, cmp_k_b1, cmp_k_w2, cmp_v_w1, cmp_v_b1, cmp_v_w2,
              w_branch_ssm, w_branch_attn, w_out, final_norm_w):
    offsets = np.cumsum(IN_SPLITS)[:-1].tolist()
    for layer in range(DEPTH):
        h = rms_norm(x, norm_w[layer])
        proj = h @ w_in[layer]
        z_ssm, xbc, dt_raw, q, kv, nsa_gate, z_attn, g_ssm, g_attn = jnp.split(proj, offsets, axis=-1)
        y_ssm = mamba2_branch(z_ssm, xbc, dt_raw, conv_w[layer], conv_b[layer], dt_bias[layer],
                              a_log[layer], d_skip[layer], ssm_norm_w[layer]) @ w_branch_ssm[layer]
        o = nsa_branch(q, kv, nsa_gate, cmp_pos_k[layer], cmp_pos_v[layer],
                       cmp_k_w1[layer], cmp_k_b1[layer], cmp_k_w2[layer],
                       cmp_v_w1[layer], cmp_v_b1[layer], cmp_v_w2[layer])
        y_attn = (o * jax.nn.silu(z_attn)) @ w_branch_attn[layer]
        merged = jax.nn.sigmoid(g_ssm) * y_ssm + jax.nn.sigmoid(g_attn) * y_attn
        x = x + merged @ w_out[layer]
    return rms_norm(x, final_norm_w)
```

```python
import functools
import math

import numpy as np
import jax
import jax.numpy as jnp
from jax import lax
from jax.experimental import pallas as pl
from jax.experimental.pallas import tpu as pltpu

F32 = jnp.float32
BF16 = jnp.bfloat16

EPS = 1e-6
ROPE_THETA = 10000.0
NEG = -1e30

D_MODEL = 1024
SSM_HEADS = 16
SSM_HEAD_DIM = 64
SSM_D_INNER = 1024
SSM_GROUPS = 4
SSM_STATE = 128
SSM_CONV = 4
SSM_CHUNK = 256
SSM_CONV_DIM = 2048
ATTN_HEADS = 16
ATTN_GROUPS = 4
HEAD_DIM = 64
ATTN_WIDTH = 1024
KV_WIDTH = 256
CMP_BLOCK = 32
CMP_STRIDE = 16
CMP_HIDDEN = 256
SLC_BLOCK = 64
SLC_TOP_N = 16
WINDOW = 512
Q_TILE = 128
KEY_TILE = 256
TOK_TILE = 256
HEADS_PER_GROUP = ATTN_HEADS // ATTN_GROUPS
QCOLS = HEADS_PER_GROUP * Q_TILE

SLAB_XBC = 0
SLAB_ZSSM = 2048
SLAB_Q = 3072
SLAB_ZATT = 4096
SLAB_GSSM = 5120
SLAB_GATT = 6144
SLAB_KV = 7168
SLAB_SMALL = 8704
SLAB_W = 8832
GATE_LANE0 = SSM_HEADS

VMEM_LIMIT = 56 * 1024 * 1024


def _cparams(n_grid):
    return pltpu.CompilerParams(dimension_semantics=("arbitrary",) * n_grid,
                                vmem_limit_bytes=VMEM_LIMIT)


def _silu(x):
    return x * jax.nn.sigmoid(x)


def _proj_kernel(x_ref, nw_ref, w_ref, o_ref):
    x = x_ref[...]
    ms = jnp.mean(x * x, axis=-1, keepdims=True)
    h = (x * lax.rsqrt(ms + EPS)) * nw_ref[...]
    o_ref[...] = jnp.dot(h.astype(BF16), w_ref[...], preferred_element_type=F32)


def _proj(x2, norm_w, w_slab):
    t = x2.shape[0]
    tm, tn = 512, SLAB_W // 3
    return pl.pallas_call(
        _proj_kernel,
        out_shape=jax.ShapeDtypeStruct((t, SLAB_W), F32),
        grid=(SLAB_W // tn, t // tm),
        in_specs=[pl.BlockSpec((tm, D_MODEL), lambda j, i: (i, 0)),
                  pl.BlockSpec((1, D_MODEL), lambda j, i: (0, 0)),
                  pl.BlockSpec((D_MODEL, tn), lambda j, i: (0, j))],
        out_specs=pl.BlockSpec((tm, tn), lambda j, i: (i, j)),
        compiler_params=_cparams(2),
        name="proj",
    )(x2, norm_w, w_slab)


def _split3(v):
    v1 = v.astype(BF16)
    r1 = v - v1.astype(F32)
    v2 = r1.astype(BF16)
    v3 = (r1 - v2.astype(F32)).astype(BF16)
    return v1, v2, v3


def _dot3(parts, rhs, lhs_side=True):
    out = None
    for p in parts:
        d = (jnp.dot(p, rhs, preferred_element_type=F32) if lhs_side
             else jnp.dot(rhs, p, preferred_element_type=F32))
        out = d if out is None else out + d
    return out


def _ssd_kernel(xbc_ref, prev_ref, small_ref, z_ref, g_ref, convw_ref, convb_ref, dtb_ref, alog_ref,
                dskip_ref, nw_ref, wb_ref, tri_ref, exp_ref, o_ref, st_ref, ext_ref):
    c = pl.program_id(1)
    L = SSM_CHUNK

    @pl.when(c == 0)
    def _():
        st_ref[...] = jnp.zeros_like(st_ref)

    ext_ref[0:8, :] = jnp.where(c > 0, prev_ref[...], 0.0)
    ext_ref[8:8 + L, :] = xbc_ref[...]
    acc = jnp.broadcast_to(convb_ref[...], (L, SSM_CONV_DIM))
    for k in range(SSM_CONV):
        r0 = 8 - (SSM_CONV - 1) + k
        acc = acc + convw_ref[k:k + 1, :] * ext_ref[r0:r0 + L, :]
    xbc = _silu(acc)
    xs = xbc[:, :SSM_D_INNER]
    bm = xbc[:, SSM_D_INNER:SSM_D_INNER + SSM_GROUPS * SSM_STATE]
    cm = xbc[:, SSM_D_INNER + SSM_GROUPS * SSM_STATE:]

    lane = lax.broadcasted_iota(jnp.int32, (L, 128), 1)
    head_lane = lane < SSM_HEADS
    u = small_ref[...] + dtb_ref[...]
    dt = jnp.maximum(u, 0.0) + jnp.log1p(jnp.exp(-jnp.abs(u)))
    dt = jnp.where(head_lane, dt, 0.0)
    a = dt * -jnp.exp(alog_ref[...])
    cs = _dot3(_split3(a), tri_ref[...], lhs_side=False)
    cs_last = cs[L - 1:L, :]
    ecs = jnp.exp(cs)
    ds = jnp.exp(cs_last - cs)
    packed = (dt + pltpu.roll(jnp.where(head_lane, ecs, 0.0), SSM_HEADS, 1)
              + pltpu.roll(jnp.where(head_lane, ds, 0.0), 2 * SSM_HEADS, 1))
    expanded = _dot3(_split3(packed), exp_ref[...])
    dt_e = expanded[:, :SSM_D_INNER]
    ecs_e = expanded[:, SSM_D_INNER:2 * SSM_D_INNER]
    ds_e = expanded[:, 2 * SSM_D_INNER:]
    xdt = xs * dt_e
    xds = xdt * ds_e
    cs_t = cs.T

    row = lax.broadcasted_iota(jnp.int32, (L, L), 0)
    col = lax.broadcasted_iota(jnp.int32, (L, L), 1)
    tril = row >= col
    half = lax.broadcasted_iota(jnp.int32, (L, 128), 1) < SSM_HEAD_DIM
    gw = SSM_D_INNER // SSM_GROUPS

    y_groups = []
    for g in range(SSM_GROUPS):
        bg = bm[:, g * SSM_STATE:(g + 1) * SSM_STATE]
        cg = cm[:, g * SSM_STATE:(g + 1) * SSM_STATE].astype(BF16)
        bg_t = bg.T.astype(BF16)
        cb = jnp.dot(cg, bg_t, preferred_element_type=F32)
        st = st_ref[g]
        y_off = jnp.dot(cg, st.astype(BF16), preferred_element_type=F32) * ecs_e[:, g * gw:(g + 1) * gw]
        st_ref[g] = (ecs_e[L - 1:L, g * gw:(g + 1) * gw] * st
                     + jnp.dot(bg_t, xds[:, g * gw:(g + 1) * gw].astype(BF16), preferred_element_type=F32))
        pairs = []
        for pr in range(2):
            rhs = xdt[:, g * gw + pr * 128:g * gw + (pr + 1) * 128].astype(BF16)
            ys = []
            for hh in range(2):
                h = g * 4 + pr * 2 + hh
                seg = cs[:, h:h + 1] - cs_t[h:h + 1, :]
                dec = jnp.exp(jnp.where(tril, seg, NEG))
                ys.append(jnp.dot((cb * dec).astype(BF16), rhs, preferred_element_type=F32))
            pairs.append(jnp.where(half, ys[0], ys[1]))
        y_groups.append(jnp.concatenate(pairs, axis=1) + y_off)
    y = jnp.concatenate(y_groups, axis=1) + dskip_ref[...] * xs

    v = y * _silu(z_ref[...])
    ms = jnp.mean(v * v, axis=-1, keepdims=True)
    vn = (v * lax.rsqrt(ms + EPS)) * nw_ref[...]
    ys = jnp.dot(vn.astype(BF16), wb_ref[...], preferred_element_type=F32)
    o_ref[...] = jax.nn.sigmoid(g_ref[...]) * ys


def _ssd(slab, bsz, s, conv_w, conv_b, dtb, alog, dskip_e, nw, wb):
    nt = s // SSM_CHUNK
    L = SSM_CHUNK
    tri = jnp.asarray(np.tril(np.ones((L, L), np.float32)), BF16)
    e = np.zeros((128, 3 * SSM_D_INNER), np.float32)
    for k in range(3):
        for h in range(SSM_HEADS):
            e[k * SSM_HEADS + h, k * SSM_D_INNER + h * SSM_HEAD_DIM:k * SSM_D_INNER + (h + 1) * SSM_HEAD_DIM] = 1.0
    e = jnp.asarray(e, BF16)
    rows = lambda b, c: b * nt + c
    const = lambda shape: pl.BlockSpec(shape, lambda b, c: (0,) * len(shape))
    return pl.pallas_call(
        _ssd_kernel,
        out_shape=jax.ShapeDtypeStruct((bsz * s, SSM_D_INNER), F32),
        grid=(bsz, nt),
        in_specs=[
            pl.BlockSpec((L, SSM_CONV_DIM), lambda b, c: (rows(b, c), SLAB_XBC // SSM_CONV_DIM)),
            pl.BlockSpec((8, SSM_CONV_DIM), lambda b, c: (jnp.maximum(rows(b, c) * (L // 8) - 1, 0), 0)),
            pl.BlockSpec((L, 128), lambda b, c: (rows(b, c), SLAB_SMALL // 128)),
            pl.BlockSpec((L, SSM_D_INNER), lambda b, c: (rows(b, c), SLAB_ZSSM // SSM_D_INNER)),
            pl.BlockSpec((L, D_MODEL), lambda b, c: (rows(b, c), SLAB_GSSM // D_MODEL)),
            const((SSM_CONV, SSM_CONV_DIM)), const((1, SSM_CONV_DIM)), const((1, 128)), const((1, 128)),
            const((1, SSM_D_INNER)), const((1, SSM_D_INNER)), const((SSM_D_INNER, D_MODEL)),
            const((L, L)), const((128, 3 * SSM_D_INNER)),
        ],
        out_specs=pl.BlockSpec((L, D_MODEL), lambda b, c: (rows(b, c), 0)),
        scratch_shapes=[pltpu.VMEM((SSM_GROUPS, SSM_STATE, SSM_D_INNER // SSM_GROUPS), F32),
                        pltpu.VMEM((L + 8, SSM_CONV_DIM), F32)],
        compiler_params=_cparams(2),
        name="ssd",
    )(slab, slab, slab, slab, slab, conv_w, conv_b, dtb, alog, dskip_e, nw, wb, tri, e)


def _prep_kernel(q_ref, kc_ref, ks_ref, vs_ref, kw_ref, vw_ref, small_ref, cos_ref, sin_ref,
                 qt_ref, kso_ref, vst_ref, kwo_ref, vwt_ref, kcr_ref, gt_ref):
    cos = cos_ref[...]
    sin = sin_ref[...]

    def rope(x):
        w = x.shape[1]
        cf = jnp.concatenate([cos] * (w // 128), axis=1)
        sf = jnp.concatenate([sin] * (w // 128), axis=1)
        lane = lax.broadcasted_iota(jnp.int32, x.shape, 1)
        first = (lane % HEAD_DIM) < (HEAD_DIM // 2)
        swapped = jnp.where(first, pltpu.roll(x, w - HEAD_DIM // 2, 1), pltpu.roll(x, HEAD_DIM // 2, 1))
        return x * cf + swapped * sf

    halves = TOK_TILE // Q_TILE
    low64 = lax.broadcasted_iota(jnp.int32, (TOK_TILE, 128), 1) < HEAD_DIM

    def group_cols(x, g):
        r = x if g == 0 else pltpu.roll(x, KV_WIDTH - HEAD_DIM * g, 1)
        return jnp.where(low64, r[:, :128], 0.0)

    qr = rope(q_ref[...]) * (HEAD_DIM ** -0.5)
    zeros_q = jnp.zeros((128 - HEAD_DIM, QCOLS), BF16)
    for hf in range(halves):
        q_t = qr[hf * Q_TILE:(hf + 1) * Q_TILE, :].T
        for g in range(ATTN_GROUPS):
            base = g * HEADS_PER_GROUP * HEAD_DIM
            blk = jnp.concatenate([q_t[base + h * HEAD_DIM:base + (h + 1) * HEAD_DIM, :]
                                   for h in range(HEADS_PER_GROUP)], axis=1)
            qt_ref[0, g, hf] = jnp.concatenate([blk.astype(BF16), zeros_q], axis=0)

    ksr = rope(ks_ref[...])
    vs_t = vs_ref[...].T
    for g in range(ATTN_GROUPS):
        kso_ref[0, g, 0] = group_cols(ksr, g).astype(BF16)
        vst_ref[0, g, 0] = vs_t[g * HEAD_DIM:(g + 1) * HEAD_DIM, :].astype(BF16)

    kwr = rope(kw_ref[...])
    vw_t = vw_ref[...].T
    for g in range(ATTN_GROUPS):
        kg = group_cols(kwr, g).astype(BF16)
        for hf in range(halves):
            kwo_ref[0, g, hf] = kg[hf * Q_TILE:(hf + 1) * Q_TILE, :]
            vwt_ref[0, g, hf] = vw_t[g * HEAD_DIM:(g + 1) * HEAD_DIM, hf * Q_TILE:(hf + 1) * Q_TILE].astype(BF16)

    kcr_ref[...] = rope(kc_ref[...])

    gates_t = jax.nn.sigmoid(small_ref[...]).T
    zeros_g = jnp.zeros((8 - 3, QCOLS), F32)
    for hf in range(halves):
        for g in range(ATTN_GROUPS):
            rows_c = []
            for cc in range(3):
                rows_c.append(jnp.concatenate(
                    [gates_t[GATE_LANE0 + (g * HEADS_PER_GROUP + r) * 3 + cc:GATE_LANE0 + (g * HEADS_PER_GROUP + r) * 3 + cc + 1,
                             hf * Q_TILE:(hf + 1) * Q_TILE] for r in range(HEADS_PER_GROUP)], axis=1))
            gt_ref[0, g, hf] = jnp.concatenate(rows_c + [zeros_g], axis=0)


def _prep(slab, bsz, s, cos_t, sin_t):
    nt = s // TOK_TILE
    nq = s // Q_TILE
    halves = TOK_TILE // Q_TILE
    G = ATTN_GROUPS
    rows = lambda b, i: b * nt + i
    kvspec = lambda k: pl.BlockSpec((TOK_TILE, KV_WIDTH), lambda b, i: (rows(b, i), SLAB_KV // KV_WIDTH + k))
    out5 = lambda blk: pl.BlockSpec(blk, lambda b, i: (b, 0, i, 0, 0))
    return pl.pallas_call(
        _prep_kernel,
        out_shape=[
            jax.ShapeDtypeStruct((bsz, G, nq, 128, QCOLS), BF16),
            jax.ShapeDtypeStruct((bsz, G, nt, KEY_TILE, 128), BF16),
            jax.ShapeDtypeStruct((bsz, G, nt, HEAD_DIM, KEY_TILE), BF16),
            jax.ShapeDtypeStruct((bsz, G, nq, Q_TILE, 128), BF16),
            jax.ShapeDtypeStruct((bsz, G, nq, HEAD_DIM, Q_TILE), BF16),
            jax.ShapeDtypeStruct((bsz * s, KV_WIDTH), F32),
            jax.ShapeDtypeStruct((bsz, G, nq, 8, QCOLS), F32),
        ],
        grid=(bsz, nt),
        in_specs=[
            pl.BlockSpec((TOK_TILE, ATTN_WIDTH), lambda b, i: (rows(b, i), SLAB_Q // ATTN_WIDTH)),
            kvspec(0), kvspec(2), kvspec(3), kvspec(4), kvspec(5),
            pl.BlockSpec((TOK_TILE, 128), lambda b, i: (rows(b, i), SLAB_SMALL // 128)),
            pl.BlockSpec((TOK_TILE, 128), lambda b, i: (i, 0)),
            pl.BlockSpec((TOK_TILE, 128), lambda b, i: (i, 0)),
        ],
        out_specs=[
            out5((1, G, halves, 128, QCOLS)),
            out5((1, G, 1, KEY_TILE, 128)),
            out5((1, G, 1, HEAD_DIM, KEY_TILE)),
            out5((1, G, halves, Q_TILE, 128)),
            out5((1, G, halves, HEAD_DIM, Q_TILE)),
            pl.BlockSpec((TOK_TILE, KV_WIDTH), lambda b, i: (rows(b, i), 0)),
            out5((1, G, halves, 8, QCOLS)),
        ],
        compiler_params=_cparams(2),
        name="prep",
    )(slab, slab, slab, slab, slab, slab, slab, cos_t, sin_t)


def _compress_kernel(h_ref, pos_ref, w1_ref, b1_ref, w2_ref, nat_ref, tr_ref):
    h = h_ref[0, 0, 0]
    nh = h.shape[0]
    w1 = w1_ref[0]
    a = jnp.dot((h + pos_ref[0, 0]).astype(BF16), w1[:, :CMP_HIDDEN], preferred_element_type=F32)
    b = jnp.dot((h + pos_ref[0, 1]).astype(BF16), w1[:, CMP_HIDDEN:], preferred_element_type=F32)
    hid = a + pltpu.roll(b, nh - 1, 0) + b1_ref[0]
    out = jnp.dot(jax.nn.gelu(hid).astype(BF16), w2_ref[0], preferred_element_type=F32)
    nat_ref[0, 0, 0] = out
    tr_ref[0, 0, 0] = out.T


def _compress(hkv, pos, w1, b1, w2):
    _, bsz, G, nh, _ = hkv.shape
    return pl.pallas_call(
        _compress_kernel,
        out_shape=[jax.ShapeDtypeStruct((2, bsz, G, nh, 128), F32),
                   jax.ShapeDtypeStruct((2, bsz, G, 128, nh), F32)],
        grid=(2, bsz, G),
        in_specs=[
            pl.BlockSpec((1, 1, 1, nh, CMP_STRIDE * HEAD_DIM), lambda k, b, g: (k, b, g, 0, 0)),
            pl.BlockSpec((1, 2, 1, CMP_STRIDE * HEAD_DIM), lambda k, b, g: (k, 0, 0, 0)),
            pl.BlockSpec((1, CMP_STRIDE * HEAD_DIM, 2 * CMP_HIDDEN), lambda k, b, g: (k, 0, 0)),
            pl.BlockSpec((1, 1, CMP_HIDDEN), lambda k, b, g: (k, 0, 0)),
            pl.BlockSpec((1, CMP_HIDDEN, 128), lambda k, b, g: (k, 0, 0)),
        ],
        out_specs=[pl.BlockSpec((1, 1, 1, nh, 128), lambda k, b, g: (k, b, g, 0, 0)),
                   pl.BlockSpec((1, 1, 1, 128, nh), lambda k, b, g: (k, b, g, 0, 0))],
        compiler_params=_cparams(3),
        name="compress",
    )(hkv, pos, w1, b1, w2)


def _online_update(carry, s, v_t):
    m, l, acc = carry
    m_new = jnp.maximum(m, jnp.max(s, axis=0, keepdims=True))
    alpha = jnp.exp(m - m_new)
    p = jnp.exp(s - m_new)
    l_new = alpha * l + jnp.sum(p, axis=0, keepdims=True)
    acc_new = alpha * acc + jnp.dot(v_t, p.astype(BF16), preferred_element_type=F32)
    return m_new, l_new, acc_new


def _attn_kernel(qt_ref, kc_ref, vct_ref, ks_ref, vst_ref, kw_ref, vwt_ref, g_ref, o_ref, p4_ref, bias_ref,
                 *, nh, ns, top_n):
    qi = pl.program_id(2)
    s0 = qi * Q_TILE
    q_t = qt_ref[0, 0, 0]
    t_row = s0 + (lax.broadcasted_iota(jnp.int32, (1, QCOLS), 1) & (Q_TILE - 1))

    sc = jnp.dot(kc_ref[0, 0, 0].astype(BF16), q_t, preferred_element_type=F32)
    cend = lax.broadcasted_iota(jnp.int32, (nh, QCOLS), 0) * CMP_STRIDE + (CMP_BLOCK - 1)
    cmask = cend <= t_row
    sm = jnp.where(cmask, sc, NEG)
    mc = jnp.max(sm, axis=0, keepdims=True)
    pc = jnp.where(cmask, jnp.exp(sm - mc), 0.0)
    lc = jnp.sum(pc, axis=0, keepdims=True)
    pn = pc * jnp.where(lc > 0.0, 1.0 / lc, 0.0)
    o_c = jnp.dot(vct_ref[0, 0, 0][:HEAD_DIM, :].astype(BF16), pn.astype(BF16), preferred_element_type=F32)

    p4 = pn[:, 0:Q_TILE]
    for h in range(1, HEADS_PER_GROUP):
        p4 = p4 + pn[:, h * Q_TILE:(h + 1) * Q_TILE]
    p4_ref[0:8, :] = jnp.zeros((8, Q_TILE), F32)
    p4_ref[8:8 + nh, :] = p4
    imp = p4_ref[pl.ds(7, ns, stride=4), :]
    for cc in range(4):
        imp = imp + p4_ref[pl.ds(8 + cc, ns, stride=4), :]

    blk = lax.broadcasted_iota(jnp.int32, (ns, Q_TILE), 0)
    tok = s0 + lax.broadcasted_iota(jnp.int32, (ns, Q_TILE), 1)
    cur = lax.shift_right_logical(tok, int(math.log2(SLC_BLOCK)))
    future = blk * SLC_BLOCK > tok
    forced = (blk == 0) | (blk == cur) | (blk == cur - 1)
    val = jnp.where(forced, 3e38, jnp.where(future, -1.0, imp))

    def pick(_, carry):
        v, sel = carry
        best = jnp.max(v, axis=0, keepdims=True)
        first = jnp.min(jnp.where(v == best, blk, ns), axis=0, keepdims=True)
        hit = blk == first
        return jnp.where(hit, -2.0, v), jnp.where(hit, 1.0, sel)

    _, sel = lax.fori_loop(0, top_n, pick, (val, jnp.zeros((ns, Q_TILE), F32)))
    bias = jnp.where(sel > 0.0, 0.0, NEG)
    for kt in range(ns * SLC_BLOCK // KEY_TILE):
        bias_ref[kt, 0:KEY_TILE // SLC_BLOCK, :] = bias[kt * (KEY_TILE // SLC_BLOCK):(kt + 1) * (KEY_TILE // SLC_BLOCK), :]

    def sel_scores(kt, diag):
        s = jnp.dot(ks_ref[0, 0, kt], q_t, preferred_element_type=F32)
        b4 = bias_ref[kt, 0:KEY_TILE // SLC_BLOCK, :]
        rows = [jnp.broadcast_to(b4[j:j + 1, :], (SLC_BLOCK, Q_TILE)) for j in range(KEY_TILE // SLC_BLOCK)]
        b = jnp.concatenate(rows, axis=0)
        s = s + jnp.concatenate([b] * HEADS_PER_GROUP, axis=1)
        if diag:
            key = kt * KEY_TILE + lax.broadcasted_iota(jnp.int32, (KEY_TILE, QCOLS), 0)
            s = jnp.where(key <= t_row, s, NEG)
        return s

    init = (jnp.full((1, QCOLS), NEG, F32), jnp.zeros((1, QCOLS), F32), jnp.zeros((HEAD_DIM, QCOLS), F32))
    n_full = lax.shift_right_logical(qi, 1)
    assert KEY_TILE == 2 * Q_TILE

    def sel_body(kt, carry):
        return _online_update(carry, sel_scores(kt, False), vst_ref[0, 0, kt])

    carry = lax.fori_loop(0, n_full, sel_body, init)
    m_s, l_s, acc_s = _online_update(carry, sel_scores(n_full, True), vst_ref[0, 0, n_full])
    o_s = acc_s * (1.0 / l_s)

    kk = lax.broadcasted_iota(jnp.int32, (Q_TILE, QCOLS), 0)
    tl = lax.broadcasted_iota(jnp.int32, (Q_TILE, QCOLS), 1) & (Q_TILE - 1)
    carry = init
    n_back = WINDOW // Q_TILE
    for d in range(n_back + 1):
        kt = jnp.maximum(qi - d, 0)
        valid = jnp.full((Q_TILE, QCOLS), qi - d, jnp.int32) >= 0
        s = jnp.dot(kw_ref[0, 0, kt], q_t, preferred_element_type=F32)
        if d == 0:
            mask = kk <= tl
        elif d == n_back:
            mask = (kk > tl) & valid
        else:
            mask = valid
        carry = _online_update(carry, jnp.where(mask, s, NEG), vwt_ref[0, 0, kt])
    m_w, l_w, acc_w = carry
    o_w = acc_w * (1.0 / l_w)

    gts = g_ref[0, 0, 0]
    o_ref[0, 0, 0] = gts[0:1, :] * o_c + gts[1:2, :] * o_s + gts[2:3, :] * o_w


def _attn(q_t, kc_nat, vc_tr, ks, vst, kw, vwt, gts, s):
    bsz, G, nq = q_t.shape[:3]
    nh = s // CMP_STRIDE
    ns = s // SLC_BLOCK
    nkt = s // KEY_TILE
    top_n = min(SLC_TOP_N, ns)
    per_q = lambda blk: pl.BlockSpec(blk, lambda b, g, i: (b, g, i, 0, 0))
    per_bg = lambda blk: pl.BlockSpec(blk, lambda b, g, i: (b, g, 0, 0, 0))
    return pl.pallas_call(
        functools.partial(_attn_kernel, nh=nh, ns=ns, top_n=top_n),
        out_shape=jax.ShapeDtypeStruct((bsz, G, nq, HEAD_DIM, QCOLS), F32),
        grid=(bsz, G, nq),
        in_specs=[
            per_q((1, 1, 1, 128, QCOLS)),
            pl.BlockSpec((1, 1, 1, nh, 128), lambda b, g, i: (0, b, g, 0, 0)),
            pl.BlockSpec((1, 1, 1, 128, nh), lambda b, g, i: (1, b, g, 0, 0)),
            per_bg((1, 1, nkt, KEY_TILE, 128)),
            per_bg((1, 1, nkt, HEAD_DIM, KEY_TILE)),
            per_bg((1, 1, nq, Q_TILE, 128)),
            per_bg((1, 1, nq, HEAD_DIM, Q_TILE)),
            per_q((1, 1, 1, 8, QCOLS)),
        ],
        out_specs=per_q((1, 1, 1, HEAD_DIM, QCOLS)),
        scratch_shapes=[pltpu.VMEM((nh + 8, Q_TILE), F32),
                        pltpu.VMEM((nkt, 8, Q_TILE), F32)],
        compiler_params=_cparams(3),
        name="attn",
    )(q_t, kc_nat, vc_tr, ks, vst, kw, vwt, gts)


def _final_kernel(x_ref, gs_ref, ot_ref, z_ref, g_ref, wba_ref, wout_ref, fnw_ref, o_ref, *, last):
    halves = TOK_TILE // Q_TILE
    parts = []
    for hf in range(halves):
        rows = []
        for g in range(ATTN_GROUPS):
            blk = ot_ref[0, g, hf]
            for h in range(HEADS_PER_GROUP):
                rows.append(blk[:, h * Q_TILE:(h + 1) * Q_TILE])
        parts.append(jnp.concatenate(rows, axis=0).T)
    o = jnp.concatenate(parts, axis=0)
    y_attn = jnp.dot((o * _silu(z_ref[...])).astype(BF16), wba_ref[...], preferred_element_type=F32)
    merged = gs_ref[...] + jax.nn.sigmoid(g_ref[...]) * y_attn
    xo = x_ref[...] + jnp.dot(merged.astype(BF16), wout_ref[...], preferred_element_type=F32)
    if last:
        ms = jnp.mean(xo * xo, axis=-1, keepdims=True)
        xo = (xo * lax.rsqrt(ms + EPS)) * fnw_ref[...]
    o_ref[...] = xo


def _final(x2, gs, o_t, slab, wba, wout, fnw, bsz, s, last):
    nt = s // TOK_TILE
    halves = TOK_TILE // Q_TILE
    rows = lambda b, i: b * nt + i
    const = lambda shape: pl.BlockSpec(shape, lambda b, i: (0,) * len(shape))
    return pl.pallas_call(
        functools.partial(_final_kernel, last=last),
        out_shape=jax.ShapeDtypeStruct((bsz * s, D_MODEL), F32),
        grid=(bsz, nt),
        in_specs=[
            pl.BlockSpec((TOK_TILE, D_MODEL), lambda b, i: (rows(b, i), 0)),
            pl.BlockSpec((TOK_TILE, D_MODEL), lambda b, i: (rows(b, i), 0)),
            pl.BlockSpec((1, ATTN_GROUPS, halves, HEAD_DIM, QCOLS), lambda b, i: (b, 0, i, 0, 0)),
            pl.BlockSpec((TOK_TILE, ATTN_WIDTH), lambda b, i: (rows(b, i), SLAB_ZATT // ATTN_WIDTH)),
            pl.BlockSpec((TOK_TILE, D_MODEL), lambda b, i: (rows(b, i), SLAB_GATT // D_MODEL)),
            const((ATTN_WIDTH, D_MODEL)), const((D_MODEL, D_MODEL)), const((1, D_MODEL)),
        ],
        out_specs=pl.BlockSpec((TOK_TILE, D_MODEL), lambda b, i: (rows(b, i), 0)),
        compiler_params=_cparams(2),
        name="final",
    )(x2, gs, o_t, slab, slab, wba, wout, fnw)


def _slab_weights(w):
    o = np.cumsum([0, SSM_D_INNER, SSM_CONV_DIM, SSM_HEADS, ATTN_WIDTH, 6 * KV_WIDTH, 3 * ATTN_HEADS,
                   ATTN_WIDTH, D_MODEL, D_MODEL])
    z_ssm, xbc, dt, q, kv, gate, z_att, g_ssm, g_att = [w[:, o[i]:o[i + 1]] for i in range(9)]
    pad = jnp.zeros((w.shape[0], SLAB_W - int(o[-1])), w.dtype)
    return jnp.concatenate([xbc, z_ssm, q, z_att, g_ssm, g_att, kv, dt, gate, pad], axis=1).astype(BF16)


def _rope_tables(s):
    half = HEAD_DIM // 2
    inv_freq = 1.0 / (ROPE_THETA ** (jnp.arange(half, dtype=F32) * 2.0 / HEAD_DIM))
    ang = jnp.arange(s, dtype=F32)[:, None] * inv_freq[None, :]
    cos, sin = jnp.cos(ang), jnp.sin(ang)
    cos_t = jnp.concatenate([cos, cos] * (128 // HEAD_DIM), axis=1)
    sin_t = jnp.concatenate([-sin, sin] * (128 // HEAD_DIM), axis=1)
    return cos_t, sin_t


def _pad_lanes(v, width=128):
    v = v.reshape(1, -1)
    return jnp.concatenate([v, jnp.zeros((1, width - v.shape[1]), v.dtype)], axis=1)


def kernel(x, norm_w, w_in, conv_w, conv_b, dt_bias, a_log, d_skip, ssm_norm_w, cmp_pos_k, cmp_pos_v, cmp_k_w1,
           cmp_k_b1, cmp_k_w2, cmp_v_w1, cmp_v_b1, cmp_v_w2, w_branch_ssm, w_branch_attn, w_out, final_norm_w):
    bsz, s, _ = x.shape
    depth = norm_w.shape[0]
    G = ATTN_GROUPS
    nh = s // CMP_STRIDE
    cos_t, sin_t = _rope_tables(s)
    x2 = x.reshape(bsz * s, D_MODEL)
    for layer in range(depth):
        slab = _proj(x2, norm_w[layer].reshape(1, -1), _slab_weights(w_in[layer]))

        gs = _ssd(slab, bsz, s, conv_w[layer], conv_b[layer].reshape(1, -1), _pad_lanes(dt_bias[layer]),
                  _pad_lanes(a_log[layer]), jnp.repeat(d_skip[layer], SSM_HEAD_DIM).reshape(1, -1),
                  ssm_norm_w[layer].reshape(1, -1), w_branch_ssm[layer].astype(BF16))

        q_t, ks, vst, kw, vwt, kcr, gts = _prep(slab, bsz, s, cos_t, sin_t)

        vcmp = slab[:, SLAB_KV + KV_WIDTH:SLAB_KV + 2 * KV_WIDTH]
        halfblocks = lambda u: u.reshape(bsz, nh, CMP_STRIDE, G, HEAD_DIM).transpose(0, 3, 1, 2, 4).reshape(
            bsz, G, nh, CMP_STRIDE * HEAD_DIM)
        hkv = jnp.stack([halfblocks(kcr), halfblocks(vcmp)])
        pos = jnp.stack([cmp_pos_k[layer], cmp_pos_v[layer]]).reshape(2, 2, 1, CMP_STRIDE * HEAD_DIM)
        w1 = jnp.stack([cmp_k_w1[layer], cmp_v_w1[layer]])
        w1 = jnp.concatenate([w1[:, :CMP_STRIDE * HEAD_DIM], w1[:, CMP_STRIDE * HEAD_DIM:]], axis=2).astype(BF16)
        b1 = jnp.stack([cmp_k_b1[layer], cmp_v_b1[layer]]).reshape(2, 1, CMP_HIDDEN)
        w2 = jnp.stack([cmp_k_w2[layer], cmp_v_w2[layer]])
        w2 = jnp.concatenate([w2, jnp.zeros((2, CMP_HIDDEN, 128 - HEAD_DIM), w2.dtype)], axis=2).astype(BF16)
        c_nat, c_tr = _compress(hkv, pos, w1, b1, w2)

        o_t = _attn(q_t, c_nat, c_tr, ks, vst, kw, vwt, gts, s)

        x2 = _final(x2, gs, o_t, slab, w_branch_attn[layer].astype(BF16), w_out[layer].astype(BF16),
                    final_norm_w.reshape(1, -1), bsz, s, layer == depth - 1)
    return x2.reshape(bsz, s, D_MODEL)
```

```python
import functools
import math

import numpy as np
import jax
import jax.numpy as jnp
from jax import lax
from jax.experimental import pallas as pl
from jax.experimental.pallas import tpu as pltpu

F32 = jnp.float32
BF16 = jnp.bfloat16

EPS = 1e-6
ROPE_THETA = 10000.0
NEG = -1e30

D_MODEL = 1024
SSM_HEADS = 16
SSM_HEAD_DIM = 64
SSM_D_INNER = 1024
SSM_GROUPS = 4
SSM_STATE = 128
SSM_CONV = 4
SSM_CHUNK = 256
SSM_CONV_DIM = 2048
ATTN_HEADS = 16
ATTN_GROUPS = 4
HEAD_DIM = 64
ATTN_WIDTH = 1024
KV_WIDTH = 256
CMP_BLOCK = 32
CMP_STRIDE = 16
CMP_HIDDEN = 256
SLC_BLOCK = 64
SLC_TOP_N = 16
WINDOW = 512
Q_TILE = 128
KEY_TILE = 256
TOK_TILE = 256
HEADS_PER_GROUP = ATTN_HEADS // ATTN_GROUPS
QCOLS = HEADS_PER_GROUP * Q_TILE
SEL_UNROLL = 4
BF16_ROWS = 16
V_ROWS = HEAD_DIM + BF16_ROWS
LOG2E = math.log2(math.e)

SLAB_XBC = 0
SLAB_ZSSM = 2048
SLAB_Q = 3072
SLAB_ZATT = 4096
SLAB_GSSM = 5120
SLAB_GATT = 6144
SLAB_KV = 7168
SLAB_SMALL = 8704
SLAB_W = 8832
GATE_LANE0 = SSM_HEADS

VMEM_LIMIT = 56 * 1024 * 1024


def _cparams(n_grid):
    return pltpu.CompilerParams(dimension_semantics=("arbitrary",) * n_grid,
                                vmem_limit_bytes=VMEM_LIMIT)


def _silu(x):
    return x * jax.nn.sigmoid(x)


def _proj_kernel(x_ref, nw_ref, w_ref, o_ref):
    x = x_ref[...]
    ms = jnp.mean(x * x, axis=-1, keepdims=True)
    h = (x * lax.rsqrt(ms + EPS)) * nw_ref[...]
    o_ref[...] = jnp.dot(h.astype(BF16), w_ref[...], preferred_element_type=F32)


def _proj(x2, norm_w, w_slab):
    t = x2.shape[0]
    tm, tn = 512, SLAB_W // 3
    return pl.pallas_call(
        _proj_kernel,
        out_shape=jax.ShapeDtypeStruct((t, SLAB_W), F32),
        grid=(SLAB_W // tn, t // tm),
        in_specs=[pl.BlockSpec((tm, D_MODEL), lambda j, i: (i, 0)),
                  pl.BlockSpec((1, D_MODEL), lambda j, i: (0, 0)),
                  pl.BlockSpec((D_MODEL, tn), lambda j, i: (0, j))],
        out_specs=pl.BlockSpec((tm, tn), lambda j, i: (i, j)),
        compiler_params=_cparams(2),
        name="proj",
    )(x2, norm_w, w_slab)


def _split3(v):
    v1 = v.astype(BF16)
    r1 = v - v1.astype(F32)
    v2 = r1.astype(BF16)
    v3 = (r1 - v2.astype(F32)).astype(BF16)
    return v1, v2, v3


def _dot3(parts, rhs, lhs_side=True):
    out = None
    for p in parts:
        d = (jnp.dot(p, rhs, preferred_element_type=F32) if lhs_side
             else jnp.dot(rhs, p, preferred_element_type=F32))
        out = d if out is None else out + d
    return out


def _ssd_kernel(xbc_ref, prev_ref, small_ref, z_ref, g_ref, convw_ref, convb_ref, dtb_ref, alog_ref,
                dskip_ref, nw_ref, wb_ref, tri_ref, exp_ref, o_ref, st_ref, ext_ref):
    c = pl.program_id(1)
    L = SSM_CHUNK

    @pl.when(c == 0)
    def _():
        st_ref[...] = jnp.zeros_like(st_ref)

    ext_ref[0:8, :] = jnp.where(c > 0, prev_ref[...], 0.0)
    ext_ref[8:8 + L, :] = xbc_ref[...]
    acc = jnp.broadcast_to(convb_ref[...], (L, SSM_CONV_DIM))
    for k in range(SSM_CONV):
        r0 = 8 - (SSM_CONV - 1) + k
        acc = acc + convw_ref[k:k + 1, :] * ext_ref[r0:r0 + L, :]
    xbc = _silu(acc)
    xs = xbc[:, :SSM_D_INNER]
    bm = xbc[:, SSM_D_INNER:SSM_D_INNER + SSM_GROUPS * SSM_STATE]
    cm = xbc[:, SSM_D_INNER + SSM_GROUPS * SSM_STATE:]

    lane = lax.broadcasted_iota(jnp.int32, (L, 128), 1)
    head_lane = lane < SSM_HEADS
    u = small_ref[...] + dtb_ref[...]
    dt = jnp.maximum(u, 0.0) + jnp.log1p(jnp.exp(-jnp.abs(u)))
    dt = jnp.where(head_lane, dt, 0.0)
    a = dt * -jnp.exp(alog_ref[...])
    cs = _dot3(_split3(a), tri_ref[...], lhs_side=False)
    cs_last = cs[L - 1:L, :]
    ecs = jnp.exp(cs)
    ds = jnp.exp(cs_last - cs)
    packed = (dt + pltpu.roll(jnp.where(head_lane, ecs, 0.0), SSM_HEADS, 1)
              + pltpu.roll(jnp.where(head_lane, ds, 0.0), 2 * SSM_HEADS, 1))
    expanded = _dot3(_split3(packed), exp_ref[...])
    dt_e = expanded[:, :SSM_D_INNER]
    ecs_e = expanded[:, SSM_D_INNER:2 * SSM_D_INNER]
    ds_e = expanded[:, 2 * SSM_D_INNER:]
    xdt = xs * dt_e
    xds = xdt * ds_e
    cs_t = cs.T

    row = lax.broadcasted_iota(jnp.int32, (L, L), 0)
    col = lax.broadcasted_iota(jnp.int32, (L, L), 1)
    tril = row >= col
    half = lax.broadcasted_iota(jnp.int32, (L, 128), 1) < SSM_HEAD_DIM
    gw = SSM_D_INNER // SSM_GROUPS

    y_groups = []
    for g in range(SSM_GROUPS):
        bg = bm[:, g * SSM_STATE:(g + 1) * SSM_STATE]
        cg = cm[:, g * SSM_STATE:(g + 1) * SSM_STATE].astype(BF16)
        bg_t = bg.T.astype(BF16)
        cb = jnp.dot(cg, bg_t, preferred_element_type=F32)
        st = st_ref[g]
        y_off = jnp.dot(cg, st.astype(BF16), preferred_element_type=F32) * ecs_e[:, g * gw:(g + 1) * gw]
        st_ref[g] = (ecs_e[L - 1:L, g * gw:(g + 1) * gw] * st
                     + jnp.dot(bg_t, xds[:, g * gw:(g + 1) * gw].astype(BF16), preferred_element_type=F32))
        pairs = []
        for pr in range(2):
            rhs = xdt[:, g * gw + pr * 128:g * gw + (pr + 1) * 128].astype(BF16)
            ys = []
            for hh in range(2):
                h = g * 4 + pr * 2 + hh
                seg = cs[:, h:h + 1] - cs_t[h:h + 1, :]
                dec = jnp.exp(jnp.where(tril, seg, NEG))
                ys.append(jnp.dot((cb * dec).astype(BF16), rhs, preferred_element_type=F32))
            pairs.append(jnp.where(half, ys[0], ys[1]))
        y_groups.append(jnp.concatenate(pairs, axis=1) + y_off)
    y = jnp.concatenate(y_groups, axis=1) + dskip_ref[...] * xs

    v = y * _silu(z_ref[...])
    ms = jnp.mean(v * v, axis=-1, keepdims=True)
    vn = (v * lax.rsqrt(ms + EPS)) * nw_ref[...]
    ys = jnp.dot(vn.astype(BF16), wb_ref[...], preferred_element_type=F32)
    o_ref[...] = jax.nn.sigmoid(g_ref[...]) * ys


def _ssd(slab, bsz, s, conv_w, conv_b, dtb, alog, dskip_e, nw, wb):
    nt = s // SSM_CHUNK
    L = SSM_CHUNK
    tri = jnp.asarray(np.tril(np.ones((L, L), np.float32)), BF16)
    e = np.zeros((128, 3 * SSM_D_INNER), np.float32)
    for k in range(3):
        for h in range(SSM_HEADS):
            e[k * SSM_HEADS + h, k * SSM_D_INNER + h * SSM_HEAD_DIM:k * SSM_D_INNER + (h + 1) * SSM_HEAD_DIM] = 1.0
    e = jnp.asarray(e, BF16)
    rows = lambda b, c: b * nt + c
    const = lambda shape: pl.BlockSpec(shape, lambda b, c: (0,) * len(shape))
    return pl.pallas_call(
        _ssd_kernel,
        out_shape=jax.ShapeDtypeStruct((bsz * s, SSM_D_INNER), F32),
        grid=(bsz, nt),
        in_specs=[
            pl.BlockSpec((L, SSM_CONV_DIM), lambda b, c: (rows(b, c), SLAB_XBC // SSM_CONV_DIM)),
            pl.BlockSpec((8, SSM_CONV_DIM), lambda b, c: (jnp.maximum(rows(b, c) * (L // 8) - 1, 0), 0)),
            pl.BlockSpec((L, 128), lambda b, c: (rows(b, c), SLAB_SMALL // 128)),
            pl.BlockSpec((L, SSM_D_INNER), lambda b, c: (rows(b, c), SLAB_ZSSM // SSM_D_INNER)),
            pl.BlockSpec((L, D_MODEL), lambda b, c: (rows(b, c), SLAB_GSSM // D_MODEL)),
            const((SSM_CONV, SSM_CONV_DIM)), const((1, SSM_CONV_DIM)), const((1, 128)), const((1, 128)),
            const((1, SSM_D_INNER)), const((1, SSM_D_INNER)), const((SSM_D_INNER, D_MODEL)),
            const((L, L)), const((128, 3 * SSM_D_INNER)),
        ],
        out_specs=pl.BlockSpec((L, D_MODEL), lambda b, c: (rows(b, c), 0)),
        scratch_shapes=[pltpu.VMEM((SSM_GROUPS, SSM_STATE, SSM_D_INNER // SSM_GROUPS), F32),
                        pltpu.VMEM((L + 8, SSM_CONV_DIM), F32)],
        compiler_params=_cparams(2),
        name="ssd",
    )(slab, slab, slab, slab, slab, conv_w, conv_b, dtb, alog, dskip_e, nw, wb, tri, e)


def _prep_kernel(q_ref, kc_ref, ks_ref, vs_ref, kw_ref, vw_ref, small_ref, cos_ref, sin_ref,
                 qt_ref, kso_ref, vst_ref, kwo_ref, vwt_ref, kcr_ref, gt_ref):
    cos = cos_ref[...]
    sin = sin_ref[...]

    def rope(x):
        w = x.shape[1]
        cf = jnp.concatenate([cos] * (w // 128), axis=1)
        sf = jnp.concatenate([sin] * (w // 128), axis=1)
        lane = lax.broadcasted_iota(jnp.int32, x.shape, 1)
        first = (lane % HEAD_DIM) < (HEAD_DIM // 2)
        swapped = jnp.where(first, pltpu.roll(x, w - HEAD_DIM // 2, 1), pltpu.roll(x, HEAD_DIM // 2, 1))
        return x * cf + swapped * sf

    halves = TOK_TILE // Q_TILE
    lane128 = lax.broadcasted_iota(jnp.int32, (TOK_TILE, 128), 1)
    row128 = lax.broadcasted_iota(jnp.int32, (TOK_TILE, 128), 0)
    low64 = lane128 < HEAD_DIM
    block_onehot = jnp.where(lane128 - HEAD_DIM == lax.shift_right_logical(row128, int(math.log2(SLC_BLOCK))), 1.0, 0.0)

    def group_cols(x, g, rest):
        r = x if g == 0 else pltpu.roll(x, KV_WIDTH - HEAD_DIM * g, 1)
        return jnp.where(low64, r[:, :128], rest)

    def values_t(v_t, g, lo, hi):
        n = hi - lo
        extra = jnp.where(lax.broadcasted_iota(jnp.int32, (V_ROWS - HEAD_DIM, n), 0) == 0, 1.0, 0.0)
        return jnp.concatenate([v_t[g * HEAD_DIM:(g + 1) * HEAD_DIM, lo:hi], extra], axis=0).astype(BF16)

    qr = rope(q_ref[...]) * (HEAD_DIM ** -0.5 * LOG2E)
    for hf in range(halves):
        q_t = qr[hf * Q_TILE:(hf + 1) * Q_TILE, :].T
        for g in range(ATTN_GROUPS):
            base = g * HEADS_PER_GROUP * HEAD_DIM
            blk = jnp.concatenate([q_t[base + h * HEAD_DIM:base + (h + 1) * HEAD_DIM, :]
                                   for h in range(HEADS_PER_GROUP)], axis=1)
            qt_ref[0, g, hf] = blk.astype(BF16)

    ksr = rope(ks_ref[...])
    vs_t = vs_ref[...].T
    for g in range(ATTN_GROUPS):
        kso_ref[0, g, 0] = group_cols(ksr, g, block_onehot).astype(BF16)
        vst_ref[0, g, 0] = values_t(vs_t, g, 0, TOK_TILE)

    kwr = rope(kw_ref[...])
    vw_t = vw_ref[...].T
    for g in range(ATTN_GROUPS):
        kg = group_cols(kwr, g, 0.0).astype(BF16)
        for hf in range(halves):
            kwo_ref[0, g, hf] = kg[hf * Q_TILE:(hf + 1) * Q_TILE, :]
            vwt_ref[0, g, hf] = values_t(vw_t, g, hf * Q_TILE, (hf + 1) * Q_TILE)

    kcr_ref[...] = rope(kc_ref[...])

    gates_t = jax.nn.sigmoid(small_ref[...]).T
    zeros_g = jnp.zeros((8 - 3, QCOLS), F32)
    for hf in range(halves):
        for g in range(ATTN_GROUPS):
            rows_c = []
            for cc in range(3):
                rows_c.append(jnp.concatenate(
                    [gates_t[GATE_LANE0 + (g * HEADS_PER_GROUP + r) * 3 + cc:GATE_LANE0 + (g * HEADS_PER_GROUP + r) * 3 + cc + 1,
                             hf * Q_TILE:(hf + 1) * Q_TILE] for r in range(HEADS_PER_GROUP)], axis=1))
            gt_ref[0, g, hf] = jnp.concatenate(rows_c + [zeros_g], axis=0)


def _prep(slab, bsz, s, cos_t, sin_t):
    nt = s // TOK_TILE
    nq = s // Q_TILE
    halves = TOK_TILE // Q_TILE
    G = ATTN_GROUPS
    rows = lambda b, i: b * nt + i
    kvspec = lambda k: pl.BlockSpec((TOK_TILE, KV_WIDTH), lambda b, i: (rows(b, i), SLAB_KV // KV_WIDTH + k))
    out5 = lambda blk: pl.BlockSpec(blk, lambda b, i: (b, 0, i, 0, 0))
    return pl.pallas_call(
        _prep_kernel,
        out_shape=[
            jax.ShapeDtypeStruct((bsz, G, nq, HEAD_DIM, QCOLS), BF16),
            jax.ShapeDtypeStruct((bsz, G, nt, KEY_TILE, 128), BF16),
            jax.ShapeDtypeStruct((bsz, G, nt, V_ROWS, KEY_TILE), BF16),
            jax.ShapeDtypeStruct((bsz, G, nq, Q_TILE, 128), BF16),
            jax.ShapeDtypeStruct((bsz, G, nq, V_ROWS, Q_TILE), BF16),
            jax.ShapeDtypeStruct((bsz * s, KV_WIDTH), F32),
            jax.ShapeDtypeStruct((bsz, G, nq, 8, QCOLS), F32),
        ],
        grid=(bsz, nt),
        in_specs=[
            pl.BlockSpec((TOK_TILE, ATTN_WIDTH), lambda b, i: (rows(b, i), SLAB_Q // ATTN_WIDTH)),
            kvspec(0), kvspec(2), kvspec(3), kvspec(4), kvspec(5),
            pl.BlockSpec((TOK_TILE, 128), lambda b, i: (rows(b, i), SLAB_SMALL // 128)),
            pl.BlockSpec((TOK_TILE, 128), lambda b, i: (i, 0)),
            pl.BlockSpec((TOK_TILE, 128), lambda b, i: (i, 0)),
        ],
        out_specs=[
            out5((1, G, halves, HEAD_DIM, QCOLS)),
            out5((1, G, 1, KEY_TILE, 128)),
            out5((1, G, 1, V_ROWS, KEY_TILE)),
            out5((1, G, halves, Q_TILE, 128)),
            out5((1, G, halves, V_ROWS, Q_TILE)),
            pl.BlockSpec((TOK_TILE, KV_WIDTH), lambda b, i: (rows(b, i), 0)),
            out5((1, G, halves, 8, QCOLS)),
        ],
        compiler_params=_cparams(2),
        name="prep",
    )(slab, slab, slab, slab, slab, slab, slab, cos_t, sin_t)


def _compress_kernel(h_ref, pos_ref, w1_ref, b1_ref, w2_ref, nat_ref, tr_ref):
    h = h_ref[0, 0, 0]
    nh = h.shape[0]
    w1 = w1_ref[0]
    a = jnp.dot((h + pos_ref[0, 0]).astype(BF16), w1[:, :CMP_HIDDEN], preferred_element_type=F32)
    b = jnp.dot((h + pos_ref[0, 1]).astype(BF16), w1[:, CMP_HIDDEN:], preferred_element_type=F32)
    hid = a + pltpu.roll(b, nh - 1, 0) + b1_ref[0]
    out = jnp.dot(jax.nn.gelu(hid).astype(BF16), w2_ref[0], preferred_element_type=F32)
    nat_ref[0, 0, 0] = out
    tr_ref[0, 0, 0] = out.T


def _compress(hkv, pos, w1, b1, w2):
    _, bsz, G, nh, _ = hkv.shape
    return pl.pallas_call(
        _compress_kernel,
        out_shape=[jax.ShapeDtypeStruct((2, bsz, G, nh, 128), F32),
                   jax.ShapeDtypeStruct((2, bsz, G, 128, nh), F32)],
        grid=(2, bsz, G),
        in_specs=[
            pl.BlockSpec((1, 1, 1, nh, CMP_STRIDE * HEAD_DIM), lambda k, b, g: (k, b, g, 0, 0)),
            pl.BlockSpec((1, 2, 1, CMP_STRIDE * HEAD_DIM), lambda k, b, g: (k, 0, 0, 0)),
            pl.BlockSpec((1, CMP_STRIDE * HEAD_DIM, 2 * CMP_HIDDEN), lambda k, b, g: (k, 0, 0)),
            pl.BlockSpec((1, 1, CMP_HIDDEN), lambda k, b, g: (k, 0, 0)),
            pl.BlockSpec((1, CMP_HIDDEN, 128), lambda k, b, g: (k, 0, 0)),
        ],
        out_specs=[pl.BlockSpec((1, 1, 1, nh, 128), lambda k, b, g: (k, b, g, 0, 0)),
                   pl.BlockSpec((1, 1, 1, 128, nh), lambda k, b, g: (k, b, g, 0, 0))],
        compiler_params=_cparams(3),
        name="compress",
    )(hkv, pos, w1, b1, w2)


def _online_update(carry, s, v_t):
    m, acc = carry
    m_new = jnp.maximum(m, jnp.max(s, axis=0, keepdims=True))
    p = jnp.exp2(s - m_new)
    acc_new = jnp.exp2(m - m_new) * acc + jnp.dot(v_t, p.astype(BF16), preferred_element_type=F32)
    return m_new, acc_new


def _attn_kernel(qt_ref, kc_ref, vct_ref, ks_ref, vst_ref, kw_ref, vwt_ref, g_ref, o_ref, p4_ref, bias_ref,
                 sbuf_ref, *, nh, ns, nkt, top_n):
    qi = pl.program_id(2)
    s0 = qi * Q_TILE
    q64 = qt_ref[0, 0, 0]
    q_t = jnp.concatenate([q64, jnp.zeros((128 - HEAD_DIM, QCOLS), BF16)], axis=0)
    t_row = s0 + (lax.broadcasted_iota(jnp.int32, (1, QCOLS), 1) & (Q_TILE - 1))

    sc = jnp.dot(kc_ref[0, 0, 0].astype(BF16), q_t, preferred_element_type=F32)
    cend = lax.broadcasted_iota(jnp.int32, (nh, QCOLS), 0) * CMP_STRIDE + (CMP_BLOCK - 1)
    cmask = cend <= t_row
    sm = jnp.where(cmask, sc, NEG)
    mc = jnp.max(sm, axis=0, keepdims=True)
    pc = jnp.where(cmask, jnp.exp2(sm - mc), 0.0)
    lc = jnp.sum(pc, axis=0, keepdims=True)
    pn = pc * jnp.where(lc > 0.0, 1.0 / lc, 0.0)
    o_c = jnp.dot(vct_ref[0, 0, 0][:HEAD_DIM, :].astype(BF16), pn.astype(BF16), preferred_element_type=F32)

    p4 = pn[:, 0:Q_TILE]
    for h in range(1, HEADS_PER_GROUP):
        p4 = p4 + pn[:, h * Q_TILE:(h + 1) * Q_TILE]
    p4_ref[0:8, :] = jnp.zeros((8, Q_TILE), F32)
    p4_ref[8:8 + nh, :] = p4
    imp = p4_ref[pl.ds(7, ns, stride=4), :]
    for cc in range(4):
        imp = imp + p4_ref[pl.ds(8 + cc, ns, stride=4), :]

    blk = lax.broadcasted_iota(jnp.int32, (ns, Q_TILE), 0)
    tok = s0 + lax.broadcasted_iota(jnp.int32, (ns, Q_TILE), 1)
    cur = lax.shift_right_logical(tok, int(math.log2(SLC_BLOCK)))
    future = blk * SLC_BLOCK > tok
    forced = (blk == 0) | (blk == cur) | (blk == cur - 1)
    val = jnp.where(forced, -2.0, jnp.where(future, -1.0, imp))

    sel = jnp.where(forced, 1.0, 0.0)
    for _ in range(top_n - 3):
        best = jnp.max(val, axis=0, keepdims=True)
        first = jnp.min(jnp.where(val == best, blk, ns), axis=0, keepdims=True)
        hit = blk == first
        val = jnp.where(hit, -2.0, val)
        sel = jnp.where(hit, 1.0, sel)
    bias = jnp.where((sel > 0.0) & jnp.logical_not(future), 0.0, NEG)
    blocks_per_tile = KEY_TILE // SLC_BLOCK
    pad_rows = jnp.zeros((BF16_ROWS - blocks_per_tile, QCOLS), F32)
    for kt in range(nkt):
        rows = bias[kt * blocks_per_tile:(kt + 1) * blocks_per_tile, :]
        rows = jnp.concatenate([rows] * HEADS_PER_GROUP, axis=1)
        bias_ref[kt] = jnp.concatenate([rows, pad_rows], axis=0).astype(BF16)
    masked_rows = jnp.where(lax.broadcasted_iota(jnp.int32, (BF16_ROWS, QCOLS), 0) < blocks_per_tile,
                            NEG, 0.0).astype(BF16)
    zero_rows = jnp.zeros((128 - HEAD_DIM - BF16_ROWS, QCOLS), BF16)

    kk = lax.broadcasted_iota(jnp.int32, (Q_TILE, QCOLS), 0)
    tl = lax.broadcasted_iota(jnp.int32, (Q_TILE, QCOLS), 1) & (Q_TILE - 1)
    n_back = WINDOW // Q_TILE
    sw, vw = [], []
    for d in range(n_back + 1):
        kt = jnp.maximum(qi - d, 0)
        valid = jnp.full((Q_TILE, QCOLS), qi - d, jnp.int32) >= 0
        s = jnp.dot(kw_ref[0, 0, kt], q_t, preferred_element_type=F32)
        if d == 0:
            mask = kk <= tl
        elif d == n_back:
            mask = (kk > tl) & valid
        else:
            mask = valid
        sw.append(jnp.where(mask, s, NEG))
        vw.append(vwt_ref[0, 0, kt])
    sw = jnp.concatenate(sw, axis=0)
    pw = jnp.exp2(sw - jnp.max(sw, axis=0, keepdims=True))
    acc_w = jnp.dot(jnp.concatenate(vw, axis=1), pw.astype(BF16), preferred_element_type=F32)
    o_w = acc_w[:HEAD_DIM, :] * (1.0 / acc_w[HEAD_DIM:HEAD_DIM + 1, :])

    n_full = lax.shift_right_logical(qi, 1)
    assert KEY_TILE == 2 * Q_TILE

    def sel_scores(kt, rows):
        q_aug = jnp.concatenate([q64, rows, zero_rows], axis=0)
        return jnp.dot(ks_ref[0, 0, kt], q_aug, preferred_element_type=F32)

    def loop_scores(kt):
        ktc = jnp.minimum(kt, nkt - 1)
        rows = jnp.where(kt < n_full, bias_ref[ktc], masked_rows)
        return sel_scores(ktc, rows)

    def sel_body(i, carry):
        s_cur = sbuf_ref[...]
        for j in range(SEL_UNROLL):
            kt = i * SEL_UNROLL + j
            s_next = loop_scores(kt + 1)
            carry = _online_update(carry, s_cur, vst_ref[0, 0, jnp.minimum(kt, nkt - 1)])
            s_cur = s_next
        sbuf_ref[...] = s_cur
        return carry

    init = (jnp.full((1, QCOLS), NEG, F32), jnp.zeros((V_ROWS, QCOLS), F32))
    n_iter = lax.shift_right_logical(n_full + (SEL_UNROLL - 1), int(math.log2(SEL_UNROLL)))
    sbuf_ref[...] = loop_scores(0)
    carry = lax.fori_loop(0, n_iter, sel_body, init)
    sd = sel_scores(n_full, bias_ref[n_full])
    key = n_full * KEY_TILE + lax.broadcasted_iota(jnp.int32, (KEY_TILE, QCOLS), 0)
    _, acc_s = _online_update(carry, jnp.where(key <= t_row, sd, NEG), vst_ref[0, 0, n_full])
    o_s = acc_s[:HEAD_DIM, :] * (1.0 / acc_s[HEAD_DIM:HEAD_DIM + 1, :])

    gts = g_ref[0, 0, 0]
    o_ref[0, 0, 0] = gts[0:1, :] * o_c + gts[1:2, :] * o_s + gts[2:3, :] * o_w


def _attn(q_t, kc_nat, vc_tr, ks, vst, kw, vwt, gts, s):
    bsz, G, nq = q_t.shape[:3]
    nh = s // CMP_STRIDE
    ns = s // SLC_BLOCK
    nkt = s // KEY_TILE
    top_n = min(SLC_TOP_N, ns)
    per_q = lambda blk: pl.BlockSpec(blk, lambda b, g, i: (b, g, i, 0, 0))
    per_bg = lambda blk: pl.BlockSpec(blk, lambda b, g, i: (b, g, 0, 0, 0))
    return pl.pallas_call(
        functools.partial(_attn_kernel, nh=nh, ns=ns, nkt=nkt, top_n=top_n),
        out_shape=jax.ShapeDtypeStruct((bsz, G, nq, HEAD_DIM, QCOLS), F32),
        grid=(bsz, G, nq),
        in_specs=[
            per_q((1, 1, 1, HEAD_DIM, QCOLS)),
            pl.BlockSpec((1, 1, 1, nh, 128), lambda b, g, i: (0, b, g, 0, 0)),
            pl.BlockSpec((1, 1, 1, 128, nh), lambda b, g, i: (1, b, g, 0, 0)),
            per_bg((1, 1, nkt, KEY_TILE, 128)),
            per_bg((1, 1, nkt, V_ROWS, KEY_TILE)),
            per_bg((1, 1, nq, Q_TILE, 128)),
            per_bg((1, 1, nq, V_ROWS, Q_TILE)),
            per_q((1, 1, 1, 8, QCOLS)),
        ],
        out_specs=per_q((1, 1, 1, HEAD_DIM, QCOLS)),
        scratch_shapes=[pltpu.VMEM((nh + 8, Q_TILE), F32),
                        pltpu.VMEM((nkt, BF16_ROWS, QCOLS), BF16),
                        pltpu.VMEM((KEY_TILE, QCOLS), F32)],
        compiler_params=_cparams(3),
        name="attn",
    )(q_t, kc_nat, vc_tr, ks, vst, kw, vwt, gts)


def _final_kernel(x_ref, gs_ref, ot_ref, z_ref, g_ref, wba_ref, wout_ref, fnw_ref, o_ref, *, last):
    halves = TOK_TILE // Q_TILE
    parts = []
    for hf in range(halves):
        rows = []
        for g in range(ATTN_GROUPS):
            blk = ot_ref[0, g, hf]
            for h in range(HEADS_PER_GROUP):
                rows.append(blk[:, h * Q_TILE:(h + 1) * Q_TILE])
        parts.append(jnp.concatenate(rows, axis=0).T)
    o = jnp.concatenate(parts, axis=0)
    y_attn = jnp.dot((o * _silu(z_ref[...])).astype(BF16), wba_ref[...], preferred_element_type=F32)
    merged = gs_ref[...] + jax.nn.sigmoid(g_ref[...]) * y_attn
    xo = x_ref[...] + jnp.dot(merged.astype(BF16), wout_ref[...], preferred_element_type=F32)
    if last:
        ms = jnp.mean(xo * xo, axis=-1, keepdims=True)
        xo = (xo * lax.rsqrt(ms + EPS)) * fnw_ref[...]
    o_ref[...] = xo


def _final(x2, gs, o_t, slab, wba, wout, fnw, bsz, s, last):
    nt = s // TOK_TILE
    halves = TOK_TILE // Q_TILE
    rows = lambda b, i: b * nt + i
    const = lambda shape: pl.BlockSpec(shape, lambda b, i: (0,) * len(shape))
    return pl.pallas_call(
        functools.partial(_final_kernel, last=last),
        out_shape=jax.ShapeDtypeStruct((bsz * s, D_MODEL), F32),
        grid=(bsz, nt),
        in_specs=[
            pl.BlockSpec((TOK_TILE, D_MODEL), lambda b, i: (rows(b, i), 0)),
            pl.BlockSpec((TOK_TILE, D_MODEL), lambda b, i: (rows(b, i), 0)),
            pl.BlockSpec((1, ATTN_GROUPS, halves, HEAD_DIM, QCOLS), lambda b, i: (b, 0, i, 0, 0)),
            pl.BlockSpec((TOK_TILE, ATTN_WIDTH), lambda b, i: (rows(b, i), SLAB_ZATT // ATTN_WIDTH)),
            pl.BlockSpec((TOK_TILE, D_MODEL), lambda b, i: (rows(b, i), SLAB_GATT // D_MODEL)),
            const((ATTN_WIDTH, D_MODEL)), const((D_MODEL, D_MODEL)), const((1, D_MODEL)),
        ],
        out_specs=pl.BlockSpec((TOK_TILE, D_MODEL), lambda b, i: (rows(b, i), 0)),
        compiler_params=_cparams(2),
        name="final",
    )(x2, gs, o_t, slab, slab, wba, wout, fnw)


def _slab_weights(w):
    o = np.cumsum([0, SSM_D_INNER, SSM_CONV_DIM, SSM_HEADS, ATTN_WIDTH, 6 * KV_WIDTH, 3 * ATTN_HEADS,
                   ATTN_WIDTH, D_MODEL, D_MODEL])
    z_ssm, xbc, dt, q, kv, gate, z_att, g_ssm, g_att = [w[:, o[i]:o[i + 1]] for i in range(9)]
    pad = jnp.zeros((w.shape[0], SLAB_W - int(o[-1])), w.dtype)
    return jnp.concatenate([xbc, z_ssm, q, z_att, g_ssm, g_att, kv, dt, gate, pad], axis=1).astype(BF16)


def _rope_tables(s):
    half = HEAD_DIM // 2
    inv_freq = 1.0 / (ROPE_THETA ** (jnp.arange(half, dtype=F32) * 2.0 / HEAD_DIM))
    ang = jnp.arange(s, dtype=F32)[:, None] * inv_freq[None, :]
    cos, sin = jnp.cos(ang), jnp.sin(ang)
    cos_t = jnp.concatenate([cos, cos] * (128 // HEAD_DIM), axis=1)
    sin_t = jnp.concatenate([-sin, sin] * (128 // HEAD_DIM), axis=1)
    return cos_t, sin_t


def _pad_lanes(v, width=128):
    v = v.reshape(1, -1)
    return jnp.concatenate([v, jnp.zeros((1, width - v.shape[1]), v.dtype)], axis=1)


def kernel(x, norm_w, w_in, conv_w, conv_b, dt_bias, a_log, d_skip, ssm_norm_w, cmp_pos_k, cmp_pos_v, cmp_k_w1,
           cmp_k_b1, cmp_k_w2, cmp_v_w1, cmp_v_b1, cmp_v_w2, w_branch_ssm, w_branch_attn, w_out, final_norm_w):
    bsz, s, _ = x.shape
    depth = norm_w.shape[0]
    G = ATTN_GROUPS
    nh = s // CMP_STRIDE
    cos_t, sin_t = _rope_tables(s)
    x2 = x.reshape(bsz * s, D_MODEL)
    for layer in range(depth):
        slab = _proj(x2, norm_w[layer].reshape(1, -1), _slab_weights(w_in[layer]))

        gs = _ssd(slab, bsz, s, conv_w[layer], conv_b[layer].reshape(1, -1), _pad_lanes(dt_bias[layer]),
                  _pad_lanes(a_log[layer]), jnp.repeat(d_skip[layer], SSM_HEAD_DIM).reshape(1, -1),
                  ssm_norm_w[layer].reshape(1, -1), w_branch_ssm[layer].astype(BF16))

        q_t, ks, vst, kw, vwt, kcr, gts = _prep(slab, bsz, s, cos_t, sin_t)

        vcmp = slab[:, SLAB_KV + KV_WIDTH:SLAB_KV + 2 * KV_WIDTH]
        halfblocks = lambda u: u.reshape(bsz, nh, CMP_STRIDE, G, HEAD_DIM).transpose(0, 3, 1, 2, 4).reshape(
            bsz, G, nh, CMP_STRIDE * HEAD_DIM)
        hkv = jnp.stack([halfblocks(kcr), halfblocks(vcmp)])
        pos = jnp.stack([cmp_pos_k[layer], cmp_pos_v[layer]]).reshape(2, 2, 1, CMP_STRIDE * HEAD_DIM)
        w1 = jnp.stack([cmp_k_w1[layer], cmp_v_w1[layer]])
        w1 = jnp.concatenate([w1[:, :CMP_STRIDE * HEAD_DIM], w1[:, CMP_STRIDE * HEAD_DIM:]], axis=2).astype(BF16)
        b1 = jnp.stack([cmp_k_b1[layer], cmp_v_b1[layer]]).reshape(2, 1, CMP_HIDDEN)
        w2 = jnp.stack([cmp_k_w2[layer], cmp_v_w2[layer]])
        w2 = jnp.concatenate([w2, jnp.zeros((2, CMP_HIDDEN, 128 - HEAD_DIM), w2.dtype)], axis=2).astype(BF16)
        c_nat, c_tr = _compress(hkv, pos, w1, b1, w2)

        o_t = _attn(q_t, c_nat, c_tr, ks, vst, kw, vwt, gts, s)

        x2 = _final(x2, gs, o_t, slab, w_branch_attn[layer].astype(BF16), w_out[layer].astype(BF16),
                    final_norm_w.reshape(1, -1), bsz, s, layer == depth - 1)
    return x2.reshape(bsz, s, D_MODEL)
```

```python
import functools
import math

import numpy as np
import jax
import jax.numpy as jnp
from jax import lax
from jax.experimental import pallas as pl
from jax.experimental.pallas import tpu as pltpu

F32 = jnp.float32
BF16 = jnp.bfloat16

EPS = 1e-6
ROPE_THETA = 10000.0
NEG = -1e30

D_MODEL = 1024
SSM_HEADS = 16
SSM_HEAD_DIM = 64
SSM_D_INNER = 1024
SSM_GROUPS = 4
SSM_STATE = 128
SSM_CONV = 4
SSM_CHUNK = 256
SSM_CONV_DIM = 2048
ATTN_HEADS = 16
ATTN_GROUPS = 4
HEAD_DIM = 64
ATTN_WIDTH = 1024
KV_WIDTH = 256
CMP_BLOCK = 32
CMP_STRIDE = 16
CMP_HIDDEN = 256
SLC_BLOCK = 64
SLC_TOP_N = 16
WINDOW = 512
Q_TILE = 128
KEY_TILE = 256
TOK_TILE = 256
HEADS_PER_GROUP = ATTN_HEADS // ATTN_GROUPS
QCOLS = HEADS_PER_GROUP * Q_TILE
SEL_UNROLL = 4
CMP_CHUNK = 128
WIN_POS = 8
BF16_ROWS = 16
V_ROWS = HEAD_DIM + BF16_ROWS
LOG2E = math.log2(math.e)

SLAB_XBC = 0
SLAB_ZSSM = 2048
SLAB_Q = 3072
SLAB_ZATT = 4096
SLAB_GSSM = 5120
SLAB_GATT = 6144
SLAB_KV = 7168
SLAB_SMALL = 8704
SLAB_W = 8832
GATE_LANE0 = SSM_HEADS

VMEM_LIMIT = 56 * 1024 * 1024


def _cparams(n_grid):
    return pltpu.CompilerParams(dimension_semantics=("arbitrary",) * n_grid,
                                vmem_limit_bytes=VMEM_LIMIT)


def _silu(x):
    return x * jax.nn.sigmoid(x)


def _proj_kernel(x_ref, nw_ref, w_ref, o_ref):
    x = x_ref[...]
    ms = jnp.mean(x * x, axis=-1, keepdims=True)
    h = (x * lax.rsqrt(ms + EPS)) * nw_ref[...]
    o_ref[...] = jnp.dot(h.astype(BF16), w_ref[...], preferred_element_type=F32)


def _proj(x2, norm_w, w_slab):
    t = x2.shape[0]
    tm, tn = 512, SLAB_W // 3
    return pl.pallas_call(
        _proj_kernel,
        out_shape=jax.ShapeDtypeStruct((t, SLAB_W), F32),
        grid=(SLAB_W // tn, t // tm),
        in_specs=[pl.BlockSpec((tm, D_MODEL), lambda j, i: (i, 0)),
                  pl.BlockSpec((1, D_MODEL), lambda j, i: (0, 0)),
                  pl.BlockSpec((D_MODEL, tn), lambda j, i: (0, j))],
        out_specs=pl.BlockSpec((tm, tn), lambda j, i: (i, j)),
        compiler_params=_cparams(2),
        name="proj",
    )(x2, norm_w, w_slab)


def _split3(v):
    v1 = v.astype(BF16)
    r1 = v - v1.astype(F32)
    v2 = r1.astype(BF16)
    v3 = (r1 - v2.astype(F32)).astype(BF16)
    return v1, v2, v3


def _dot3(parts, rhs, lhs_side=True):
    out = None
    for p in parts:
        d = (jnp.dot(p, rhs, preferred_element_type=F32) if lhs_side
             else jnp.dot(rhs, p, preferred_element_type=F32))
        out = d if out is None else out + d
    return out


def _ssd_kernel(xbc_ref, prev_ref, small_ref, z_ref, g_ref, convw_ref, convb_ref, dtb_ref, alog_ref,
                dskip_ref, nw_ref, wb_ref, tri_ref, exp_ref, o_ref, st_ref, ext_ref):
    c = pl.program_id(1)
    L = SSM_CHUNK

    @pl.when(c == 0)
    def _():
        st_ref[...] = jnp.zeros_like(st_ref)

    ext_ref[0:8, :] = jnp.where(c > 0, prev_ref[...], 0.0)
    ext_ref[8:8 + L, :] = xbc_ref[...]
    acc = jnp.broadcast_to(convb_ref[...], (L, SSM_CONV_DIM))
    for k in range(SSM_CONV):
        r0 = 8 - (SSM_CONV - 1) + k
        acc = acc + convw_ref[k:k + 1, :] * ext_ref[r0:r0 + L, :]
    xbc = _silu(acc)
    xs = xbc[:, :SSM_D_INNER]
    bm = xbc[:, SSM_D_INNER:SSM_D_INNER + SSM_GROUPS * SSM_STATE]
    cm = xbc[:, SSM_D_INNER + SSM_GROUPS * SSM_STATE:]

    lane = lax.broadcasted_iota(jnp.int32, (L, 128), 1)
    head_lane = lane < SSM_HEADS
    u = small_ref[...] + dtb_ref[...]
    dt = jnp.maximum(u, 0.0) + jnp.log1p(jnp.exp(-jnp.abs(u)))
    dt = jnp.where(head_lane, dt, 0.0)
    a = dt * -jnp.exp(alog_ref[...])
    cs = _dot3(_split3(a), tri_ref[...], lhs_side=False)
    cs_last = cs[L - 1:L, :]
    ecs = jnp.exp(cs)
    ds = jnp.exp(cs_last - cs)
    packed = (dt + pltpu.roll(jnp.where(head_lane, ecs, 0.0), SSM_HEADS, 1)
              + pltpu.roll(jnp.where(head_lane, ds, 0.0), 2 * SSM_HEADS, 1))
    expanded = _dot3(_split3(packed), exp_ref[...])
    dt_e = expanded[:, :SSM_D_INNER]
    ecs_e = expanded[:, SSM_D_INNER:2 * SSM_D_INNER]
    ds_e = expanded[:, 2 * SSM_D_INNER:]
    xdt = xs * dt_e
    xds = xdt * ds_e
    cs_t = cs.T

    row = lax.broadcasted_iota(jnp.int32, (L, L), 0)
    col = lax.broadcasted_iota(jnp.int32, (L, L), 1)
    tril = row >= col
    half = lax.broadcasted_iota(jnp.int32, (L, 128), 1) < SSM_HEAD_DIM
    gw = SSM_D_INNER // SSM_GROUPS

    y_groups = []
    for g in range(SSM_GROUPS):
        bg = bm[:, g * SSM_STATE:(g + 1) * SSM_STATE]
        cg = cm[:, g * SSM_STATE:(g + 1) * SSM_STATE].astype(BF16)
        bg_t = bg.T.astype(BF16)
        cb = jnp.dot(cg, bg_t, preferred_element_type=F32)
        st = st_ref[g]
        y_off = jnp.dot(cg, st.astype(BF16), preferred_element_type=F32) * ecs_e[:, g * gw:(g + 1) * gw]
        st_ref[g] = (ecs_e[L - 1:L, g * gw:(g + 1) * gw] * st
                     + jnp.dot(bg_t, xds[:, g * gw:(g + 1) * gw].astype(BF16), preferred_element_type=F32))
        pairs = []
        for pr in range(2):
            rhs = xdt[:, g * gw + pr * 128:g * gw + (pr + 1) * 128].astype(BF16)
            ys = []
            for hh in range(2):
                h = g * 4 + pr * 2 + hh
                seg = cs[:, h:h + 1] - cs_t[h:h + 1, :]
                dec = jnp.exp(jnp.where(tril, seg, NEG))
                ys.append(jnp.dot((cb * dec).astype(BF16), rhs, preferred_element_type=F32))
            pairs.append(jnp.where(half, ys[0], ys[1]))
        y_groups.append(jnp.concatenate(pairs, axis=1) + y_off)
    y = jnp.concatenate(y_groups, axis=1) + dskip_ref[...] * xs

    v = y * _silu(z_ref[...])
    ms = jnp.mean(v * v, axis=-1, keepdims=True)
    vn = (v * lax.rsqrt(ms + EPS)) * nw_ref[...]
    ys = jnp.dot(vn.astype(BF16), wb_ref[...], preferred_element_type=F32)
    o_ref[...] = jax.nn.sigmoid(g_ref[...]) * ys


def _ssd(slab, bsz, s, conv_w, conv_b, dtb, alog, dskip_e, nw, wb):
    nt = s // SSM_CHUNK
    L = SSM_CHUNK
    tri = jnp.asarray(np.tril(np.ones((L, L), np.float32)), BF16)
    e = np.zeros((128, 3 * SSM_D_INNER), np.float32)
    for k in range(3):
        for h in range(SSM_HEADS):
            e[k * SSM_HEADS + h, k * SSM_D_INNER + h * SSM_HEAD_DIM:k * SSM_D_INNER + (h + 1) * SSM_HEAD_DIM] = 1.0
    e = jnp.asarray(e, BF16)
    rows = lambda b, c: b * nt + c
    const = lambda shape: pl.BlockSpec(shape, lambda b, c: (0,) * len(shape))
    return pl.pallas_call(
        _ssd_kernel,
        out_shape=jax.ShapeDtypeStruct((bsz * s, SSM_D_INNER), F32),
        grid=(bsz, nt),
        in_specs=[
            pl.BlockSpec((L, SSM_CONV_DIM), lambda b, c: (rows(b, c), SLAB_XBC // SSM_CONV_DIM)),
            pl.BlockSpec((8, SSM_CONV_DIM), lambda b, c: (jnp.maximum(rows(b, c) * (L // 8) - 1, 0), 0)),
            pl.BlockSpec((L, 128), lambda b, c: (rows(b, c), SLAB_SMALL // 128)),
            pl.BlockSpec((L, SSM_D_INNER), lambda b, c: (rows(b, c), SLAB_ZSSM // SSM_D_INNER)),
            pl.BlockSpec((L, D_MODEL), lambda b, c: (rows(b, c), SLAB_GSSM // D_MODEL)),
            const((SSM_CONV, SSM_CONV_DIM)), const((1, SSM_CONV_DIM)), const((1, 128)), const((1, 128)),
            const((1, SSM_D_INNER)), const((1, SSM_D_INNER)), const((SSM_D_INNER, D_MODEL)),
            const((L, L)), const((128, 3 * SSM_D_INNER)),
        ],
        out_specs=pl.BlockSpec((L, D_MODEL), lambda b, c: (rows(b, c), 0)),
        scratch_shapes=[pltpu.VMEM((SSM_GROUPS, SSM_STATE, SSM_D_INNER // SSM_GROUPS), F32),
                        pltpu.VMEM((L + 8, SSM_CONV_DIM), F32)],
        compiler_params=_cparams(2),
        name="ssd",
    )(slab, slab, slab, slab, slab, conv_w, conv_b, dtb, alog, dskip_e, nw, wb, tri, e)


def _prep_kernel(q_ref, kc_ref, ks_ref, vs_ref, kw_ref, vw_ref, small_ref, cos_ref, sin_ref,
                 qt_ref, kso_ref, vst_ref, kwo_ref, vwt_ref, kcr_ref, gt_ref):
    cos = cos_ref[...]
    sin = sin_ref[...]

    def rope(x):
        w = x.shape[1]
        cf = jnp.concatenate([cos] * (w // 128), axis=1)
        sf = jnp.concatenate([sin] * (w // 128), axis=1)
        lane = lax.broadcasted_iota(jnp.int32, x.shape, 1)
        first = (lane % HEAD_DIM) < (HEAD_DIM // 2)
        swapped = jnp.where(first, pltpu.roll(x, w - HEAD_DIM // 2, 1), pltpu.roll(x, HEAD_DIM // 2, 1))
        return x * cf + swapped * sf

    halves = TOK_TILE // Q_TILE
    lane128 = lax.broadcasted_iota(jnp.int32, (TOK_TILE, 128), 1)
    row128 = lax.broadcasted_iota(jnp.int32, (TOK_TILE, 128), 0)
    low64 = lane128 < HEAD_DIM
    blocks_per_tile = KEY_TILE // SLC_BLOCK
    tile_in_super = pl.program_id(1) & (SEL_UNROLL - 1)
    block_in_super = tile_in_super * blocks_per_tile + lax.shift_right_logical(row128, int(math.log2(SLC_BLOCK)))
    block_onehot = jnp.where(lane128 - HEAD_DIM == block_in_super, 1.0, 0.0)
    win_pos = ((pl.program_id(1) * halves + lax.shift_right_logical(row128, int(math.log2(Q_TILE))))
               & (WIN_POS - 1))
    win_onehot = jnp.where(lane128 - HEAD_DIM == win_pos, 1.0, 0.0)
    assert TOK_TILE == KEY_TILE and SEL_UNROLL * blocks_per_tile == BF16_ROWS

    def group_cols(x, g, rest):
        r = x if g == 0 else pltpu.roll(x, KV_WIDTH - HEAD_DIM * g, 1)
        return jnp.where(low64, r[:, :128], rest)

    def values_t(v_t, g, lo, hi):
        n = hi - lo
        extra = jnp.where(lax.broadcasted_iota(jnp.int32, (V_ROWS - HEAD_DIM, n), 0) == 0, 1.0, 0.0)
        return jnp.concatenate([v_t[g * HEAD_DIM:(g + 1) * HEAD_DIM, lo:hi], extra], axis=0).astype(BF16)

    qr = rope(q_ref[...]) * (HEAD_DIM ** -0.5 * LOG2E)
    for hf in range(halves):
        q_t = qr[hf * Q_TILE:(hf + 1) * Q_TILE, :].T
        for g in range(ATTN_GROUPS):
            base = g * HEADS_PER_GROUP * HEAD_DIM
            blk = jnp.concatenate([q_t[base + h * HEAD_DIM:base + (h + 1) * HEAD_DIM, :]
                                   for h in range(HEADS_PER_GROUP)], axis=1)
            qt_ref[0, g, hf] = blk.astype(BF16)

    ksr = rope(ks_ref[...])
    vs_t = vs_ref[...].T
    for g in range(ATTN_GROUPS):
        kso_ref[0, g, 0] = group_cols(ksr, g, block_onehot).astype(BF16)
        vst_ref[0, g, 0] = values_t(vs_t, g, 0, TOK_TILE)

    kwr = rope(kw_ref[...])
    vw_t = vw_ref[...].T
    for g in range(ATTN_GROUPS):
        kg = group_cols(kwr, g, win_onehot).astype(BF16)
        for hf in range(halves):
            kwo_ref[0, g, hf] = kg[hf * Q_TILE:(hf + 1) * Q_TILE, :]
            vwt_ref[0, g, hf] = values_t(vw_t, g, hf * Q_TILE, (hf + 1) * Q_TILE)

    kcr_ref[...] = rope(kc_ref[...])

    gates_t = jax.nn.sigmoid(small_ref[...]).T
    zeros_g = jnp.zeros((8 - 3, QCOLS), F32)
    for hf in range(halves):
        for g in range(ATTN_GROUPS):
            rows_c = []
            for cc in range(3):
                rows_c.append(jnp.concatenate(
                    [gates_t[GATE_LANE0 + (g * HEADS_PER_GROUP + r) * 3 + cc:GATE_LANE0 + (g * HEADS_PER_GROUP + r) * 3 + cc + 1,
                             hf * Q_TILE:(hf + 1) * Q_TILE] for r in range(HEADS_PER_GROUP)], axis=1))
            gt_ref[0, g, hf] = jnp.concatenate(rows_c + [zeros_g], axis=0)


def _prep(slab, bsz, s, cos_t, sin_t):
    nt = s // TOK_TILE
    nq = s // Q_TILE
    halves = TOK_TILE // Q_TILE
    G = ATTN_GROUPS
    rows = lambda b, i: b * nt + i
    kvspec = lambda k: pl.BlockSpec((TOK_TILE, KV_WIDTH), lambda b, i: (rows(b, i), SLAB_KV // KV_WIDTH + k))
    out5 = lambda blk: pl.BlockSpec(blk, lambda b, i: (b, 0, i, 0, 0))
    return pl.pallas_call(
        _prep_kernel,
        out_shape=[
            jax.ShapeDtypeStruct((bsz, G, nq, HEAD_DIM, QCOLS), BF16),
            jax.ShapeDtypeStruct((bsz, G, nt, KEY_TILE, 128), BF16),
            jax.ShapeDtypeStruct((bsz, G, nt, V_ROWS, KEY_TILE), BF16),
            jax.ShapeDtypeStruct((bsz, G, nq, Q_TILE, 128), BF16),
            jax.ShapeDtypeStruct((bsz, G, nq, V_ROWS, Q_TILE), BF16),
            jax.ShapeDtypeStruct((bsz * s, KV_WIDTH), F32),
            jax.ShapeDtypeStruct((bsz, G, nq, 8, QCOLS), F32),
        ],
        grid=(bsz, nt),
        in_specs=[
            pl.BlockSpec((TOK_TILE, ATTN_WIDTH), lambda b, i: (rows(b, i), SLAB_Q // ATTN_WIDTH)),
            kvspec(0), kvspec(2), kvspec(3), kvspec(4), kvspec(5),
            pl.BlockSpec((TOK_TILE, 128), lambda b, i: (rows(b, i), SLAB_SMALL // 128)),
            pl.BlockSpec((TOK_TILE, 128), lambda b, i: (i, 0)),
            pl.BlockSpec((TOK_TILE, 128), lambda b, i: (i, 0)),
        ],
        out_specs=[
            out5((1, G, halves, HEAD_DIM, QCOLS)),
            out5((1, G, 1, KEY_TILE, 128)),
            out5((1, G, 1, V_ROWS, KEY_TILE)),
            out5((1, G, halves, Q_TILE, 128)),
            out5((1, G, halves, V_ROWS, Q_TILE)),
            pl.BlockSpec((TOK_TILE, KV_WIDTH), lambda b, i: (rows(b, i), 0)),
            out5((1, G, halves, 8, QCOLS)),
        ],
        compiler_params=_cparams(2),
        name="prep",
    )(slab, slab, slab, slab, slab, slab, slab, cos_t, sin_t)


def _compress_kernel(h_ref, pos_ref, w1_ref, b1_ref, w2_ref, nat_ref, tr_ref):
    h = h_ref[0, 0, 0]
    nh = h.shape[0]
    w1 = w1_ref[0]
    a = jnp.dot((h + pos_ref[0, 0]).astype(BF16), w1[:, :CMP_HIDDEN], preferred_element_type=F32)
    b = jnp.dot((h + pos_ref[0, 1]).astype(BF16), w1[:, CMP_HIDDEN:], preferred_element_type=F32)
    hid = a + pltpu.roll(b, nh - 1, 0) + b1_ref[0]
    out = jnp.dot(jax.nn.gelu(hid).astype(BF16), w2_ref[0], preferred_element_type=F32)
    nat_ref[0, 0, 0] = out.astype(BF16)
    tr_ref[0, 0, 0] = out.T[:HEAD_DIM, :].astype(BF16)


def _compress(hkv, pos, w1, b1, w2):
    _, bsz, G, nh, _ = hkv.shape
    return pl.pallas_call(
        _compress_kernel,
        out_shape=[jax.ShapeDtypeStruct((2, bsz, G, nh, 128), BF16),
                   jax.ShapeDtypeStruct((2, bsz, G, HEAD_DIM, nh), BF16)],
        grid=(2, bsz, G),
        in_specs=[
            pl.BlockSpec((1, 1, 1, nh, CMP_STRIDE * HEAD_DIM), lambda k, b, g: (k, b, g, 0, 0)),
            pl.BlockSpec((1, 2, 1, CMP_STRIDE * HEAD_DIM), lambda k, b, g: (k, 0, 0, 0)),
            pl.BlockSpec((1, CMP_STRIDE * HEAD_DIM, 2 * CMP_HIDDEN), lambda k, b, g: (k, 0, 0)),
            pl.BlockSpec((1, 1, CMP_HIDDEN), lambda k, b, g: (k, 0, 0)),
            pl.BlockSpec((1, CMP_HIDDEN, 128), lambda k, b, g: (k, 0, 0)),
        ],
        out_specs=[pl.BlockSpec((1, 1, 1, nh, 128), lambda k, b, g: (k, b, g, 0, 0)),
                   pl.BlockSpec((1, 1, 1, HEAD_DIM, nh), lambda k, b, g: (k, b, g, 0, 0))],
        compiler_params=_cparams(3),
        name="compress",
    )(hkv, pos, w1, b1, w2)


def _online_update(carry, s, v_t):
    m, acc = carry
    m_new = jnp.maximum(m, jnp.max(s, axis=0, keepdims=True))
    p = jnp.exp2(s - m_new)
    acc_new = jnp.exp2(m - m_new) * acc + jnp.dot(v_t, p.astype(BF16), preferred_element_type=F32)
    return m_new, acc_new


def _attn_kernel(qt_ref, kc_ref, vct_ref, ks_ref, vst_ref, kw_ref, vwt_ref, g_ref, tri_ref, o_ref, p4_ref, oc_ref,
                 bias_ref, biasm_ref, sbuf_a_ref, sbuf_b_ref, sd_ref, *, nh, ns, nst, nq, top_n):
    qi = pl.program_id(2)
    s0 = qi * Q_TILE
    q64 = qt_ref[0, 0, 0]
    q_t = jnp.concatenate([q64, jnp.zeros((128 - HEAD_DIM, QCOLS), BF16)], axis=0)
    t_row = s0 + (lax.broadcasted_iota(jnp.int32, (1, QCOLS), 1) & (Q_TILE - 1))

    def compressed(rows):
        sc = jnp.dot(kc_ref[0, 0, 0, 0:rows, :], q_t, preferred_element_type=F32)
        cend = lax.broadcasted_iota(jnp.int32, (rows, QCOLS), 0) * CMP_STRIDE + (CMP_BLOCK - 1)
        sm = jnp.where(cend <= t_row, sc, NEG)
        mc = jnp.max(sm, axis=0, keepdims=True)
        pc = jnp.exp2(sm - mc)
        lc = jnp.sum(pc, axis=0, keepdims=True)
        pn = pc * jnp.where(t_row >= CMP_BLOCK - 1, 1.0 / lc, 0.0)
        oc_ref[...] = jnp.dot(vct_ref[0, 0, 0, :, 0:rows], pn.astype(BF16), preferred_element_type=F32)
        p4 = pn[:, 0:Q_TILE]
        for h in range(1, HEADS_PER_GROUP):
            p4 = p4 + pn[:, h * Q_TILE:(h + 1) * Q_TILE]
        p4_ref[8:8 + rows, :] = p4
        if rows < nh:
            p4_ref[8 + rows:8 + nh, :] = jnp.zeros((nh - rows, Q_TILE), F32)

    p4_ref[0:8, :] = jnp.zeros((8, Q_TILE), F32)
    n_chunks = nh // CMP_CHUNK
    need = lax.shift_right_logical(qi * (Q_TILE // CMP_STRIDE) + (Q_TILE - CMP_BLOCK) // CMP_STRIDE,
                                   int(math.log2(CMP_CHUNK)))
    for c in range(n_chunks):
        pl.when(need == c)(functools.partial(compressed, (c + 1) * CMP_CHUNK))
    o_c = oc_ref[...]

    imp = p4_ref[pl.ds(7, ns, stride=4), :]
    for cc in range(4):
        imp = imp + p4_ref[pl.ds(8 + cc, ns, stride=4), :]

    blk = lax.broadcasted_iota(jnp.int32, (ns, Q_TILE), 0)
    tok = s0 + lax.broadcasted_iota(jnp.int32, (ns, Q_TILE), 1)
    cur = lax.shift_right_logical(tok, int(math.log2(SLC_BLOCK)))
    future = blk * SLC_BLOCK > tok
    forced = (blk == 0) | (blk == cur) | (blk == cur - 1)
    val = jnp.where(forced, -2.0, jnp.where(future, -1.0, imp))

    zero_rows = jnp.zeros((128 - HEAD_DIM - BF16_ROWS, QCOLS), BF16)
    eye = jnp.where(lax.broadcasted_iota(jnp.int32, (Q_TILE, Q_TILE), 0)
                    == lax.broadcasted_iota(jnp.int32, (Q_TILE, Q_TILE), 1), 1.0, 0.0).astype(BF16)
    causal_tri = tri_ref[0]
    window_tri = tri_ref[1]

    def edge_scores(k_tile, rows16, tri):
        q_aug = jnp.concatenate([q64, rows16, zero_rows, tri], axis=0)
        return jnp.dot(jnp.concatenate([k_tile, eye], axis=1), q_aug, preferred_element_type=F32)

    n_back = WINDOW // Q_TILE
    row16 = lax.broadcasted_iota(jnp.int32, (BF16_ROWS, QCOLS), 0)
    back = (qi - row16) & (WIN_POS - 1)
    rows_w = jnp.where((row16 < WIN_POS) & (back <= n_back) & (qi - back < 0), NEG, 0.0).astype(BF16)
    win_tile = lambda d: jnp.where(qi - d < 0, qi - d + nq, qi - d)
    sw = jnp.concatenate(
        [edge_scores(kw_ref[0, 0, win_tile(0)], rows_w, causal_tri),
         jnp.dot(jnp.concatenate([kw_ref[0, 0, win_tile(d)] for d in range(1, n_back)], axis=0),
                 jnp.concatenate([q64, rows_w, zero_rows], axis=0), preferred_element_type=F32),
         edge_scores(kw_ref[0, 0, win_tile(n_back)], rows_w, window_tri)], axis=0)
    vw = jnp.concatenate([vwt_ref[0, 0, win_tile(d)] for d in range(n_back + 1)], axis=1)

    for _ in range(top_n - 3):
        best = jnp.max(val, axis=0, keepdims=True)
        first = jnp.min(jnp.where(val == best, blk, ns), axis=0, keepdims=True)
        val = jnp.where(blk == first, -2.0, val)
    n_full = lax.shift_right_logical(qi, 1)
    assert KEY_TILE == 2 * Q_TILE
    bias = jnp.where((val == -2.0) & jnp.logical_not(future), 0.0, NEG)
    bias_main = jnp.where(blk < n_full * (KEY_TILE // SLC_BLOCK), bias, NEG)
    for st in range(nst):
        for table, ref in ((bias, bias_ref), (bias_main, biasm_ref)):
            rows = table[st * BF16_ROWS:(st + 1) * BF16_ROWS, :]
            ref[st] = jnp.concatenate([rows] * HEADS_PER_GROUP, axis=1).astype(BF16)

    def produce(st, dst_ref):
        stc = jnp.minimum(st, nst - 1)
        k = ks_ref[0, 0, pl.ds(stc * SEL_UNROLL, SEL_UNROLL)].reshape(SEL_UNROLL * KEY_TILE, 128)
        q_aug = jnp.concatenate([q64, biasm_ref[stc], zero_rows], axis=0)
        s = jnp.dot(k, q_aug, preferred_element_type=F32)
        dst_ref[...] = s
        return jnp.max(s, axis=0, keepdims=True)

    def step(i, carry, cur_ref, next_ref):
        m, acc, mx = carry
        mx_next = produce(i + 1, next_ref)
        m_new = jnp.maximum(m, mx)
        p = jnp.exp2(cur_ref[...] - m_new)
        v4 = vst_ref[0, 0, pl.ds(i * SEL_UNROLL, SEL_UNROLL)]
        v_t = jnp.concatenate([v4[j] for j in range(SEL_UNROLL)], axis=1)
        acc = jnp.exp2(m - m_new) * acc + jnp.dot(v_t, p.astype(BF16), preferred_element_type=F32)
        return m_new, acc, mx_next

    def sel_body(i, carry):
        return lax.cond((i & 1) == 0,
                        lambda c: step(i, c, sbuf_a_ref, sbuf_b_ref),
                        lambda c: step(i, c, sbuf_b_ref, sbuf_a_ref), carry)

    n_iter = lax.shift_right_logical(n_full + (SEL_UNROLL - 1), int(math.log2(SEL_UNROLL)))
    init = (jnp.full((1, QCOLS), NEG, F32), jnp.zeros((V_ROWS, QCOLS), F32), produce(0, sbuf_a_ref))

    odd = (qi & 1) == 1
    rows_d = bias_ref[lax.shift_right_logical(n_full, int(math.log2(SEL_UNROLL)))]
    all_neg = jnp.full((Q_TILE, QCOLS), NEG, BF16)
    tri_a = jnp.where(odd, jnp.zeros((Q_TILE, QCOLS), BF16), causal_tri)
    tri_b = jnp.where(odd, causal_tri, all_neg)
    sd_ref[0:Q_TILE, :] = edge_scores(ks_ref[0, 0, n_full, 0:Q_TILE, :], rows_d, tri_a)
    sd_ref[Q_TILE:KEY_TILE, :] = edge_scores(ks_ref[0, 0, n_full, Q_TILE:KEY_TILE, :], rows_d, tri_b)

    pw = jnp.exp2(sw - jnp.max(sw, axis=0, keepdims=True))
    acc_w = jnp.dot(vw, pw.astype(BF16), preferred_element_type=F32)
    o_w = acc_w[:HEAD_DIM, :] * (1.0 / acc_w[HEAD_DIM:HEAD_DIM + 1, :])

    m_s, acc_s, _ = lax.fori_loop(0, n_iter, sel_body, init)
    _, acc_s = _online_update((m_s, acc_s), sd_ref[...], vst_ref[0, 0, n_full])
    o_s = acc_s[:HEAD_DIM, :] * (1.0 / acc_s[HEAD_DIM:HEAD_DIM + 1, :])

    gts = g_ref[0, 0, 0]
    o_ref[0, 0, 0] = gts[0:1, :] * o_c + gts[1:2, :] * o_s + gts[2:3, :] * o_w


def _attn(q_t, kc_nat, vc_tr, ks, vst, kw, vwt, gts, s):
    bsz, G, nq = q_t.shape[:3]
    nh = s // CMP_STRIDE
    ns = s // SLC_BLOCK
    nkt = s // KEY_TILE
    nst = nkt // SEL_UNROLL
    assert s % (SEL_UNROLL * KEY_TILE) == 0 and nh % CMP_CHUNK == 0
    top_n = min(SLC_TOP_N, ns)
    per_q = lambda blk: pl.BlockSpec(blk, lambda b, g, i: (b, g, i, 0, 0))
    per_bg = lambda blk: pl.BlockSpec(blk, lambda b, g, i: (b, g, 0, 0, 0))
    kk = np.arange(Q_TILE)[:, None]
    tl = np.tile(np.arange(Q_TILE), HEADS_PER_GROUP)[None, :]
    tri = jnp.asarray(np.stack([np.where(kk <= tl, 0.0, NEG), np.where(kk > tl, 0.0, NEG)]), BF16)
    return pl.pallas_call(
        functools.partial(_attn_kernel, nh=nh, ns=ns, nst=nst, nq=nq, top_n=top_n),
        out_shape=jax.ShapeDtypeStruct((bsz, G, nq, HEAD_DIM, QCOLS), F32),
        grid=(bsz, G, nq),
        in_specs=[
            per_q((1, 1, 1, HEAD_DIM, QCOLS)),
            pl.BlockSpec((1, 1, 1, nh, 128), lambda b, g, i: (0, b, g, 0, 0)),
            pl.BlockSpec((1, 1, 1, HEAD_DIM, nh), lambda b, g, i: (1, b, g, 0, 0)),
            per_bg((1, 1, nkt, KEY_TILE, 128)),
            per_bg((1, 1, nkt, V_ROWS, KEY_TILE)),
            per_bg((1, 1, nq, Q_TILE, 128)),
            per_bg((1, 1, nq, V_ROWS, Q_TILE)),
            per_q((1, 1, 1, 8, QCOLS)),
            pl.BlockSpec((2, Q_TILE, QCOLS), lambda b, g, i: (0, 0, 0)),
        ],
        out_specs=per_q((1, 1, 1, HEAD_DIM, QCOLS)),
        scratch_shapes=[pltpu.VMEM((nh + 8, Q_TILE), F32),
                        pltpu.VMEM((HEAD_DIM, QCOLS), F32),
                        pltpu.VMEM((nst, BF16_ROWS, QCOLS), BF16),
                        pltpu.VMEM((nst, BF16_ROWS, QCOLS), BF16),
                        pltpu.VMEM((SEL_UNROLL * KEY_TILE, QCOLS), F32),
                        pltpu.VMEM((SEL_UNROLL * KEY_TILE, QCOLS), F32),
                        pltpu.VMEM((KEY_TILE, QCOLS), F32)],
        compiler_params=_cparams(3),
        name="attn",
    )(q_t, kc_nat, vc_tr, ks, vst, kw, vwt, gts, tri)


def _final_kernel(x_ref, gs_ref, ot_ref, z_ref, g_ref, wba_ref, wout_ref, fnw_ref, o_ref, *, last):
    halves = TOK_TILE // Q_TILE
    parts = []
    for hf in range(halves):
        rows = []
        for g in range(ATTN_GROUPS):
            blk = ot_ref[0, g, hf]
            for h in range(HEADS_PER_GROUP):
                rows.append(blk[:, h * Q_TILE:(h + 1) * Q_TILE])
        parts.append(jnp.concatenate(rows, axis=0).T)
    o = jnp.concatenate(parts, axis=0)
    y_attn = jnp.dot((o * _silu(z_ref[...])).astype(BF16), wba_ref[...], preferred_element_type=F32)
    merged = gs_ref[...] + jax.nn.sigmoid(g_ref[...]) * y_attn
    xo = x_ref[...] + jnp.dot(merged.astype(BF16), wout_ref[...], preferred_element_type=F32)
    if last:
        ms = jnp.mean(xo * xo, axis=-1, keepdims=True)
        xo = (xo * lax.rsqrt(ms + EPS)) * fnw_ref[...]
    o_ref[...] = xo


def _final(x2, gs, o_t, slab, wba, wout, fnw, bsz, s, last):
    nt = s // TOK_TILE
    halves = TOK_TILE // Q_TILE
    rows = lambda b, i: b * nt + i
    const = lambda shape: pl.BlockSpec(shape, lambda b, i: (0,) * len(shape))
    return pl.pallas_call(
        functools.partial(_final_kernel, last=last),
        out_shape=jax.ShapeDtypeStruct((bsz * s, D_MODEL), F32),
        grid=(bsz, nt),
        in_specs=[
            pl.BlockSpec((TOK_TILE, D_MODEL), lambda b, i: (rows(b, i), 0)),
            pl.BlockSpec((TOK_TILE, D_MODEL), lambda b, i: (rows(b, i), 0)),
            pl.BlockSpec((1, ATTN_GROUPS, halves, HEAD_DIM, QCOLS), lambda b, i: (b, 0, i, 0, 0)),
            pl.BlockSpec((TOK_TILE, ATTN_WIDTH), lambda b, i: (rows(b, i), SLAB_ZATT // ATTN_WIDTH)),
            pl.BlockSpec((TOK_TILE, D_MODEL), lambda b, i: (rows(b, i), SLAB_GATT // D_MODEL)),
            const((ATTN_WIDTH, D_MODEL)), const((D_MODEL, D_MODEL)), const((1, D_MODEL)),
        ],
        out_specs=pl.BlockSpec((TOK_TILE, D_MODEL), lambda b, i: (rows(b, i), 0)),
        compiler_params=_cparams(2),
        name="final",
    )(x2, gs, o_t, slab, slab, wba, wout, fnw)


def _slab_weights(w):
    o = np.cumsum([0, SSM_D_INNER, SSM_CONV_DIM, SSM_HEADS, ATTN_WIDTH, 6 * KV_WIDTH, 3 * ATTN_HEADS,
                   ATTN_WIDTH, D_MODEL, D_MODEL])
    z_ssm, xbc, dt, q, kv, gate, z_att, g_ssm, g_att = [w[:, o[i]:o[i + 1]] for i in range(9)]
    pad = jnp.zeros((w.shape[0], SLAB_W - int(o[-1])), w.dtype)
    return jnp.concatenate([xbc, z_ssm, q, z_att, g_ssm, g_att, kv, dt, gate, pad], axis=1).astype(BF16)


def _rope_tables(s):
    half = HEAD_DIM // 2
    inv_freq = 1.0 / (ROPE_THETA ** (jnp.arange(half, dtype=F32) * 2.0 / HEAD_DIM))
    ang = jnp.arange(s, dtype=F32)[:, None] * inv_freq[None, :]
    cos, sin = jnp.cos(ang), jnp.sin(ang)
    cos_t = jnp.concatenate([cos, cos] * (128 // HEAD_DIM), axis=1)
    sin_t = jnp.concatenate([-sin, sin] * (128 // HEAD_DIM), axis=1)
    return cos_t, sin_t


def _pad_lanes(v, width=128):
    v = v.reshape(1, -1)
    return jnp.concatenate([v, jnp.zeros((1, width - v.shape[1]), v.dtype)], axis=1)


def kernel(x, norm_w, w_in, conv_w, conv_b, dt_bias, a_log, d_skip, ssm_norm_w, cmp_pos_k, cmp_pos_v, cmp_k_w1,
           cmp_k_b1, cmp_k_w2, cmp_v_w1, cmp_v_b1, cmp_v_w2, w_branch_ssm, w_branch_attn, w_out, final_norm_w):
    bsz, s, _ = x.shape
    depth = norm_w.shape[0]
    G = ATTN_GROUPS
    nh = s // CMP_STRIDE
    cos_t, sin_t = _rope_tables(s)
    x2 = x.reshape(bsz * s, D_MODEL)
    for layer in range(depth):
        slab = _proj(x2, norm_w[layer].reshape(1, -1), _slab_weights(w_in[layer]))

        gs = _ssd(slab, bsz, s, conv_w[layer], conv_b[layer].reshape(1, -1), _pad_lanes(dt_bias[layer]),
                  _pad_lanes(a_log[layer]), jnp.repeat(d_skip[layer], SSM_HEAD_DIM).reshape(1, -1),
                  ssm_norm_w[layer].reshape(1, -1), w_branch_ssm[layer].astype(BF16))

        q_t, ks, vst, kw, vwt, kcr, gts = _prep(slab, bsz, s, cos_t, sin_t)

        vcmp = slab[:, SLAB_KV + KV_WIDTH:SLAB_KV + 2 * KV_WIDTH]
        halfblocks = lambda u: u.reshape(bsz, nh, CMP_STRIDE, G, HEAD_DIM).transpose(0, 3, 1, 2, 4).reshape(
            bsz, G, nh, CMP_STRIDE * HEAD_DIM)
        hkv = jnp.stack([halfblocks(kcr), halfblocks(vcmp)])
        pos = jnp.stack([cmp_pos_k[layer], cmp_pos_v[layer]]).reshape(2, 2, 1, CMP_STRIDE * HEAD_DIM)
        w1 = jnp.stack([cmp_k_w1[layer], cmp_v_w1[layer]])
        w1 = jnp.concatenate([w1[:, :CMP_STRIDE * HEAD_DIM], w1[:, CMP_STRIDE * HEAD_DIM:]], axis=2).astype(BF16)
        b1 = jnp.stack([cmp_k_b1[layer], cmp_v_b1[layer]]).reshape(2, 1, CMP_HIDDEN)
        w2 = jnp.stack([cmp_k_w2[layer], cmp_v_w2[layer]])
        w2 = jnp.concatenate([w2, jnp.zeros((2, CMP_HIDDEN, 128 - HEAD_DIM), w2.dtype)], axis=2).astype(BF16)
        c_nat, c_tr = _compress(hkv, pos, w1, b1, w2)

        o_t = _attn(q_t, c_nat, c_tr, ks, vst, kw, vwt, gts, s)

        x2 = _final(x2, gs, o_t, slab, w_branch_attn[layer].astype(BF16), w_out[layer].astype(BF16),
                    final_norm_w.reshape(1, -1), bsz, s, layer == depth - 1)
    return x2.reshape(bsz, s, D_MODEL)
```

```python
import functools
import math

import numpy as np
import jax
import jax.numpy as jnp
from jax import lax
from jax.experimental import pallas as pl
from jax.experimental.pallas import tpu as pltpu

F32 = jnp.float32
BF16 = jnp.bfloat16

EPS = 1e-6
ROPE_THETA = 10000.0
NEG = -1e30

D_MODEL = 1024
SSM_HEADS = 16
SSM_HEAD_DIM = 64
SSM_D_INNER = 1024
SSM_GROUPS = 4
SSM_STATE = 128
SSM_CONV = 4
SSM_CHUNK = 256
SSM_CONV_DIM = 2048
ATTN_HEADS = 16
ATTN_GROUPS = 4
HEAD_DIM = 64
ATTN_WIDTH = 1024
KV_WIDTH = 256
CMP_BLOCK = 32
CMP_STRIDE = 16
CMP_HIDDEN = 256
SLC_BLOCK = 64
SLC_TOP_N = 16
WINDOW = 512
Q_TILE = 128
KEY_TILE = 256
TOK_TILE = 256
HEADS_PER_GROUP = ATTN_HEADS // ATTN_GROUPS
QCOLS = HEADS_PER_GROUP * Q_TILE
SEL_UNROLL = 4
CMP_CHUNK = 128
WIN_POS = 8
BF16_ROWS = 16
V_ROWS = HEAD_DIM + BF16_ROWS
LOG2E = math.log2(math.e)

SLAB_XBC = 0
SLAB_ZSSM = 2048
SLAB_Q = 3072
SLAB_ZATT = 4096
SLAB_GSSM = 5120
SLAB_GATT = 6144
SLAB_KV = 7168
SLAB_SMALL = 8704
SLAB_W = 8832
GATE_LANE0 = SSM_HEADS

VMEM_LIMIT = 56 * 1024 * 1024


def _cparams(n_grid):
    return pltpu.CompilerParams(dimension_semantics=("arbitrary",) * n_grid,
                                vmem_limit_bytes=VMEM_LIMIT)


def _silu(x):
    h = 0.5 * x
    return h + h * jnp.tanh(h)


def _proj_kernel(x_ref, nw_ref, w_ref, o_ref):
    x = x_ref[...]
    ms = jnp.mean(x * x, axis=-1, keepdims=True)
    h = (x * lax.rsqrt(ms + EPS)) * nw_ref[...]
    o_ref[...] = jnp.dot(h.astype(BF16), w_ref[...], preferred_element_type=F32)


def _proj(x2, norm_w, w_slab):
    t = x2.shape[0]
    tm, tn = 512, SLAB_W // 3
    return pl.pallas_call(
        _proj_kernel,
        out_shape=jax.ShapeDtypeStruct((t, SLAB_W), F32),
        grid=(SLAB_W // tn, t // tm),
        in_specs=[pl.BlockSpec((tm, D_MODEL), lambda j, i: (i, 0)),
                  pl.BlockSpec((1, D_MODEL), lambda j, i: (0, 0)),
                  pl.BlockSpec((D_MODEL, tn), lambda j, i: (0, j))],
        out_specs=pl.BlockSpec((tm, tn), lambda j, i: (i, j)),
        compiler_params=_cparams(2),
        name="proj",
    )(x2, norm_w, w_slab)


def _split3(v):
    v1 = v.astype(BF16)
    r1 = v - v1.astype(F32)
    v2 = r1.astype(BF16)
    v3 = (r1 - v2.astype(F32)).astype(BF16)
    return v1, v2, v3


def _dot3(parts, rhs, lhs_side=True):
    out = None
    for p in parts:
        d = (jnp.dot(p, rhs, preferred_element_type=F32) if lhs_side
             else jnp.dot(rhs, p, preferred_element_type=F32))
        out = d if out is None else out + d
    return out


def _ssd_kernel(xbc_ref, prev_ref, small_ref, z_ref, g_ref, convw_ref, convb_ref, dtb_ref, alog_ref,
                dskip_ref, nw_ref, wb_ref, tri_ref, exp_ref, o_ref, st_ref, ext_ref):
    c = pl.program_id(1)
    L = SSM_CHUNK

    @pl.when(c == 0)
    def _():
        st_ref[...] = jnp.zeros_like(st_ref)

    prev = jnp.where(c > 0, prev_ref[...], 0.0)
    cols = []
    for j in range(SSM_CONV_DIM // 128):
        lanes = slice(j * 128, (j + 1) * 128)
        ext_ref[j, 0:8, :] = prev[:, lanes]
        ext_ref[j, 8:8 + L, :] = xbc_ref[:, lanes]
        acc = jnp.broadcast_to(convb_ref[:, lanes], (L, 128))
        for k in range(SSM_CONV):
            r0 = 8 - (SSM_CONV - 1) + k
            acc = acc + convw_ref[k:k + 1, lanes] * ext_ref[j, r0:r0 + L, :]
        cols.append(_silu(acc))
    xbc = jnp.concatenate(cols, axis=1)
    xs = xbc[:, :SSM_D_INNER]
    bm = xbc[:, SSM_D_INNER:SSM_D_INNER + SSM_GROUPS * SSM_STATE]
    cm = xbc[:, SSM_D_INNER + SSM_GROUPS * SSM_STATE:]

    lane = lax.broadcasted_iota(jnp.int32, (L, 128), 1)
    head_lane = lane < SSM_HEADS
    u = small_ref[...] + dtb_ref[...]
    dt = jnp.maximum(u, 0.0) + jnp.log1p(jnp.exp(-jnp.abs(u)))
    dt = jnp.where(head_lane, dt, 0.0)
    a = dt * -jnp.exp(alog_ref[...])
    cs = _dot3(_split3(a), tri_ref[...], lhs_side=False)
    cs_last = cs[L - 1:L, :]
    ecs = jnp.exp(cs)
    ds = jnp.exp(cs_last - cs)
    packed = (dt + pltpu.roll(jnp.where(head_lane, ecs, 0.0), SSM_HEADS, 1)
              + pltpu.roll(jnp.where(head_lane, ds, 0.0), 2 * SSM_HEADS, 1))
    expanded = _dot3(_split3(packed)[:2], exp_ref[...])
    dt_e = expanded[:, :SSM_D_INNER]
    ecs_e = expanded[:, SSM_D_INNER:2 * SSM_D_INNER]
    ds_e = expanded[:, 2 * SSM_D_INNER:]
    xdt = xs * dt_e
    xds = xdt * ds_e
    cs_t = cs.T

    row = lax.broadcasted_iota(jnp.int32, (L, L), 0)
    col = lax.broadcasted_iota(jnp.int32, (L, L), 1)
    tril = row >= col
    half = lax.broadcasted_iota(jnp.int32, (L, 128), 1) < SSM_HEAD_DIM
    gw = SSM_D_INNER // SSM_GROUPS

    y_groups = []
    for g in range(SSM_GROUPS):
        bg = bm[:, g * SSM_STATE:(g + 1) * SSM_STATE]
        cg = cm[:, g * SSM_STATE:(g + 1) * SSM_STATE].astype(BF16)
        bg_t = bg.T.astype(BF16)
        cb = jnp.dot(cg, bg_t, preferred_element_type=F32)
        st = st_ref[g]
        y_off = jnp.dot(cg, st.astype(BF16), preferred_element_type=F32) * ecs_e[:, g * gw:(g + 1) * gw]
        st_ref[g] = (ecs_e[L - 1:L, g * gw:(g + 1) * gw] * st
                     + jnp.dot(bg_t, xds[:, g * gw:(g + 1) * gw].astype(BF16), preferred_element_type=F32))
        pairs = []
        for pr in range(2):
            rhs = xdt[:, g * gw + pr * 128:g * gw + (pr + 1) * 128].astype(BF16)
            ys = []
            for hh in range(2):
                h = g * 4 + pr * 2 + hh
                seg = cs[:, h:h + 1] - cs_t[h:h + 1, :]
                dec = jnp.exp(jnp.where(tril, seg, NEG))
                ys.append(jnp.dot((cb * dec).astype(BF16), rhs, preferred_element_type=F32))
            pairs.append(jnp.where(half, ys[0], ys[1]))
        y_groups.append(jnp.concatenate(pairs, axis=1) + y_off)
    y = jnp.concatenate(y_groups, axis=1) + dskip_ref[...] * xs

    v = y * _silu(z_ref[...])
    ms = jnp.mean(v * v, axis=-1, keepdims=True)
    vn = (v * lax.rsqrt(ms + EPS)) * nw_ref[...]
    ys = jnp.dot(vn.astype(BF16), wb_ref[...], preferred_element_type=F32)
    o_ref[...] = jax.nn.sigmoid(g_ref[...]) * ys


def _ssd(slab, bsz, s, conv_w, conv_b, dtb, alog, dskip_e, nw, wb):
    nt = s // SSM_CHUNK
    L = SSM_CHUNK
    tri = jnp.asarray(np.tril(np.ones((L, L), np.float32)), BF16)
    e = np.zeros((128, 3 * SSM_D_INNER), np.float32)
    for k in range(3):
        for h in range(SSM_HEADS):
            e[k * SSM_HEADS + h, k * SSM_D_INNER + h * SSM_HEAD_DIM:k * SSM_D_INNER + (h + 1) * SSM_HEAD_DIM] = 1.0
    e = jnp.asarray(e, BF16)
    rows = lambda b, c: b * nt + c
    const = lambda shape: pl.BlockSpec(shape, lambda b, c: (0,) * len(shape))
    return pl.pallas_call(
        _ssd_kernel,
        out_shape=jax.ShapeDtypeStruct((bsz * s, SSM_D_INNER), F32),
        grid=(bsz, nt),
        in_specs=[
            pl.BlockSpec((L, SSM_CONV_DIM), lambda b, c: (rows(b, c), SLAB_XBC // SSM_CONV_DIM)),
            pl.BlockSpec((8, SSM_CONV_DIM), lambda b, c: (jnp.maximum(rows(b, c) * (L // 8) - 1, 0), 0)),
            pl.BlockSpec((L, 128), lambda b, c: (rows(b, c), SLAB_SMALL // 128)),
            pl.BlockSpec((L, SSM_D_INNER), lambda b, c: (rows(b, c), SLAB_ZSSM // SSM_D_INNER)),
            pl.BlockSpec((L, D_MODEL), lambda b, c: (rows(b, c), SLAB_GSSM // D_MODEL)),
            const((SSM_CONV, SSM_CONV_DIM)), const((1, SSM_CONV_DIM)), const((1, 128)), const((1, 128)),
            const((1, SSM_D_INNER)), const((1, SSM_D_INNER)), const((SSM_D_INNER, D_MODEL)),
            const((L, L)), const((128, 3 * SSM_D_INNER)),
        ],
        out_specs=pl.BlockSpec((L, D_MODEL), lambda b, c: (rows(b, c), 0)),
        scratch_shapes=[pltpu.VMEM((SSM_GROUPS, SSM_STATE, SSM_D_INNER // SSM_GROUPS), F32),
                        pltpu.VMEM((SSM_CONV_DIM // 128, L + 8, 128), F32)],
        compiler_params=_cparams(2),
        name="ssd",
    )(slab, slab, slab, slab, slab, conv_w, conv_b, dtb, alog, dskip_e, nw, wb, tri, e)


def _prep_kernel(q_ref, kc_ref, ks_ref, vs_ref, kw_ref, vw_ref, small_ref, cos_ref, sin_ref,
                 qt_ref, kso_ref, vst_ref, kwo_ref, vwt_ref, kcr_ref, gt_ref):
    cos = cos_ref[...]
    sin = sin_ref[...]

    def rope(x):
        w = x.shape[1]
        cf = jnp.concatenate([cos] * (w // 128), axis=1)
        sf = jnp.concatenate([sin] * (w // 128), axis=1)
        lane = lax.broadcasted_iota(jnp.int32, x.shape, 1)
        first = (lane % HEAD_DIM) < (HEAD_DIM // 2)
        swapped = jnp.where(first, pltpu.roll(x, w - HEAD_DIM // 2, 1), pltpu.roll(x, HEAD_DIM // 2, 1))
        return x * cf + swapped * sf

    halves = TOK_TILE // Q_TILE
    lane128 = lax.broadcasted_iota(jnp.int32, (TOK_TILE, 128), 1)
    row128 = lax.broadcasted_iota(jnp.int32, (TOK_TILE, 128), 0)
    low64 = lane128 < HEAD_DIM
    blocks_per_tile = KEY_TILE // SLC_BLOCK
    tile_in_super = pl.program_id(1) & (SEL_UNROLL - 1)
    block_in_super = tile_in_super * blocks_per_tile + lax.shift_right_logical(row128, int(math.log2(SLC_BLOCK)))
    block_onehot = jnp.where(lane128 - HEAD_DIM == block_in_super, 1.0, 0.0)
    win_pos = ((pl.program_id(1) * halves + lax.shift_right_logical(row128, int(math.log2(Q_TILE))))
               & (WIN_POS - 1))
    win_onehot = jnp.where(lane128 - HEAD_DIM == win_pos, 1.0, 0.0)
    assert TOK_TILE == KEY_TILE and SEL_UNROLL * blocks_per_tile == BF16_ROWS

    def group_cols(x, g, rest):
        r = x if g == 0 else pltpu.roll(x, KV_WIDTH - HEAD_DIM * g, 1)
        return jnp.where(low64, r[:, :128], rest)

    def values_t(v_t, g, lo, hi):
        n = hi - lo
        extra = jnp.where(lax.broadcasted_iota(jnp.int32, (V_ROWS - HEAD_DIM, n), 0) == 0, 1.0, 0.0)
        return jnp.concatenate([v_t[g * HEAD_DIM:(g + 1) * HEAD_DIM, lo:hi], extra], axis=0).astype(BF16)

    qr = rope(q_ref[...]) * (HEAD_DIM ** -0.5 * LOG2E)
    for hf in range(halves):
        q_t = qr[hf * Q_TILE:(hf + 1) * Q_TILE, :].T
        for g in range(ATTN_GROUPS):
            base = g * HEADS_PER_GROUP * HEAD_DIM
            blk = jnp.concatenate([q_t[base + h * HEAD_DIM:base + (h + 1) * HEAD_DIM, :]
                                   for h in range(HEADS_PER_GROUP)], axis=1)
            qt_ref[0, g, hf] = blk.astype(BF16)

    ksr = rope(ks_ref[...])
    vs_t = vs_ref[...].T
    for g in range(ATTN_GROUPS):
        kso_ref[0, g, 0] = group_cols(ksr, g, block_onehot).astype(BF16)
        vst_ref[0, g, 0] = values_t(vs_t, g, 0, TOK_TILE)

    kwr = rope(kw_ref[...])
    vw_t = vw_ref[...].T
    for g in range(ATTN_GROUPS):
        kg = group_cols(kwr, g, win_onehot).astype(BF16)
        for hf in range(halves):
            kwo_ref[0, g, hf] = kg[hf * Q_TILE:(hf + 1) * Q_TILE, :]
            vwt_ref[0, g, hf] = values_t(vw_t, g, hf * Q_TILE, (hf + 1) * Q_TILE)

    kcr_ref[...] = rope(kc_ref[...])

    gates_t = jax.nn.sigmoid(small_ref[...]).T
    zeros_g = jnp.zeros((8 - 3, QCOLS), F32)
    for hf in range(halves):
        for g in range(ATTN_GROUPS):
            rows_c = []
            for cc in range(3):
                rows_c.append(jnp.concatenate(
                    [gates_t[GATE_LANE0 + (g * HEADS_PER_GROUP + r) * 3 + cc:GATE_LANE0 + (g * HEADS_PER_GROUP + r) * 3 + cc + 1,
                             hf * Q_TILE:(hf + 1) * Q_TILE] for r in range(HEADS_PER_GROUP)], axis=1))
            gt_ref[0, g, hf] = jnp.concatenate(rows_c + [zeros_g], axis=0)


def _prep(slab, bsz, s, cos_t, sin_t):
    nt = s // TOK_TILE
    nq = s // Q_TILE
    halves = TOK_TILE // Q_TILE
    G = ATTN_GROUPS
    rows = lambda b, i: b * nt + i
    kvspec = lambda k: pl.BlockSpec((TOK_TILE, KV_WIDTH), lambda b, i: (rows(b, i), SLAB_KV // KV_WIDTH + k))
    out5 = lambda blk: pl.BlockSpec(blk, lambda b, i: (b, 0, i, 0, 0))
    return pl.pallas_call(
        _prep_kernel,
        out_shape=[
            jax.ShapeDtypeStruct((bsz, G, nq, HEAD_DIM, QCOLS), BF16),
            jax.ShapeDtypeStruct((bsz, G, nt, KEY_TILE, 128), BF16),
            jax.ShapeDtypeStruct((bsz, G, nt, V_ROWS, KEY_TILE), BF16),
            jax.ShapeDtypeStruct((bsz, G, nq, Q_TILE, 128), BF16),
            jax.ShapeDtypeStruct((bsz, G, nq, V_ROWS, Q_TILE), BF16),
            jax.ShapeDtypeStruct((bsz * s, KV_WIDTH), F32),
            jax.ShapeDtypeStruct((bsz, G, nq, 8, QCOLS), F32),
        ],
        grid=(bsz, nt),
        in_specs=[
            pl.BlockSpec((TOK_TILE, ATTN_WIDTH), lambda b, i: (rows(b, i), SLAB_Q // ATTN_WIDTH)),
            kvspec(0), kvspec(2), kvspec(3), kvspec(4), kvspec(5),
            pl.BlockSpec((TOK_TILE, 128), lambda b, i: (rows(b, i), SLAB_SMALL // 128)),
            pl.BlockSpec((TOK_TILE, 128), lambda b, i: (i, 0)),
            pl.BlockSpec((TOK_TILE, 128), lambda b, i: (i, 0)),
        ],
        out_specs=[
            out5((1, G, halves, HEAD_DIM, QCOLS)),
            out5((1, G, 1, KEY_TILE, 128)),
            out5((1, G, 1, V_ROWS, KEY_TILE)),
            out5((1, G, halves, Q_TILE, 128)),
            out5((1, G, halves, V_ROWS, Q_TILE)),
            pl.BlockSpec((TOK_TILE, KV_WIDTH), lambda b, i: (rows(b, i), 0)),
            out5((1, G, halves, 8, QCOLS)),
        ],
        compiler_params=_cparams(2),
        name="prep",
    )(slab, slab, slab, slab, slab, slab, slab, cos_t, sin_t)


def _compress_kernel(h_ref, pos_ref, w1_ref, b1_ref, w2_ref, nat_ref, tr_ref):
    h = h_ref[0, 0, 0]
    nh = h.shape[0]
    w1 = w1_ref[0]
    a = jnp.dot((h + pos_ref[0, 0]).astype(BF16), w1[:, :CMP_HIDDEN], preferred_element_type=F32)
    b = jnp.dot((h + pos_ref[0, 1]).astype(BF16), w1[:, CMP_HIDDEN:], preferred_element_type=F32)
    hid = a + pltpu.roll(b, nh - 1, 0) + b1_ref[0]
    out = jnp.dot(jax.nn.gelu(hid).astype(BF16), w2_ref[0], preferred_element_type=F32)
    nat_ref[0, 0, 0] = out.astype(BF16)
    tr_ref[0, 0, 0] = out.T[:HEAD_DIM, :].astype(BF16)


def _compress(hkv, pos, w1, b1, w2):
    _, bsz, G, nh, _ = hkv.shape
    return pl.pallas_call(
        _compress_kernel,
        out_shape=[jax.ShapeDtypeStruct((2, bsz, G, nh, 128), BF16),
                   jax.ShapeDtypeStruct((2, bsz, G, HEAD_DIM, nh), BF16)],
        grid=(2, bsz, G),
        in_specs=[
            pl.BlockSpec((1, 1, 1, nh, CMP_STRIDE * HEAD_DIM), lambda k, b, g: (k, b, g, 0, 0)),
            pl.BlockSpec((1, 2, 1, CMP_STRIDE * HEAD_DIM), lambda k, b, g: (k, 0, 0, 0)),
            pl.BlockSpec((1, CMP_STRIDE * HEAD_DIM, 2 * CMP_HIDDEN), lambda k, b, g: (k, 0, 0)),
            pl.BlockSpec((1, 1, CMP_HIDDEN), lambda k, b, g: (k, 0, 0)),
            pl.BlockSpec((1, CMP_HIDDEN, 128), lambda k, b, g: (k, 0, 0)),
        ],
        out_specs=[pl.BlockSpec((1, 1, 1, nh, 128), lambda k, b, g: (k, b, g, 0, 0)),
                   pl.BlockSpec((1, 1, 1, HEAD_DIM, nh), lambda k, b, g: (k, b, g, 0, 0))],
        compiler_params=_cparams(3),
        name="compress",
    )(hkv, pos, w1, b1, w2)


def _attn_kernel(qt_ref, kc_ref, vct_ref, ks_ref, vst_ref, kw_ref, vwt_ref, g_ref, tri_ref, o_ref, p4_ref, oc_ref,
                 bias_ref, biasm_ref, sbuf_a_ref, sbuf_b_ref, sd_ref, *, nh, ns, nst, nq, top_n):
    qi = pl.program_id(2)
    s0 = qi * Q_TILE
    q64 = qt_ref[0, 0, 0]
    q_t = jnp.concatenate([q64, jnp.zeros((128 - HEAD_DIM, QCOLS), BF16)], axis=0)
    t_row = s0 + (lax.broadcasted_iota(jnp.int32, (1, QCOLS), 1) & (Q_TILE - 1))

    def compressed(rows):
        sc = jnp.dot(kc_ref[0, 0, 0, 0:rows, :], q_t, preferred_element_type=F32)
        cend = lax.broadcasted_iota(jnp.int32, (rows, QCOLS), 0) * CMP_STRIDE + (CMP_BLOCK - 1)
        sm = jnp.where(cend <= t_row, sc, NEG)
        mc = jnp.max(sm, axis=0, keepdims=True)
        pc = jnp.exp2(sm - mc)
        lc = jnp.sum(pc, axis=0, keepdims=True)
        pn = pc * jnp.where(t_row >= CMP_BLOCK - 1, 1.0 / lc, 0.0)
        oc_ref[...] = jnp.dot(vct_ref[0, 0, 0, :, 0:rows], pn.astype(BF16), preferred_element_type=F32)
        p4 = pn[:, 0:Q_TILE]
        for h in range(1, HEADS_PER_GROUP):
            p4 = p4 + pn[:, h * Q_TILE:(h + 1) * Q_TILE]
        p4_ref[8:8 + rows, :] = p4
        if rows < nh:
            p4_ref[8 + rows:8 + nh, :] = jnp.zeros((nh - rows, Q_TILE), F32)

    p4_ref[0:8, :] = jnp.zeros((8, Q_TILE), F32)
    n_chunks = nh // CMP_CHUNK
    need = lax.shift_right_logical(qi * (Q_TILE // CMP_STRIDE) + (Q_TILE - CMP_BLOCK) // CMP_STRIDE,
                                   int(math.log2(CMP_CHUNK)))
    for c in range(n_chunks):
        pl.when(need == c)(functools.partial(compressed, (c + 1) * CMP_CHUNK))
    o_c = oc_ref[...]

    imp = p4_ref[pl.ds(7, ns, stride=4), :]
    for cc in range(4):
        imp = imp + p4_ref[pl.ds(8 + cc, ns, stride=4), :]

    blk = lax.broadcasted_iota(jnp.int32, (ns, Q_TILE), 0)
    tok = s0 + lax.broadcasted_iota(jnp.int32, (ns, Q_TILE), 1)
    cur = lax.shift_right_logical(tok, int(math.log2(SLC_BLOCK)))
    future = blk * SLC_BLOCK > tok
    forced = (blk == 0) | (blk == cur) | (blk == cur - 1)
    val = jnp.where(forced, -2.0, jnp.where(future, -1.0, imp))

    zero_rows = jnp.zeros((128 - HEAD_DIM - BF16_ROWS, QCOLS), BF16)
    eye = jnp.where(lax.broadcasted_iota(jnp.int32, (Q_TILE, Q_TILE), 0)
                    == lax.broadcasted_iota(jnp.int32, (Q_TILE, Q_TILE), 1), 1.0, 0.0).astype(BF16)
    causal_tri = tri_ref[0]
    window_tri = tri_ref[1]

    def edge_scores(k_tile, rows16, tri):
        q_aug = jnp.concatenate([q64, rows16, zero_rows, tri], axis=0)
        return jnp.dot(jnp.concatenate([k_tile, eye], axis=1), q_aug, preferred_element_type=F32)

    n_back = WINDOW // Q_TILE
    row16 = lax.broadcasted_iota(jnp.int32, (BF16_ROWS, QCOLS), 0)
    back = (qi - row16) & (WIN_POS - 1)
    rows_w = jnp.where((row16 < WIN_POS) & (back <= n_back) & (qi - back < 0), NEG, 0.0).astype(BF16)
    win_tile = lambda d: jnp.where(qi - d < 0, qi - d + nq, qi - d)
    sw = jnp.concatenate(
        [edge_scores(kw_ref[0, 0, win_tile(0)], rows_w, causal_tri),
         jnp.dot(jnp.concatenate([kw_ref[0, 0, win_tile(d)] for d in range(1, n_back)], axis=0),
                 jnp.concatenate([q64, rows_w, zero_rows], axis=0), preferred_element_type=F32),
         edge_scores(kw_ref[0, 0, win_tile(n_back)], rows_w, window_tri)], axis=0)
    vw = jnp.concatenate([vwt_ref[0, 0, win_tile(d)] for d in range(n_back + 1)], axis=1)

    for _ in range(top_n - 3):
        best = jnp.max(val, axis=0, keepdims=True)
        first = jnp.min(jnp.where(val == best, blk, ns), axis=0, keepdims=True)
        val = jnp.where(blk == first, -2.0, val)
    n_full = lax.shift_right_logical(qi, 1)
    assert KEY_TILE == 2 * Q_TILE
    bias = jnp.where((val == -2.0) & jnp.logical_not(future), 0.0, NEG)
    bias_main = jnp.where(blk < n_full * (KEY_TILE // SLC_BLOCK), bias, NEG)
    for st in range(nst):
        for table, ref in ((bias, bias_ref), (bias_main, biasm_ref)):
            rows = table[st * BF16_ROWS:(st + 1) * BF16_ROWS, :]
            ref[st] = jnp.concatenate([rows] * HEADS_PER_GROUP, axis=1).astype(BF16)

    def produce(st, dst_ref):
        stc = jnp.minimum(st, nst - 1)
        k = ks_ref[0, 0, pl.ds(stc * SEL_UNROLL, SEL_UNROLL)].reshape(SEL_UNROLL * KEY_TILE, 128)
        q_aug = jnp.concatenate([q64, biasm_ref[stc], zero_rows], axis=0)
        s = jnp.dot(k, q_aug, preferred_element_type=F32)
        dst_ref[...] = s
        return jnp.max(s, axis=0, keepdims=True)

    def super_values(i):
        v4 = vst_ref[0, 0, pl.ds(i * SEL_UNROLL, SEL_UNROLL)]
        return jnp.concatenate([v4[j] for j in range(SEL_UNROLL)], axis=1)

    def step(i, carry, cur_ref, next_ref):
        m, acc, mx = carry
        mx_next = produce(i + 1, next_ref)
        m_new = jnp.maximum(m, mx)
        p = jnp.exp2(cur_ref[...] - m_new)
        acc = jnp.exp2(m - m_new) * acc + jnp.dot(super_values(i), p.astype(BF16), preferred_element_type=F32)
        return m_new, acc, mx_next

    def sel_body(i, carry):
        return lax.cond((i & 1) == 0,
                        lambda c: step(i, c, sbuf_a_ref, sbuf_b_ref),
                        lambda c: step(i, c, sbuf_b_ref, sbuf_a_ref), carry)

    n_iter = lax.shift_right_logical(n_full + (SEL_UNROLL - 1), int(math.log2(SEL_UNROLL)))
    last = jnp.maximum(n_iter - 1, 0)
    init = (jnp.full((1, QCOLS), NEG, F32), jnp.zeros((V_ROWS, QCOLS), F32), produce(0, sbuf_a_ref))

    odd = (qi & 1) == 1
    rows_d = bias_ref[lax.shift_right_logical(n_full, int(math.log2(SEL_UNROLL)))]
    all_neg = jnp.full((Q_TILE, QCOLS), NEG, BF16)
    tri_a = jnp.where(odd, jnp.zeros((Q_TILE, QCOLS), BF16), causal_tri)
    tri_b = jnp.where(odd, causal_tri, all_neg)
    sd_a = edge_scores(ks_ref[0, 0, n_full, 0:Q_TILE, :], rows_d, tri_a)
    sd_b = edge_scores(ks_ref[0, 0, n_full, Q_TILE:KEY_TILE, :], rows_d, tri_b)
    sd_ref[0:Q_TILE, :] = sd_a
    sd_ref[Q_TILE:KEY_TILE, :] = sd_b
    mx_d = jnp.maximum(jnp.max(sd_a, axis=0, keepdims=True), jnp.max(sd_b, axis=0, keepdims=True))

    pw = jnp.exp2(sw - jnp.max(sw, axis=0, keepdims=True))
    acc_w = jnp.dot(vw, pw.astype(BF16), preferred_element_type=F32)
    o_w = acc_w[:HEAD_DIM, :] * (1.0 / acc_w[HEAD_DIM:HEAD_DIM + 1, :])

    m_s, acc_s, mx_s = lax.fori_loop(0, last, sel_body, init)

    def finish(cur_ref):
        m_new = jnp.maximum(jnp.maximum(m_s, mx_s), mx_d)
        p_last = jnp.exp2(cur_ref[...] - m_new).astype(BF16)
        p_diag = jnp.exp2(sd_ref[...] - m_new).astype(BF16)
        return (jnp.exp2(m_s - m_new) * acc_s
                + jnp.dot(super_values(last), p_last, preferred_element_type=F32)
                + jnp.dot(vst_ref[0, 0, n_full], p_diag, preferred_element_type=F32))

    acc_s = lax.cond((last & 1) == 0, lambda: finish(sbuf_a_ref), lambda: finish(sbuf_b_ref))
    o_s = acc_s[:HEAD_DIM, :] * (1.0 / acc_s[HEAD_DIM:HEAD_DIM + 1, :])

    gts = g_ref[0, 0, 0]
    o_ref[0, 0, 0] = gts[0:1, :] * o_c + gts[1:2, :] * o_s + gts[2:3, :] * o_w


def _attn(q_t, kc_nat, vc_tr, ks, vst, kw, vwt, gts, s):
    bsz, G, nq = q_t.shape[:3]
    nh = s // CMP_STRIDE
    ns = s // SLC_BLOCK
    nkt = s // KEY_TILE
    nst = nkt // SEL_UNROLL
    assert s % (SEL_UNROLL * KEY_TILE) == 0 and nh % CMP_CHUNK == 0
    top_n = min(SLC_TOP_N, ns)
    per_q = lambda blk: pl.BlockSpec(blk, lambda b, g, i: (b, g, i, 0, 0))
    per_bg = lambda blk: pl.BlockSpec(blk, lambda b, g, i: (b, g, 0, 0, 0))
    kk = np.arange(Q_TILE)[:, None]
    tl = np.tile(np.arange(Q_TILE), HEADS_PER_GROUP)[None, :]
    tri = jnp.asarray(np.stack([np.where(kk <= tl, 0.0, NEG), np.where(kk > tl, 0.0, NEG)]), BF16)
    return pl.pallas_call(
        functools.partial(_attn_kernel, nh=nh, ns=ns, nst=nst, nq=nq, top_n=top_n),
        out_shape=jax.ShapeDtypeStruct((bsz, G, nq, HEAD_DIM, QCOLS), F32),
        grid=(bsz, G, nq),
        in_specs=[
            per_q((1, 1, 1, HEAD_DIM, QCOLS)),
            pl.BlockSpec((1, 1, 1, nh, 128), lambda b, g, i: (0, b, g, 0, 0)),
            pl.BlockSpec((1, 1, 1, HEAD_DIM, nh), lambda b, g, i: (1, b, g, 0, 0)),
            per_bg((1, 1, nkt, KEY_TILE, 128)),
            per_bg((1, 1, nkt, V_ROWS, KEY_TILE)),
            per_bg((1, 1, nq, Q_TILE, 128)),
            per_bg((1, 1, nq, V_ROWS, Q_TILE)),
            per_q((1, 1, 1, 8, QCOLS)),
            pl.BlockSpec((2, Q_TILE, QCOLS), lambda b, g, i: (0, 0, 0)),
        ],
        out_specs=per_q((1, 1, 1, HEAD_DIM, QCOLS)),
        scratch_shapes=[pltpu.VMEM((nh + 8, Q_TILE), F32),
                        pltpu.VMEM((HEAD_DIM, QCOLS), F32),
                        pltpu.VMEM((nst, BF16_ROWS, QCOLS), BF16),
                        pltpu.VMEM((nst, BF16_ROWS, QCOLS), BF16),
                        pltpu.VMEM((SEL_UNROLL * KEY_TILE, QCOLS), F32),
                        pltpu.VMEM((SEL_UNROLL * KEY_TILE, QCOLS), F32),
                        pltpu.VMEM((KEY_TILE, QCOLS), F32)],
        compiler_params=_cparams(3),
        name="attn",
    )(q_t, kc_nat, vc_tr, ks, vst, kw, vwt, gts, tri)


def _final_kernel(x_ref, gs_ref, ot_ref, z_ref, g_ref, wba_ref, wout_ref, fnw_ref, o_ref, *, last):
    halves = TOK_TILE // Q_TILE
    parts = []
    for hf in range(halves):
        rows = []
        for g in range(ATTN_GROUPS):
            blk = ot_ref[0, g, hf]
            for h in range(HEADS_PER_GROUP):
                rows.append(blk[:, h * Q_TILE:(h + 1) * Q_TILE])
        parts.append(jnp.concatenate(rows, axis=0).T)
    o = jnp.concatenate(parts, axis=0)
    y_attn = jnp.dot((o * _silu(z_ref[...])).astype(BF16), wba_ref[...], preferred_element_type=F32)
    merged = gs_ref[...] + jax.nn.sigmoid(g_ref[...]) * y_attn
    xo = x_ref[...] + jnp.dot(merged.astype(BF16), wout_ref[...], preferred_element_type=F32)
    if last:
        ms = jnp.mean(xo * xo, axis=-1, keepdims=True)
        xo = (xo * lax.rsqrt(ms + EPS)) * fnw_ref[...]
    o_ref[...] = xo


def _final(x2, gs, o_t, slab, wba, wout, fnw, bsz, s, last):
    nt = s // TOK_TILE
    halves = TOK_TILE // Q_TILE
    rows = lambda b, i: b * nt + i
    const = lambda shape: pl.BlockSpec(shape, lambda b, i: (0,) * len(shape))
    return pl.pallas_call(
        functools.partial(_final_kernel, last=last),
        out_shape=jax.ShapeDtypeStruct((bsz * s, D_MODEL), F32),
        grid=(bsz, nt),
        in_specs=[
            pl.BlockSpec((TOK_TILE, D_MODEL), lambda b, i: (rows(b, i), 0)),
            pl.BlockSpec((TOK_TILE, D_MODEL), lambda b, i: (rows(b, i), 0)),
            pl.BlockSpec((1, ATTN_GROUPS, halves, HEAD_DIM, QCOLS), lambda b, i: (b, 0, i, 0, 0)),
            pl.BlockSpec((TOK_TILE, ATTN_WIDTH), lambda b, i: (rows(b, i), SLAB_ZATT // ATTN_WIDTH)),
            pl.BlockSpec((TOK_TILE, D_MODEL), lambda b, i: (rows(b, i), SLAB_GATT // D_MODEL)),
            const((ATTN_WIDTH, D_MODEL)), const((D_MODEL, D_MODEL)), const((1, D_MODEL)),
        ],
        out_specs=pl.BlockSpec((TOK_TILE, D_MODEL), lambda b, i: (rows(b, i), 0)),
        compiler_params=_cparams(2),
        name="final",
    )(x2, gs, o_t, slab, slab, wba, wout, fnw)


def _slab_weights(w):
    o = np.cumsum([0, SSM_D_INNER, SSM_CONV_DIM, SSM_HEADS, ATTN_WIDTH, 6 * KV_WIDTH, 3 * ATTN_HEADS,
                   ATTN_WIDTH, D_MODEL, D_MODEL])
    z_ssm, xbc, dt, q, kv, gate, z_att, g_ssm, g_att = [w[:, o[i]:o[i + 1]] for i in range(9)]
    pad = jnp.zeros((w.shape[0], SLAB_W - int(o[-1])), w.dtype)
    return jnp.concatenate([xbc, z_ssm, q, z_att, g_ssm, g_att, kv, dt, gate, pad], axis=1).astype(BF16)


def _rope_tables(s):
    half = HEAD_DIM // 2
    inv_freq = 1.0 / (ROPE_THETA ** (jnp.arange(half, dtype=F32) * 2.0 / HEAD_DIM))
    ang = jnp.arange(s, dtype=F32)[:, None] * inv_freq[None, :]
    cos, sin = jnp.cos(ang), jnp.sin(ang)
    cos_t = jnp.concatenate([cos, cos] * (128 // HEAD_DIM), axis=1)
    sin_t = jnp.concatenate([-sin, sin] * (128 // HEAD_DIM), axis=1)
    return cos_t, sin_t


def _pad_lanes(v, width=128):
    v = v.reshape(1, -1)
    return jnp.concatenate([v, jnp.zeros((1, width - v.shape[1]), v.dtype)], axis=1)


def kernel(x, norm_w, w_in, conv_w, conv_b, dt_bias, a_log, d_skip, ssm_norm_w, cmp_pos_k, cmp_pos_v, cmp_k_w1,
           cmp_k_b1, cmp_k_w2, cmp_v_w1, cmp_v_b1, cmp_v_w2, w_branch_ssm, w_branch_attn, w_out, final_norm_w):
    bsz, s, _ = x.shape
    depth = norm_w.shape[0]
    G = ATTN_GROUPS
    nh = s // CMP_STRIDE
    cos_t, sin_t = _rope_tables(s)
    x2 = x.reshape(bsz * s, D_MODEL)
    for layer in range(depth):
        slab = _proj(x2, norm_w[layer].reshape(1, -1), _slab_weights(w_in[layer]))

        gs = _ssd(slab, bsz, s, conv_w[layer], conv_b[layer].reshape(1, -1), _pad_lanes(dt_bias[layer]),
                  _pad_lanes(a_log[layer]), jnp.repeat(d_skip[layer], SSM_HEAD_DIM).reshape(1, -1),
                  ssm_norm_w[layer].reshape(1, -1), w_branch_ssm[layer].astype(BF16))

        q_t, ks, vst, kw, vwt, kcr, gts = _prep(slab, bsz, s, cos_t, sin_t)

        vcmp = slab[:, SLAB_KV + KV_WIDTH:SLAB_KV + 2 * KV_WIDTH]
        halfblocks = lambda u: u.reshape(bsz, nh, CMP_STRIDE, G, HEAD_DIM).transpose(0, 3, 1, 2, 4).reshape(
            bsz, G, nh, CMP_STRIDE * HEAD_DIM)
        hkv = jnp.stack([halfblocks(kcr), halfblocks(vcmp)])
        pos = jnp.stack([cmp_pos_k[layer], cmp_pos_v[layer]]).reshape(2, 2, 1, CMP_STRIDE * HEAD_DIM)
        w1 = jnp.stack([cmp_k_w1[layer], cmp_v_w1[layer]])
        w1 = jnp.concatenate([w1[:, :CMP_STRIDE * HEAD_DIM], w1[:, CMP_STRIDE * HEAD_DIM:]], axis=2).astype(BF16)
        b1 = jnp.stack([cmp_k_b1[layer], cmp_v_b1[layer]]).reshape(2, 1, CMP_HIDDEN)
        w2 = jnp.stack([cmp_k_w2[layer], cmp_v_w2[layer]])
        w2 = jnp.concatenate([w2, jnp.zeros((2, CMP_HIDDEN, 128 - HEAD_DIM), w2.dtype)], axis=2).astype(BF16)
        c_nat, c_tr = _compress(hkv, pos, w1, b1, w2)

        o_t = _attn(q_t, c_nat, c_tr, ks, vst, kw, vwt, gts, s)

        x2 = _final(x2, gs, o_t, slab, w_branch_attn[layer].astype(BF16), w_out[layer].astype(BF16),
                    final_norm_w.reshape(1, -1), bsz, s, layer == depth - 1)
    return x2.reshape(bsz, s, D_MODEL)
```

```python
import functools
import math

import numpy as np
import jax
import jax.numpy as jnp
from jax import lax
from jax.experimental import pallas as pl
from jax.experimental.pallas import tpu as pltpu

F32 = jnp.float32
BF16 = jnp.bfloat16

EPS = 1e-6
ROPE_THETA = 10000.0
NEG = -1e30

D_MODEL = 1024
SSM_HEADS = 16
SSM_HEAD_DIM = 64
SSM_D_INNER = 1024
SSM_GROUPS = 4
SSM_STATE = 128
SSM_CONV = 4
SSM_CHUNK = 256
SSM_CONV_DIM = 2048
ATTN_HEADS = 16
ATTN_GROUPS = 4
HEAD_DIM = 64
ATTN_WIDTH = 1024
KV_WIDTH = 256
CMP_BLOCK = 32
CMP_STRIDE = 16
CMP_HIDDEN = 256
SLC_BLOCK = 64
SLC_TOP_N = 16
WINDOW = 512
Q_TILE = 128
KEY_TILE = 256
TOK_TILE = 256
HEADS_PER_GROUP = ATTN_HEADS // ATTN_GROUPS
QCOLS = HEADS_PER_GROUP * Q_TILE
SEL_UNROLL = 4
WIN_POS = 8
BF16_ROWS = 16
V_ROWS = HEAD_DIM + BF16_ROWS
LOG2E = math.log2(math.e)

SLAB_XBC = 0
SLAB_ZSSM = 2048
SLAB_Q = 3072
SLAB_ZATT = 4096
SLAB_GSSM = 5120
SLAB_GATT = 6144
SLAB_KV = 7168
SLAB_SMALL = 8704
SLAB_W = 8832
GATE_LANE0 = SSM_HEADS

VMEM_LIMIT = 56 * 1024 * 1024


def _cparams(n_grid):
    return pltpu.CompilerParams(dimension_semantics=("arbitrary",) * n_grid,
                                vmem_limit_bytes=VMEM_LIMIT)


def _silu(x):
    h = 0.5 * x
    return h + h * jnp.tanh(h)


def _proj_kernel(x_ref, nw_ref, w_ref, o_ref):
    x = x_ref[...]
    ms = jnp.mean(x * x, axis=-1, keepdims=True)
    h = (x * lax.rsqrt(ms + EPS)) * nw_ref[...]
    o_ref[...] = jnp.dot(h.astype(BF16), w_ref[...], preferred_element_type=F32)


def _proj(x2, norm_w, w_slab):
    t = x2.shape[0]
    tm, tn = 512, SLAB_W // 3
    return pl.pallas_call(
        _proj_kernel,
        out_shape=jax.ShapeDtypeStruct((t, SLAB_W), F32),
        grid=(SLAB_W // tn, t // tm),
        in_specs=[pl.BlockSpec((tm, D_MODEL), lambda j, i: (i, 0)),
                  pl.BlockSpec((1, D_MODEL), lambda j, i: (0, 0)),
                  pl.BlockSpec((D_MODEL, tn), lambda j, i: (0, j))],
        out_specs=pl.BlockSpec((tm, tn), lambda j, i: (i, j)),
        compiler_params=_cparams(2),
        name="proj",
    )(x2, norm_w, w_slab)


def _split3(v):
    v1 = v.astype(BF16)
    r1 = v - v1.astype(F32)
    v2 = r1.astype(BF16)
    v3 = (r1 - v2.astype(F32)).astype(BF16)
    return v1, v2, v3


def _dot3(parts, rhs, lhs_side=True):
    out = None
    for p in parts:
        d = (jnp.dot(p, rhs, preferred_element_type=F32) if lhs_side
             else jnp.dot(rhs, p, preferred_element_type=F32))
        out = d if out is None else out + d
    return out


def _ssd_kernel(xbc_ref, prev_ref, small_ref, z_ref, g_ref, convw_ref, convb_ref, dtb_ref, alog_ref,
                dskip_ref, nw_ref, wb_ref, tri_ref, exp_ref, o_ref, st_ref, ext_ref):
    c = pl.program_id(1)
    L = SSM_CHUNK

    @pl.when(c == 0)
    def _():
        st_ref[...] = jnp.zeros_like(st_ref)

    prev = jnp.where(c > 0, prev_ref[...], 0.0)
    cols = []
    for j in range(SSM_CONV_DIM // 128):
        lanes = slice(j * 128, (j + 1) * 128)
        ext_ref[j, 0:8, :] = prev[:, lanes]
        ext_ref[j, 8:8 + L, :] = xbc_ref[:, lanes]
        acc = jnp.broadcast_to(convb_ref[:, lanes], (L, 128))
        for k in range(SSM_CONV):
            r0 = 8 - (SSM_CONV - 1) + k
            acc = acc + convw_ref[k:k + 1, lanes] * ext_ref[j, r0:r0 + L, :]
        cols.append(_silu(acc))
    xbc = jnp.concatenate(cols, axis=1)
    xs = xbc[:, :SSM_D_INNER]
    bm = xbc[:, SSM_D_INNER:SSM_D_INNER + SSM_GROUPS * SSM_STATE]
    cm = xbc[:, SSM_D_INNER + SSM_GROUPS * SSM_STATE:]

    lane = lax.broadcasted_iota(jnp.int32, (L, 128), 1)
    head_lane = lane < SSM_HEADS
    u = small_ref[...] + dtb_ref[...]
    dt = jnp.maximum(u, 0.0) + jnp.log1p(jnp.exp(-jnp.abs(u)))
    dt = jnp.where(head_lane, dt, 0.0)
    a = dt * -jnp.exp(alog_ref[...])
    cs = _dot3(_split3(a), tri_ref[...], lhs_side=False)
    cs_last = cs[L - 1:L, :]
    ecs = jnp.exp(cs)
    ds = jnp.exp(cs_last - cs)
    packed = (dt + pltpu.roll(jnp.where(head_lane, ecs, 0.0), SSM_HEADS, 1)
              + pltpu.roll(jnp.where(head_lane, ds, 0.0), 2 * SSM_HEADS, 1))
    expanded = _dot3(_split3(packed)[:2], exp_ref[...])
    dt_e = expanded[:, :SSM_D_INNER]
    ecs_e = expanded[:, SSM_D_INNER:2 * SSM_D_INNER]
    ds_e = expanded[:, 2 * SSM_D_INNER:]
    xdt = xs * dt_e
    xds = xdt * ds_e
    cs_t = cs.T

    row = lax.broadcasted_iota(jnp.int32, (L, L), 0)
    col = lax.broadcasted_iota(jnp.int32, (L, L), 1)
    tril = row >= col
    half = lax.broadcasted_iota(jnp.int32, (L, 128), 1) < SSM_HEAD_DIM
    gw = SSM_D_INNER // SSM_GROUPS

    y_groups = []
    for g in range(SSM_GROUPS):
        bg = bm[:, g * SSM_STATE:(g + 1) * SSM_STATE]
        cg = cm[:, g * SSM_STATE:(g + 1) * SSM_STATE].astype(BF16)
        bg_t = bg.T.astype(BF16)
        cb = jnp.dot(cg, bg_t, preferred_element_type=F32)
        st = st_ref[g]
        y_off = jnp.dot(cg, st.astype(BF16), preferred_element_type=F32) * ecs_e[:, g * gw:(g + 1) * gw]
        st_ref[g] = (ecs_e[L - 1:L, g * gw:(g + 1) * gw] * st
                     + jnp.dot(bg_t, xds[:, g * gw:(g + 1) * gw].astype(BF16), preferred_element_type=F32))
        pairs = []
        for pr in range(2):
            rhs = xdt[:, g * gw + pr * 128:g * gw + (pr + 1) * 128].astype(BF16)
            ys = []
            for hh in range(2):
                h = g * 4 + pr * 2 + hh
                seg = cs[:, h:h + 1] - cs_t[h:h + 1, :]
                dec = jnp.exp(jnp.where(tril, seg, NEG))
                ys.append(jnp.dot((cb * dec).astype(BF16), rhs, preferred_element_type=F32))
            pairs.append(jnp.where(half, ys[0], ys[1]))
        y_groups.append(jnp.concatenate(pairs, axis=1) + y_off)
    y = jnp.concatenate(y_groups, axis=1) + dskip_ref[...] * xs

    v = y * _silu(z_ref[...])
    ms = jnp.mean(v * v, axis=-1, keepdims=True)
    vn = (v * lax.rsqrt(ms + EPS)) * nw_ref[...]
    ys = jnp.dot(vn.astype(BF16), wb_ref[...], preferred_element_type=F32)
    o_ref[...] = jax.nn.sigmoid(g_ref[...]) * ys


def _ssd(slab, bsz, s, conv_w, conv_b, dtb, alog, dskip_e, nw, wb):
    nt = s // SSM_CHUNK
    L = SSM_CHUNK
    tri = jnp.asarray(np.tril(np.ones((L, L), np.float32)), BF16)
    e = np.zeros((128, 3 * SSM_D_INNER), np.float32)
    for k in range(3):
        for h in range(SSM_HEADS):
            e[k * SSM_HEADS + h, k * SSM_D_INNER + h * SSM_HEAD_DIM:k * SSM_D_INNER + (h + 1) * SSM_HEAD_DIM] = 1.0
    e = jnp.asarray(e, BF16)
    rows = lambda b, c: b * nt + c
    const = lambda shape: pl.BlockSpec(shape, lambda b, c: (0,) * len(shape))
    return pl.pallas_call(
        _ssd_kernel,
        out_shape=jax.ShapeDtypeStruct((bsz * s, SSM_D_INNER), F32),
        grid=(bsz, nt),
        in_specs=[
            pl.BlockSpec((L, SSM_CONV_DIM), lambda b, c: (rows(b, c), SLAB_XBC // SSM_CONV_DIM)),
            pl.BlockSpec((8, SSM_CONV_DIM), lambda b, c: (jnp.maximum(rows(b, c) * (L // 8) - 1, 0), 0)),
            pl.BlockSpec((L, 128), lambda b, c: (rows(b, c), SLAB_SMALL // 128)),
            pl.BlockSpec((L, SSM_D_INNER), lambda b, c: (rows(b, c), SLAB_ZSSM // SSM_D_INNER)),
            pl.BlockSpec((L, D_MODEL), lambda b, c: (rows(b, c), SLAB_GSSM // D_MODEL)),
            const((SSM_CONV, SSM_CONV_DIM)), const((1, SSM_CONV_DIM)), const((1, 128)), const((1, 128)),
            const((1, SSM_D_INNER)), const((1, SSM_D_INNER)), const((SSM_D_INNER, D_MODEL)),
            const((L, L)), const((128, 3 * SSM_D_INNER)),
        ],
        out_specs=pl.BlockSpec((L, D_MODEL), lambda b, c: (rows(b, c), 0)),
        scratch_shapes=[pltpu.VMEM((SSM_GROUPS, SSM_STATE, SSM_D_INNER // SSM_GROUPS), F32),
                        pltpu.VMEM((SSM_CONV_DIM // 128, L + 8, 128), F32)],
        compiler_params=_cparams(2),
        name="ssd",
    )(slab, slab, slab, slab, slab, conv_w, conv_b, dtb, alog, dskip_e, nw, wb, tri, e)


def _prep_kernel(q_ref, kc_ref, vc_ref, ks_ref, vs_ref, kw_ref, vw_ref, small_ref, cos_ref, sin_ref,
                 qt_ref, kso_ref, vst_ref, kwo_ref, vwt_ref, hkv_ref, gt_ref, stage_ref):
    cos = cos_ref[...]
    sin = sin_ref[...]

    def rope(x):
        w = x.shape[1]
        cf = jnp.concatenate([cos] * (w // 128), axis=1)
        sf = jnp.concatenate([sin] * (w // 128), axis=1)
        lane = lax.broadcasted_iota(jnp.int32, x.shape, 1)
        first = (lane % HEAD_DIM) < (HEAD_DIM // 2)
        swapped = jnp.where(first, pltpu.roll(x, w - HEAD_DIM // 2, 1), pltpu.roll(x, HEAD_DIM // 2, 1))
        return x * cf + swapped * sf

    halves = TOK_TILE // Q_TILE
    lane128 = lax.broadcasted_iota(jnp.int32, (TOK_TILE, 128), 1)
    row128 = lax.broadcasted_iota(jnp.int32, (TOK_TILE, 128), 0)
    low64 = lane128 < HEAD_DIM
    blocks_per_tile = KEY_TILE // SLC_BLOCK
    tile_in_super = pl.program_id(1) & (SEL_UNROLL - 1)
    block_in_super = tile_in_super * blocks_per_tile + lax.shift_right_logical(row128, int(math.log2(SLC_BLOCK)))
    block_onehot = jnp.where(lane128 - HEAD_DIM == block_in_super, 1.0, 0.0)
    win_pos = ((pl.program_id(1) * halves + lax.shift_right_logical(row128, int(math.log2(Q_TILE))))
               & (WIN_POS - 1))
    win_onehot = jnp.where(lane128 - HEAD_DIM == win_pos, 1.0, 0.0)
    assert TOK_TILE == KEY_TILE and SEL_UNROLL * blocks_per_tile == BF16_ROWS

    def group_cols(x, g, rest):
        r = x if g == 0 else pltpu.roll(x, KV_WIDTH - HEAD_DIM * g, 1)
        return jnp.where(low64, r[:, :128], rest)

    def values_t(v_t, g, lo, hi):
        n = hi - lo
        extra = jnp.where(lax.broadcasted_iota(jnp.int32, (V_ROWS - HEAD_DIM, n), 0) == 0, 1.0, 0.0)
        return jnp.concatenate([v_t[g * HEAD_DIM:(g + 1) * HEAD_DIM, lo:hi], extra], axis=0).astype(BF16)

    qr = rope(q_ref[...]) * (HEAD_DIM ** -0.5 * LOG2E)
    for hf in range(halves):
        q_t = qr[hf * Q_TILE:(hf + 1) * Q_TILE, :].T
        for g in range(ATTN_GROUPS):
            base = g * HEADS_PER_GROUP * HEAD_DIM
            blk = jnp.concatenate([q_t[base + h * HEAD_DIM:base + (h + 1) * HEAD_DIM, :]
                                   for h in range(HEADS_PER_GROUP)], axis=1)
            qt_ref[0, g, hf] = blk.astype(BF16)

    ksr = rope(ks_ref[...])
    vs_t = vs_ref[...].T
    for g in range(ATTN_GROUPS):
        kso_ref[0, g, 0] = group_cols(ksr, g, block_onehot).astype(BF16)
        vst_ref[0, g, 0] = values_t(vs_t, g, 0, TOK_TILE)

    kwr = rope(kw_ref[...])
    vw_t = vw_ref[...].T
    for g in range(ATTN_GROUPS):
        kg = group_cols(kwr, g, win_onehot).astype(BF16)
        for hf in range(halves):
            kwo_ref[0, g, hf] = kg[hf * Q_TILE:(hf + 1) * Q_TILE, :]
            vwt_ref[0, g, hf] = values_t(vw_t, g, hf * Q_TILE, (hf + 1) * Q_TILE)

    per_half = CMP_STRIDE
    n_half = TOK_TILE // per_half
    lane_h = lax.broadcasted_iota(jnp.int32, (n_half, 128), 1) < HEAD_DIM
    for kv, x in enumerate((rope(kc_ref[...]), vc_ref[...])):
        for c in range(KV_WIDTH // 128):
            stage_ref[kv, c] = x[:, c * 128:(c + 1) * 128]
        for c in range(KV_WIDTH // 128):
            pieces = [stage_ref[kv, c, pl.ds(l, n_half, stride=per_half), :] for l in range(per_half)]
            swapped = [pltpu.roll(p, HEAD_DIM, 1) for p in pieces]
            for hi in range(128 // HEAD_DIM):
                tiles = []
                for l in range(0, per_half, 2):
                    if hi == 0:
                        tiles.append(jnp.where(lane_h, pieces[l], swapped[l + 1]))
                    else:
                        tiles.append(jnp.where(lane_h, swapped[l], pieces[l + 1]))
                hkv_ref[kv, 0, c * (128 // HEAD_DIM) + hi] = jnp.concatenate(tiles, axis=1)

    gates_t = jax.nn.sigmoid(small_ref[...]).T
    zeros_g = jnp.zeros((8 - 3, QCOLS), F32)
    for hf in range(halves):
        for g in range(ATTN_GROUPS):
            rows_c = []
            for cc in range(3):
                rows_c.append(jnp.concatenate(
                    [gates_t[GATE_LANE0 + (g * HEADS_PER_GROUP + r) * 3 + cc:GATE_LANE0 + (g * HEADS_PER_GROUP + r) * 3 + cc + 1,
                             hf * Q_TILE:(hf + 1) * Q_TILE] for r in range(HEADS_PER_GROUP)], axis=1))
            gt_ref[0, g, hf] = jnp.concatenate(rows_c + [zeros_g], axis=0)


def _prep(slab, bsz, s, cos_t, sin_t):
    nt = s // TOK_TILE
    nq = s // Q_TILE
    halves = TOK_TILE // Q_TILE
    G = ATTN_GROUPS
    rows = lambda b, i: b * nt + i
    kvspec = lambda k: pl.BlockSpec((TOK_TILE, KV_WIDTH), lambda b, i: (rows(b, i), SLAB_KV // KV_WIDTH + k))
    out5 = lambda blk: pl.BlockSpec(blk, lambda b, i: (b, 0, i, 0, 0))
    return pl.pallas_call(
        _prep_kernel,
        out_shape=[
            jax.ShapeDtypeStruct((bsz, G, nq, HEAD_DIM, QCOLS), BF16),
            jax.ShapeDtypeStruct((bsz, G, nt, KEY_TILE, 128), BF16),
            jax.ShapeDtypeStruct((bsz, G, nt, V_ROWS, KEY_TILE), BF16),
            jax.ShapeDtypeStruct((bsz, G, nq, Q_TILE, 128), BF16),
            jax.ShapeDtypeStruct((bsz, G, nq, V_ROWS, Q_TILE), BF16),
            jax.ShapeDtypeStruct((2, bsz, G, s // CMP_STRIDE, CMP_STRIDE * HEAD_DIM), F32),
            jax.ShapeDtypeStruct((bsz, G, nq, 8, QCOLS), F32),
        ],
        grid=(bsz, nt),
        in_specs=[
            pl.BlockSpec((TOK_TILE, ATTN_WIDTH), lambda b, i: (rows(b, i), SLAB_Q // ATTN_WIDTH)),
            kvspec(0), kvspec(1), kvspec(2), kvspec(3), kvspec(4), kvspec(5),
            pl.BlockSpec((TOK_TILE, 128), lambda b, i: (rows(b, i), SLAB_SMALL // 128)),
            pl.BlockSpec((TOK_TILE, 128), lambda b, i: (i, 0)),
            pl.BlockSpec((TOK_TILE, 128), lambda b, i: (i, 0)),
        ],
        out_specs=[
            out5((1, G, halves, HEAD_DIM, QCOLS)),
            out5((1, G, 1, KEY_TILE, 128)),
            out5((1, G, 1, V_ROWS, KEY_TILE)),
            out5((1, G, halves, Q_TILE, 128)),
            out5((1, G, halves, V_ROWS, Q_TILE)),
            pl.BlockSpec((2, 1, G, TOK_TILE // CMP_STRIDE, CMP_STRIDE * HEAD_DIM), lambda b, i: (0, b, 0, i, 0)),
            out5((1, G, halves, 8, QCOLS)),
        ],
        scratch_shapes=[pltpu.VMEM((2, KV_WIDTH // 128, TOK_TILE, 128), F32)],
        compiler_params=_cparams(2),
        name="prep",
    )(slab, slab, slab, slab, slab, slab, slab, slab, cos_t, sin_t)


def _compress_kernel(h_ref, pos_ref, w1_ref, b1_ref, w2_ref, nat_ref, tr_ref):
    h = h_ref[0, 0, 0]
    nh = h.shape[0]
    w1 = w1_ref[0]
    a = jnp.dot((h + pos_ref[0, 0]).astype(BF16), w1[:, :CMP_HIDDEN], preferred_element_type=F32)
    b = jnp.dot((h + pos_ref[0, 1]).astype(BF16), w1[:, CMP_HIDDEN:], preferred_element_type=F32)
    hid = a + pltpu.roll(b, nh - 1, 0) + b1_ref[0]
    out = jnp.dot(jax.nn.gelu(hid).astype(BF16), w2_ref[0], preferred_element_type=F32)
    nat_ref[0, 0, 0] = out.astype(BF16)
    tr_ref[0, 0, 0] = out.T[:HEAD_DIM, :].astype(BF16)


def _compress(hkv, pos, w1, b1, w2):
    _, bsz, G, nh, _ = hkv.shape
    return pl.pallas_call(
        _compress_kernel,
        out_shape=[jax.ShapeDtypeStruct((2, bsz, G, nh, 128), BF16),
                   jax.ShapeDtypeStruct((2, bsz, G, HEAD_DIM, nh), BF16)],
        grid=(2, bsz, G),
        in_specs=[
            pl.BlockSpec((1, 1, 1, nh, CMP_STRIDE * HEAD_DIM), lambda k, b, g: (k, b, g, 0, 0)),
            pl.BlockSpec((1, 2, 1, CMP_STRIDE * HEAD_DIM), lambda k, b, g: (k, 0, 0, 0)),
            pl.BlockSpec((1, CMP_STRIDE * HEAD_DIM, 2 * CMP_HIDDEN), lambda k, b, g: (k, 0, 0)),
            pl.BlockSpec((1, 1, CMP_HIDDEN), lambda k, b, g: (k, 0, 0)),
            pl.BlockSpec((1, CMP_HIDDEN, 128), lambda k, b, g: (k, 0, 0)),
        ],
        out_specs=[pl.BlockSpec((1, 1, 1, nh, 128), lambda k, b, g: (k, b, g, 0, 0)),
                   pl.BlockSpec((1, 1, 1, HEAD_DIM, nh), lambda k, b, g: (k, b, g, 0, 0))],
        compiler_params=_cparams(3),
        name="compress",
    )(hkv, pos, w1, b1, w2)


def _attn_kernel(qt_ref, qn_ref, kc_ref, vct_ref, ks_ref, vst_ref, kw_ref, vwt_ref, g_ref, tri_ref, o_ref, p4_ref,
                 oc_ref, bias_ref, biasm_ref, sbuf_a_ref, sbuf_b_ref, sd_ref, *, nh, ns, nst, nq, top_n):
    qi = pl.program_id(2)
    slot = qi & 1
    s0 = qi * Q_TILE
    q64 = qt_ref[0, 0, 0]
    lane_tok = lax.broadcasted_iota(jnp.int32, (1, QCOLS), 1) & (Q_TILE - 1)
    zero_half = jnp.zeros((128 - HEAD_DIM, QCOLS), BF16)
    assert KEY_TILE == 2 * Q_TILE

    def cmp_scores(q64x):
        return jnp.dot(kc_ref[0, 0, 0], jnp.concatenate([q64x, zero_half], axis=0),
                       preferred_element_type=F32)

    def select(tile, sc, dst):
        t_row = tile * Q_TILE + lane_tok
        cend = lax.broadcasted_iota(jnp.int32, (nh, QCOLS), 0) * CMP_STRIDE + (CMP_BLOCK - 1)
        sm = jnp.where(cend <= t_row, sc, NEG)
        mc = jnp.max(sm, axis=0, keepdims=True)
        pc = jnp.exp2(sm - mc)
        lc = jnp.sum(pc, axis=0, keepdims=True)
        pn = pc * jnp.where(t_row >= CMP_BLOCK - 1, 1.0 / lc, 0.0)
        oc_ref[dst] = jnp.dot(vct_ref[0, 0, 0], pn.astype(BF16), preferred_element_type=F32)
        p4 = pn[:, 0:Q_TILE]
        for h in range(1, HEADS_PER_GROUP):
            p4 = p4 + pn[:, h * Q_TILE:(h + 1) * Q_TILE]
        p4_ref[0:8, :] = jnp.zeros((8, Q_TILE), F32)
        p4_ref[8:8 + nh, :] = p4
        imp = p4_ref[pl.ds(7, ns, stride=4), :]
        for cc in range(4):
            imp = imp + p4_ref[pl.ds(8 + cc, ns, stride=4), :]
        blk = lax.broadcasted_iota(jnp.int32, (ns, Q_TILE), 0)
        tok = tile * Q_TILE + lax.broadcasted_iota(jnp.int32, (ns, Q_TILE), 1)
        cur = lax.shift_right_logical(tok, int(math.log2(SLC_BLOCK)))
        future = blk * SLC_BLOCK > tok
        forced = (blk == 0) | (blk == cur) | (blk == cur - 1)
        val = jnp.where(forced, -2.0, jnp.where(future, -1.0, imp))
        for _ in range(top_n - 3):
            best = jnp.max(val, axis=0, keepdims=True)
            first = jnp.min(jnp.where(val == best, blk, ns), axis=0, keepdims=True)
            val = jnp.where(blk == first, -2.0, val)
        bias = jnp.where((val == -2.0) & jnp.logical_not(future), 0.0, NEG)
        full_blocks = lax.shift_right_logical(tile, 1) * (KEY_TILE // SLC_BLOCK)
        bias_main = jnp.where(blk < full_blocks, bias, NEG)
        for st in range(nst):
            for table, ref in ((bias, bias_ref), (bias_main, biasm_ref)):
                rows = table[st * BF16_ROWS:(st + 1) * BF16_ROWS, :]
                ref[dst, st] = jnp.concatenate([rows] * HEADS_PER_GROUP, axis=1).astype(BF16)

    @pl.when(qi == 0)
    def _():
        select(0, cmp_scores(q64), 0)

    sc_next = cmp_scores(qn_ref[0, 0, 0])

    zero_rows = jnp.zeros((128 - HEAD_DIM - BF16_ROWS, QCOLS), BF16)
    eye = jnp.where(lax.broadcasted_iota(jnp.int32, (Q_TILE, Q_TILE), 0)
                    == lax.broadcasted_iota(jnp.int32, (Q_TILE, Q_TILE), 1), 1.0, 0.0).astype(BF16)
    causal_tri = tri_ref[0]
    window_tri = tri_ref[1]

    def edge_scores(k_tile, rows16, tri):
        q_aug = jnp.concatenate([q64, rows16, zero_rows, tri], axis=0)
        return jnp.dot(jnp.concatenate([k_tile, eye], axis=1), q_aug, preferred_element_type=F32)

    n_back = WINDOW // Q_TILE
    row16 = lax.broadcasted_iota(jnp.int32, (BF16_ROWS, QCOLS), 0)
    back = (qi - row16) & (WIN_POS - 1)
    rows_w = jnp.where((row16 < WIN_POS) & (back <= n_back) & (qi - back < 0), NEG, 0.0).astype(BF16)
    win_tile = lambda d: jnp.where(qi - d < 0, qi - d + nq, qi - d)
    sw = jnp.concatenate(
        [edge_scores(kw_ref[0, 0, win_tile(0)], rows_w, causal_tri),
         jnp.dot(jnp.concatenate([kw_ref[0, 0, win_tile(d)] for d in range(1, n_back)], axis=0),
                 jnp.concatenate([q64, rows_w, zero_rows], axis=0), preferred_element_type=F32),
         edge_scores(kw_ref[0, 0, win_tile(n_back)], rows_w, window_tri)], axis=0)
    vw = jnp.concatenate([vwt_ref[0, 0, win_tile(d)] for d in range(n_back + 1)], axis=1)

    n_full = lax.shift_right_logical(qi, 1)

    def produce(st, dst_ref):
        stc = jnp.minimum(st, nst - 1)
        k = ks_ref[0, 0, pl.ds(stc * SEL_UNROLL, SEL_UNROLL)].reshape(SEL_UNROLL * KEY_TILE, 128)
        q_aug = jnp.concatenate([q64, biasm_ref[slot, stc], zero_rows], axis=0)
        s = jnp.dot(k, q_aug, preferred_element_type=F32)
        dst_ref[...] = s
        return jnp.max(s, axis=0, keepdims=True)

    def super_values(i):
        v4 = vst_ref[0, 0, pl.ds(i * SEL_UNROLL, SEL_UNROLL)]
        return jnp.concatenate([v4[j] for j in range(SEL_UNROLL)], axis=1)

    def step(i, carry, cur_ref, next_ref):
        m, acc, mx = carry
        mx_next = produce(i + 1, next_ref)
        m_new = jnp.maximum(m, mx)
        p = jnp.exp2(cur_ref[...] - m_new)
        acc = jnp.exp2(m - m_new) * acc + jnp.dot(super_values(i), p.astype(BF16), preferred_element_type=F32)
        return m_new, acc, mx_next

    def sel_body(i, carry):
        return lax.cond((i & 1) == 0,
                        lambda c: step(i, c, sbuf_a_ref, sbuf_b_ref),
                        lambda c: step(i, c, sbuf_b_ref, sbuf_a_ref), carry)

    n_iter = lax.shift_right_logical(n_full + (SEL_UNROLL - 1), int(math.log2(SEL_UNROLL)))
    last = jnp.maximum(n_iter - 1, 0)
    init = (jnp.full((1, QCOLS), NEG, F32), jnp.zeros((V_ROWS, QCOLS), F32), produce(0, sbuf_a_ref))

    odd = (qi & 1) == 1
    rows_d = bias_ref[slot, lax.shift_right_logical(n_full, int(math.log2(SEL_UNROLL)))]
    all_neg = jnp.full((Q_TILE, QCOLS), NEG, BF16)
    tri_a = jnp.where(odd, jnp.zeros((Q_TILE, QCOLS), BF16), causal_tri)
    tri_b = jnp.where(odd, causal_tri, all_neg)
    sd_a = edge_scores(ks_ref[0, 0, n_full, 0:Q_TILE, :], rows_d, tri_a)
    sd_b = edge_scores(ks_ref[0, 0, n_full, Q_TILE:KEY_TILE, :], rows_d, tri_b)
    sd_ref[0:Q_TILE, :] = sd_a
    sd_ref[Q_TILE:KEY_TILE, :] = sd_b
    mx_d = jnp.maximum(jnp.max(sd_a, axis=0, keepdims=True), jnp.max(sd_b, axis=0, keepdims=True))

    select(jnp.minimum(qi + 1, nq - 1), sc_next, 1 - slot)

    pw = jnp.exp2(sw - jnp.max(sw, axis=0, keepdims=True))
    acc_w = jnp.dot(vw, pw.astype(BF16), preferred_element_type=F32)
    o_w = acc_w[:HEAD_DIM, :] * (1.0 / acc_w[HEAD_DIM:HEAD_DIM + 1, :])

    m_s, acc_s, mx_s = lax.fori_loop(0, last, sel_body, init)

    def finish(cur_ref):
        m_new = jnp.maximum(jnp.maximum(m_s, mx_s), mx_d)
        p_last = jnp.exp2(cur_ref[...] - m_new).astype(BF16)
        p_diag = jnp.exp2(sd_ref[...] - m_new).astype(BF16)
        return (jnp.exp2(m_s - m_new) * acc_s
                + jnp.dot(super_values(last), p_last, preferred_element_type=F32)
                + jnp.dot(vst_ref[0, 0, n_full], p_diag, preferred_element_type=F32))

    acc_s = lax.cond((last & 1) == 0, lambda: finish(sbuf_a_ref), lambda: finish(sbuf_b_ref))
    o_s = acc_s[:HEAD_DIM, :] * (1.0 / acc_s[HEAD_DIM:HEAD_DIM + 1, :])

    gts = g_ref[0, 0, 0]
    o_ref[0, 0, 0] = gts[0:1, :] * oc_ref[slot] + gts[1:2, :] * o_s + gts[2:3, :] * o_w


def _attn(q_t, kc_nat, vc_tr, ks, vst, kw, vwt, gts, s):
    bsz, G, nq = q_t.shape[:3]
    nh = s // CMP_STRIDE
    ns = s // SLC_BLOCK
    nkt = s // KEY_TILE
    nst = nkt // SEL_UNROLL
    assert s % (SEL_UNROLL * KEY_TILE) == 0
    top_n = min(SLC_TOP_N, ns)
    per_q = lambda blk: pl.BlockSpec(blk, lambda b, g, i: (b, g, i, 0, 0))
    per_bg = lambda blk: pl.BlockSpec(blk, lambda b, g, i: (b, g, 0, 0, 0))
    kk = np.arange(Q_TILE)[:, None]
    tl = np.tile(np.arange(Q_TILE), HEADS_PER_GROUP)[None, :]
    tri = jnp.asarray(np.stack([np.where(kk <= tl, 0.0, NEG), np.where(kk > tl, 0.0, NEG)]), BF16)
    return pl.pallas_call(
        functools.partial(_attn_kernel, nh=nh, ns=ns, nst=nst, nq=nq, top_n=top_n),
        out_shape=jax.ShapeDtypeStruct((bsz, G, nq, HEAD_DIM, QCOLS), F32),
        grid=(bsz, G, nq),
        in_specs=[
            per_q((1, 1, 1, HEAD_DIM, QCOLS)),
            pl.BlockSpec((1, 1, 1, HEAD_DIM, QCOLS), lambda b, g, i: (b, g, jnp.minimum(i + 1, nq - 1), 0, 0)),
            pl.BlockSpec((1, 1, 1, nh, 128), lambda b, g, i: (0, b, g, 0, 0)),
            pl.BlockSpec((1, 1, 1, HEAD_DIM, nh), lambda b, g, i: (1, b, g, 0, 0)),
            per_bg((1, 1, nkt, KEY_TILE, 128)),
            per_bg((1, 1, nkt, V_ROWS, KEY_TILE)),
            per_bg((1, 1, nq, Q_TILE, 128)),
            per_bg((1, 1, nq, V_ROWS, Q_TILE)),
            per_q((1, 1, 1, 8, QCOLS)),
            pl.BlockSpec((2, Q_TILE, QCOLS), lambda b, g, i: (0, 0, 0)),
        ],
        out_specs=per_q((1, 1, 1, HEAD_DIM, QCOLS)),
        scratch_shapes=[pltpu.VMEM((nh + 8, Q_TILE), F32),
                        pltpu.VMEM((2, HEAD_DIM, QCOLS), F32),
                        pltpu.VMEM((2, nst, BF16_ROWS, QCOLS), BF16),
                        pltpu.VMEM((2, nst, BF16_ROWS, QCOLS), BF16),
                        pltpu.VMEM((SEL_UNROLL * KEY_TILE, QCOLS), F32),
                        pltpu.VMEM((SEL_UNROLL * KEY_TILE, QCOLS), F32),
                        pltpu.VMEM((KEY_TILE, QCOLS), F32)],
        compiler_params=_cparams(3),
        name="attn",
    )(q_t, q_t, kc_nat, vc_tr, ks, vst, kw, vwt, gts, tri)


def _final_kernel(x_ref, gs_ref, ot_ref, z_ref, g_ref, wba_ref, wout_ref, fnw_ref, o_ref, *, last):
    halves = TOK_TILE // Q_TILE
    parts = []
    for hf in range(halves):
        rows = []
        for g in range(ATTN_GROUPS):
            blk = ot_ref[0, g, hf]
            for h in range(HEADS_PER_GROUP):
                rows.append(blk[:, h * Q_TILE:(h + 1) * Q_TILE])
        parts.append(jnp.concatenate(rows, axis=0).T)
    o = jnp.concatenate(parts, axis=0)
    y_attn = jnp.dot((o * _silu(z_ref[...])).astype(BF16), wba_ref[...], preferred_element_type=F32)
    merged = gs_ref[...] + jax.nn.sigmoid(g_ref[...]) * y_attn
    xo = x_ref[...] + jnp.dot(merged.astype(BF16), wout_ref[...], preferred_element_type=F32)
    if last:
        ms = jnp.mean(xo * xo, axis=-1, keepdims=True)
        xo = (xo * lax.rsqrt(ms + EPS)) * fnw_ref[...]
    o_ref[...] = xo


def _final(x2, gs, o_t, slab, wba, wout, fnw, bsz, s, last):
    nt = s // TOK_TILE
    halves = TOK_TILE // Q_TILE
    rows = lambda b, i: b * nt + i
    const = lambda shape: pl.BlockSpec(shape, lambda b, i: (0,) * len(shape))
    return pl.pallas_call(
        functools.partial(_final_kernel, last=last),
        out_shape=jax.ShapeDtypeStruct((bsz * s, D_MODEL), F32),
        grid=(bsz, nt),
        in_specs=[
            pl.BlockSpec((TOK_TILE, D_MODEL), lambda b, i: (rows(b, i), 0)),
            pl.BlockSpec((TOK_TILE, D_MODEL), lambda b, i: (rows(b, i), 0)),
            pl.BlockSpec((1, ATTN_GROUPS, halves, HEAD_DIM, QCOLS), lambda b, i: (b, 0, i, 0, 0)),
            pl.BlockSpec((TOK_TILE, ATTN_WIDTH), lambda b, i: (rows(b, i), SLAB_ZATT // ATTN_WIDTH)),
            pl.BlockSpec((TOK_TILE, D_MODEL), lambda b, i: (rows(b, i), SLAB_GATT // D_MODEL)),
            const((ATTN_WIDTH, D_MODEL)), const((D_MODEL, D_MODEL)), const((1, D_MODEL)),
        ],
        out_specs=pl.BlockSpec((TOK_TILE, D_MODEL), lambda b, i: (rows(b, i), 0)),
        compiler_params=_cparams(2),
        name="final",
    )(x2, gs, o_t, slab, slab, wba, wout, fnw)


_W_OFF = np.cumsum([0, SSM_D_INNER, SSM_CONV_DIM, SSM_HEADS, ATTN_WIDTH, 6 * KV_WIDTH, 3 * ATTN_HEADS,
                    ATTN_WIDTH, D_MODEL, D_MODEL]).tolist()
_W_ALIGNED = (_W_OFF[-1] // 128) * 128


def _wslab_kernel(w_ref, tail_ref, o_ref):
    x = w_ref[...]
    rows = x.shape[0]
    lane = lax.broadcasted_iota(jnp.int32, (rows, 128), 1)

    def window(lo, hi):
        first = (lo // 128) * 128
        width = -(-(hi - first) // 128) * 128
        win = x[:, first:first + width]
        return win if lo == first else pltpu.roll(win, width - (lo - first), 1)

    qkv = window(_W_OFF[3], _W_OFF[5])
    gates3 = window(_W_OFF[6], _W_ALIGNED)
    n3 = _W_OFF[9] - _W_OFF[6]
    last = jnp.where(lane < 128 - (_W_OFF[9] - _W_ALIGNED), gates3[:, n3 - 128:n3],
                     pltpu.roll(tail_ref[...], 128 - (_W_OFF[9] - _W_ALIGNED), 1))
    small = jnp.where(lane < SSM_HEADS, x[:, _W_OFF[2]:_W_OFF[2] + 128],
                      jnp.where(lane < GATE_LANE0 + 3 * ATTN_HEADS,
                                x[:, (_W_OFF[5] // 128) * 128:(_W_OFF[5] // 128) * 128 + 128], 0.0))
    assert _W_OFF[2] % 128 == 0 and _W_OFF[5] % 128 == GATE_LANE0
    o_ref[...] = jnp.concatenate(
        [x[:, _W_OFF[1]:_W_OFF[2]], x[:, _W_OFF[0]:_W_OFF[1]], qkv[:, :ATTN_WIDTH], gates3[:, :n3 - 128], last,
         qkv[:, ATTN_WIDTH:ATTN_WIDTH + 6 * KV_WIDTH], small], axis=1).astype(BF16)


def _slab_weights(w):
    rows = 128
    tail = jnp.pad(w[:, _W_ALIGNED:], ((0, 0), (0, 128 - (w.shape[1] - _W_ALIGNED))))
    return pl.pallas_call(
        _wslab_kernel,
        out_shape=jax.ShapeDtypeStruct((w.shape[0], SLAB_W), BF16),
        grid=(w.shape[0] // rows,),
        in_specs=[pl.BlockSpec((rows, _W_ALIGNED), lambda i: (i, 0)),
                  pl.BlockSpec((rows, 128), lambda i: (i, 0))],
        out_specs=pl.BlockSpec((rows, SLAB_W), lambda i: (i, 0)),
        compiler_params=_cparams(1),
        name="wslab",
    )(w, tail)


def _rope_tables(s):
    half = HEAD_DIM // 2
    inv_freq = 1.0 / (ROPE_THETA ** (np.arange(half, dtype=np.float64) * 2.0 / HEAD_DIM))
    ang = np.arange(s, dtype=np.float64)[:, None] * inv_freq[None, :]
    cos, sin = np.cos(ang), np.sin(ang)
    cos_t = np.concatenate([cos, cos] * (128 // HEAD_DIM), axis=1)
    sin_t = np.concatenate([-sin, sin] * (128 // HEAD_DIM), axis=1)
    return jnp.asarray(cos_t, F32), jnp.asarray(sin_t, F32)


def _pad_lanes(v, width=128):
    v = v.reshape(1, -1)
    return jnp.concatenate([v, jnp.zeros((1, width - v.shape[1]), v.dtype)], axis=1)


def kernel(x, norm_w, w_in, conv_w, conv_b, dt_bias, a_log, d_skip, ssm_norm_w, cmp_pos_k, cmp_pos_v, cmp_k_w1,
           cmp_k_b1, cmp_k_w2, cmp_v_w1, cmp_v_b1, cmp_v_w2, w_branch_ssm, w_branch_attn, w_out, final_norm_w):
    bsz, s, _ = x.shape
    depth = norm_w.shape[0]
    cos_t, sin_t = _rope_tables(s)
    x2 = x.reshape(bsz * s, D_MODEL)
    for layer in range(depth):
        slab = _proj(x2, norm_w[layer].reshape(1, -1), _slab_weights(w_in[layer]))

        gs = _ssd(slab, bsz, s, conv_w[layer], conv_b[layer].reshape(1, -1), _pad_lanes(dt_bias[layer]),
                  _pad_lanes(a_log[layer]), jnp.repeat(d_skip[layer], SSM_HEAD_DIM).reshape(1, -1),
                  ssm_norm_w[layer].reshape(1, -1), w_branch_ssm[layer].astype(BF16))

        q_t, ks, vst, kw, vwt, hkv, gts = _prep(slab, bsz, s, cos_t, sin_t)

        pos = jnp.stack([cmp_pos_k[layer], cmp_pos_v[layer]]).reshape(2, 2, 1, CMP_STRIDE * HEAD_DIM)
        w1 = jnp.stack([cmp_k_w1[layer], cmp_v_w1[layer]])
        w1 = jnp.concatenate([w1[:, :CMP_STRIDE * HEAD_DIM], w1[:, CMP_STRIDE * HEAD_DIM:]], axis=2).astype(BF16)
        b1 = jnp.stack([cmp_k_b1[layer], cmp_v_b1[layer]]).reshape(2, 1, CMP_HIDDEN)
        w2 = jnp.stack([cmp_k_w2[layer], cmp_v_w2[layer]])
        w2 = jnp.concatenate([w2, jnp.zeros((2, CMP_HIDDEN, 128 - HEAD_DIM), w2.dtype)], axis=2).astype(BF16)
        c_nat, c_tr = _compress(hkv, pos, w1, b1, w2)

        o_t = _attn(q_t, c_nat, c_tr, ks, vst, kw, vwt, gts, s)

        x2 = _final(x2, gs, o_t, slab, w_branch_attn[layer].astype(BF16), w_out[layer].astype(BF16),
                    final_norm_w.reshape(1, -1), bsz, s, layer == depth - 1)
    return x2.reshape(bsz, s, D_MODEL)
```

```python
import functools
import math

import numpy as np
import jax
import jax.numpy as jnp
from jax import lax
from jax.experimental import pallas as pl
from jax.experimental.pallas import tpu as pltpu

F32 = jnp.float32
BF16 = jnp.bfloat16

EPS = 1e-6
ROPE_THETA = 10000.0
NEG = -1e30

D_MODEL = 1024
SSM_HEADS = 16
SSM_HEAD_DIM = 64
SSM_D_INNER = 1024
SSM_GROUPS = 4
SSM_STATE = 128
SSM_CONV = 4
SSM_CHUNK = 256
SSM_CONV_DIM = 2048
ATTN_HEADS = 16
ATTN_GROUPS = 4
HEAD_DIM = 64
ATTN_WIDTH = 1024
KV_WIDTH = 256
CMP_BLOCK = 32
CMP_STRIDE = 16
CMP_HIDDEN = 256
SLC_BLOCK = 64
SLC_TOP_N = 16
WINDOW = 512
Q_TILE = 128
KEY_TILE = 256
TOK_TILE = 256
HEADS_PER_GROUP = ATTN_HEADS // ATTN_GROUPS
QCOLS = HEADS_PER_GROUP * Q_TILE
SEL_UNROLL = 4
WIN_POS = 8
BF16_ROWS = 16
V_ROWS = HEAD_DIM + BF16_ROWS
LOG2E = math.log2(math.e)

SLAB_XBC = 0
SLAB_ZSSM = 2048
SLAB_Q = 3072
SLAB_ZATT = 4096
SLAB_GSSM = 5120
SLAB_GATT = 6144
SLAB_KV = 7168
SLAB_SMALL = 8704
SLAB_W = 8832
GATE_LANE0 = SSM_HEADS

VMEM_LIMIT = 56 * 1024 * 1024


def _cparams(n_grid):
    return pltpu.CompilerParams(dimension_semantics=("arbitrary",) * n_grid,
                                vmem_limit_bytes=VMEM_LIMIT)


def _silu(x):
    h = 0.5 * x
    return h + h * jnp.tanh(h)


def _proj_kernel(x_ref, nw_ref, w_ref, o_ref):
    x = x_ref[...]
    ms = jnp.mean(x * x, axis=-1, keepdims=True)
    h = (x * lax.rsqrt(ms + EPS)) * nw_ref[...]
    o_ref[...] = jnp.dot(h.astype(BF16), w_ref[...], preferred_element_type=F32)


def _proj(x2, norm_w, w_slab):
    t = x2.shape[0]
    tm, tn = 512, SLAB_W // 3
    return pl.pallas_call(
        _proj_kernel,
        out_shape=jax.ShapeDtypeStruct((t, SLAB_W), F32),
        grid=(SLAB_W // tn, t // tm),
        in_specs=[pl.BlockSpec((tm, D_MODEL), lambda j, i: (i, 0)),
                  pl.BlockSpec((1, D_MODEL), lambda j, i: (0, 0)),
                  pl.BlockSpec((D_MODEL, tn), lambda j, i: (0, j))],
        out_specs=pl.BlockSpec((tm, tn), lambda j, i: (i, j)),
        compiler_params=_cparams(2),
        name="proj",
    )(x2, norm_w, w_slab)


def _split3(v):
    v1 = v.astype(BF16)
    r1 = v - v1.astype(F32)
    v2 = r1.astype(BF16)
    v3 = (r1 - v2.astype(F32)).astype(BF16)
    return v1, v2, v3


def _dot3(parts, rhs, lhs_side=True):
    out = None
    for p in parts:
        d = (jnp.dot(p, rhs, preferred_element_type=F32) if lhs_side
             else jnp.dot(rhs, p, preferred_element_type=F32))
        out = d if out is None else out + d
    return out


def _ssd_kernel(xbc_ref, prev_ref, small_ref, z_ref, g_ref, convw_ref, convb_ref, dtb_ref, alog_ref,
                dskip_ref, nw_ref, wb_ref, tri_ref, exp_ref, o_ref, st_ref, ext_ref):
    c = pl.program_id(1)
    L = SSM_CHUNK

    @pl.when(c == 0)
    def _():
        st_ref[...] = jnp.zeros_like(st_ref)

    prev = jnp.where(c > 0, prev_ref[...], 0.0)
    cols = []
    for j in range(SSM_CONV_DIM // 128):
        lanes = slice(j * 128, (j + 1) * 128)
        ext_ref[j, 0:8, :] = prev[:, lanes]
        ext_ref[j, 8:8 + L, :] = xbc_ref[:, lanes]
        acc = jnp.broadcast_to(convb_ref[:, lanes], (L, 128))
        for k in range(SSM_CONV):
            r0 = 8 - (SSM_CONV - 1) + k
            acc = acc + convw_ref[k:k + 1, lanes] * ext_ref[j, r0:r0 + L, :]
        cols.append(_silu(acc))
    xbc = jnp.concatenate(cols, axis=1)
    xs = xbc[:, :SSM_D_INNER]
    bm = xbc[:, SSM_D_INNER:SSM_D_INNER + SSM_GROUPS * SSM_STATE]
    cm = xbc[:, SSM_D_INNER + SSM_GROUPS * SSM_STATE:]

    lane = lax.broadcasted_iota(jnp.int32, (L, 128), 1)
    head_lane = lane < SSM_HEADS
    u = small_ref[...] + dtb_ref[...]
    dt = jnp.maximum(u, 0.0) + jnp.log1p(jnp.exp(-jnp.abs(u)))
    dt = jnp.where(head_lane, dt, 0.0)
    a = dt * -jnp.exp(alog_ref[...])
    cs = _dot3(_split3(a), tri_ref[...], lhs_side=False)
    cs_last = cs[L - 1:L, :]
    ecs = jnp.exp(cs)
    ds = jnp.exp(cs_last - cs)
    packed = (dt + pltpu.roll(jnp.where(head_lane, ecs, 0.0), SSM_HEADS, 1)
              + pltpu.roll(jnp.where(head_lane, ds, 0.0), 2 * SSM_HEADS, 1))
    expanded = _dot3(_split3(packed)[:2], exp_ref[...])
    dt_e = expanded[:, :SSM_D_INNER]
    ecs_e = expanded[:, SSM_D_INNER:2 * SSM_D_INNER]
    ds_e = expanded[:, 2 * SSM_D_INNER:]
    xdt = xs * dt_e
    xds = xdt * ds_e
    cs_t = cs.T

    row = lax.broadcasted_iota(jnp.int32, (L, L), 0)
    col = lax.broadcasted_iota(jnp.int32, (L, L), 1)
    tril = row >= col
    half = lax.broadcasted_iota(jnp.int32, (L, 128), 1) < SSM_HEAD_DIM
    gw = SSM_D_INNER // SSM_GROUPS

    y_groups = []
    for g in range(SSM_GROUPS):
        bg = bm[:, g * SSM_STATE:(g + 1) * SSM_STATE]
        cg = cm[:, g * SSM_STATE:(g + 1) * SSM_STATE].astype(BF16)
        bg_t = bg.T.astype(BF16)
        cb = jnp.dot(cg, bg_t, preferred_element_type=F32)
        st = st_ref[g]
        y_off = jnp.dot(cg, st.astype(BF16), preferred_element_type=F32) * ecs_e[:, g * gw:(g + 1) * gw]
        st_ref[g] = (ecs_e[L - 1:L, g * gw:(g + 1) * gw] * st
                     + jnp.dot(bg_t, xds[:, g * gw:(g + 1) * gw].astype(BF16), preferred_element_type=F32))
        pairs = []
        for pr in range(2):
            rhs = xdt[:, g * gw + pr * 128:g * gw + (pr + 1) * 128].astype(BF16)
            ys = []
            for hh in range(2):
                h = g * 4 + pr * 2 + hh
                seg = cs[:, h:h + 1] - cs_t[h:h + 1, :]
                dec = jnp.exp(jnp.where(tril, seg, NEG))
                ys.append(jnp.dot((cb * dec).astype(BF16), rhs, preferred_element_type=F32))
            pairs.append(jnp.where(half, ys[0], ys[1]))
        y_groups.append(jnp.concatenate(pairs, axis=1) + y_off)
    y = jnp.concatenate(y_groups, axis=1) + dskip_ref[...] * xs

    v = y * _silu(z_ref[...])
    ms = jnp.mean(v * v, axis=-1, keepdims=True)
    vn = (v * lax.rsqrt(ms + EPS)) * nw_ref[...]
    ys = jnp.dot(vn.astype(BF16), wb_ref[...], preferred_element_type=F32)
    o_ref[...] = (jax.nn.sigmoid(g_ref[...]) * ys).astype(o_ref.dtype)


def _ssd(slab, bsz, s, conv_w, conv_b, dtb, alog, dskip_e, nw, wb):
    nt = s // SSM_CHUNK
    L = SSM_CHUNK
    tri = jnp.asarray(np.tril(np.ones((L, L), np.float32)), BF16)
    e = np.zeros((128, 3 * SSM_D_INNER), np.float32)
    for k in range(3):
        for h in range(SSM_HEADS):
            e[k * SSM_HEADS + h, k * SSM_D_INNER + h * SSM_HEAD_DIM:k * SSM_D_INNER + (h + 1) * SSM_HEAD_DIM] = 1.0
    e = jnp.asarray(e, BF16)
    rows = lambda b, c: b * nt + c
    const = lambda shape: pl.BlockSpec(shape, lambda b, c: (0,) * len(shape))
    return pl.pallas_call(
        _ssd_kernel,
        out_shape=jax.ShapeDtypeStruct((bsz * s, SSM_D_INNER), BF16),
        grid=(bsz, nt),
        in_specs=[
            pl.BlockSpec((L, SSM_CONV_DIM), lambda b, c: (rows(b, c), SLAB_XBC // SSM_CONV_DIM)),
            pl.BlockSpec((8, SSM_CONV_DIM), lambda b, c: (jnp.maximum(rows(b, c) * (L // 8) - 1, 0), 0)),
            pl.BlockSpec((L, 128), lambda b, c: (rows(b, c), SLAB_SMALL // 128)),
            pl.BlockSpec((L, SSM_D_INNER), lambda b, c: (rows(b, c), SLAB_ZSSM // SSM_D_INNER)),
            pl.BlockSpec((L, D_MODEL), lambda b, c: (rows(b, c), SLAB_GSSM // D_MODEL)),
            const((SSM_CONV, SSM_CONV_DIM)), const((1, SSM_CONV_DIM)), const((1, 128)), const((1, 128)),
            const((1, SSM_D_INNER)), const((1, SSM_D_INNER)), const((SSM_D_INNER, D_MODEL)),
            const((L, L)), const((128, 3 * SSM_D_INNER)),
        ],
        out_specs=pl.BlockSpec((L, D_MODEL), lambda b, c: (rows(b, c), 0)),
        scratch_shapes=[pltpu.VMEM((SSM_GROUPS, SSM_STATE, SSM_D_INNER // SSM_GROUPS), F32),
                        pltpu.VMEM((SSM_CONV_DIM // 128, L + 8, 128), F32)],
        compiler_params=_cparams(2),
        name="ssd",
    )(slab, slab, slab, slab, slab, conv_w, conv_b, dtb, alog, dskip_e, nw, wb, tri, e)


def _prep_kernel(q_ref, kc_ref, vc_ref, ks_ref, vs_ref, kw_ref, vw_ref, small_ref, cos_ref, sin_ref,
                 qt_ref, kso_ref, vst_ref, kwo_ref, vwt_ref, hkv_ref, gt_ref, stage_ref):
    cos = cos_ref[...]
    sin = sin_ref[...]

    def rope(x):
        w = x.shape[1]
        cf = jnp.concatenate([cos] * (w // 128), axis=1)
        sf = jnp.concatenate([sin] * (w // 128), axis=1)
        lane = lax.broadcasted_iota(jnp.int32, x.shape, 1)
        first = (lane % HEAD_DIM) < (HEAD_DIM // 2)
        swapped = jnp.where(first, pltpu.roll(x, w - HEAD_DIM // 2, 1), pltpu.roll(x, HEAD_DIM // 2, 1))
        return x * cf + swapped * sf

    halves = TOK_TILE // Q_TILE
    lane128 = lax.broadcasted_iota(jnp.int32, (TOK_TILE, 128), 1)
    row128 = lax.broadcasted_iota(jnp.int32, (TOK_TILE, 128), 0)
    low64 = lane128 < HEAD_DIM
    blocks_per_tile = KEY_TILE // SLC_BLOCK
    tile_in_super = pl.program_id(1) & (SEL_UNROLL - 1)
    block_in_super = tile_in_super * blocks_per_tile + lax.shift_right_logical(row128, int(math.log2(SLC_BLOCK)))
    block_onehot = jnp.where(lane128 - HEAD_DIM == block_in_super, 1.0, 0.0)
    win_pos = ((pl.program_id(1) * halves + lax.shift_right_logical(row128, int(math.log2(Q_TILE))))
               & (WIN_POS - 1))
    win_onehot = jnp.where(lane128 - HEAD_DIM == win_pos, 1.0, 0.0)
    assert TOK_TILE == KEY_TILE and SEL_UNROLL * blocks_per_tile == BF16_ROWS

    def group_cols(x, g, rest):
        r = x if g == 0 else pltpu.roll(x, KV_WIDTH - HEAD_DIM * g, 1)
        return jnp.where(low64, r[:, :128], rest)

    def values_t(v_t, g, lo, hi):
        n = hi - lo
        extra = jnp.where(lax.broadcasted_iota(jnp.int32, (V_ROWS - HEAD_DIM, n), 0) == 0, 1.0, 0.0)
        return jnp.concatenate([v_t[g * HEAD_DIM:(g + 1) * HEAD_DIM, lo:hi], extra], axis=0).astype(BF16)

    qr = rope(q_ref[...]) * (HEAD_DIM ** -0.5 * LOG2E)
    for hf in range(halves):
        q_t = qr[hf * Q_TILE:(hf + 1) * Q_TILE, :].T
        for g in range(ATTN_GROUPS):
            base = g * HEADS_PER_GROUP * HEAD_DIM
            blk = jnp.concatenate([q_t[base + h * HEAD_DIM:base + (h + 1) * HEAD_DIM, :]
                                   for h in range(HEADS_PER_GROUP)], axis=1)
            qt_ref[0, g, hf] = blk.astype(BF16)

    ksr = rope(ks_ref[...])
    vs_t = vs_ref[...].T
    for g in range(ATTN_GROUPS):
        kso_ref[0, g, 0] = group_cols(ksr, g, block_onehot).astype(BF16)
        vst_ref[0, g, 0] = values_t(vs_t, g, 0, TOK_TILE)

    kwr = rope(kw_ref[...])
    vw_t = vw_ref[...].T
    for g in range(ATTN_GROUPS):
        kg = group_cols(kwr, g, win_onehot).astype(BF16)
        for hf in range(halves):
            kwo_ref[0, g, hf] = kg[hf * Q_TILE:(hf + 1) * Q_TILE, :]
            vwt_ref[0, g, hf] = values_t(vw_t, g, hf * Q_TILE, (hf + 1) * Q_TILE)

    per_half = CMP_STRIDE
    n_half = TOK_TILE // per_half
    lane_h = lax.broadcasted_iota(jnp.int32, (n_half, 128), 1) < HEAD_DIM
    for kv, x in enumerate((rope(kc_ref[...]), vc_ref[...])):
        for c in range(KV_WIDTH // 128):
            stage_ref[kv, c] = x[:, c * 128:(c + 1) * 128]
        for c in range(KV_WIDTH // 128):
            pieces = [stage_ref[kv, c, pl.ds(l, n_half, stride=per_half), :] for l in range(per_half)]
            swapped = [pltpu.roll(p, HEAD_DIM, 1) for p in pieces]
            for hi in range(128 // HEAD_DIM):
                tiles = []
                for l in range(0, per_half, 2):
                    if hi == 0:
                        tiles.append(jnp.where(lane_h, pieces[l], swapped[l + 1]))
                    else:
                        tiles.append(jnp.where(lane_h, swapped[l], pieces[l + 1]))
                hkv_ref[kv, 0, c * (128 // HEAD_DIM) + hi] = jnp.concatenate(tiles, axis=1)

    gates_t = jax.nn.sigmoid(small_ref[...]).T
    zeros_g = jnp.zeros((8 - 3, QCOLS), F32)
    for hf in range(halves):
        for g in range(ATTN_GROUPS):
            rows_c = []
            for cc in range(3):
                rows_c.append(jnp.concatenate(
                    [gates_t[GATE_LANE0 + (g * HEADS_PER_GROUP + r) * 3 + cc:GATE_LANE0 + (g * HEADS_PER_GROUP + r) * 3 + cc + 1,
                             hf * Q_TILE:(hf + 1) * Q_TILE] for r in range(HEADS_PER_GROUP)], axis=1))
            gt_ref[0, g, hf] = jnp.concatenate(rows_c + [zeros_g], axis=0)


def _prep(slab, bsz, s, cos_t, sin_t):
    nt = s // TOK_TILE
    nq = s // Q_TILE
    halves = TOK_TILE // Q_TILE
    G = ATTN_GROUPS
    rows = lambda b, i: b * nt + i
    kvspec = lambda k: pl.BlockSpec((TOK_TILE, KV_WIDTH), lambda b, i: (rows(b, i), SLAB_KV // KV_WIDTH + k))
    out5 = lambda blk: pl.BlockSpec(blk, lambda b, i: (b, 0, i, 0, 0))
    return pl.pallas_call(
        _prep_kernel,
        out_shape=[
            jax.ShapeDtypeStruct((bsz, G, nq, HEAD_DIM, QCOLS), BF16),
            jax.ShapeDtypeStruct((bsz, G, nt, KEY_TILE, 128), BF16),
            jax.ShapeDtypeStruct((bsz, G, nt, V_ROWS, KEY_TILE), BF16),
            jax.ShapeDtypeStruct((bsz, G, nq, Q_TILE, 128), BF16),
            jax.ShapeDtypeStruct((bsz, G, nq, V_ROWS, Q_TILE), BF16),
            jax.ShapeDtypeStruct((2, bsz, G, s // CMP_STRIDE, CMP_STRIDE * HEAD_DIM), F32),
            jax.ShapeDtypeStruct((bsz, G, nq, 8, QCOLS), F32),
        ],
        grid=(bsz, nt),
        in_specs=[
            pl.BlockSpec((TOK_TILE, ATTN_WIDTH), lambda b, i: (rows(b, i), SLAB_Q // ATTN_WIDTH)),
            kvspec(0), kvspec(1), kvspec(2), kvspec(3), kvspec(4), kvspec(5),
            pl.BlockSpec((TOK_TILE, 128), lambda b, i: (rows(b, i), SLAB_SMALL // 128)),
            pl.BlockSpec((TOK_TILE, 128), lambda b, i: (i, 0)),
            pl.BlockSpec((TOK_TILE, 128), lambda b, i: (i, 0)),
        ],
        out_specs=[
            out5((1, G, halves, HEAD_DIM, QCOLS)),
            out5((1, G, 1, KEY_TILE, 128)),
            out5((1, G, 1, V_ROWS, KEY_TILE)),
            out5((1, G, halves, Q_TILE, 128)),
            out5((1, G, halves, V_ROWS, Q_TILE)),
            pl.BlockSpec((2, 1, G, TOK_TILE // CMP_STRIDE, CMP_STRIDE * HEAD_DIM), lambda b, i: (0, b, 0, i, 0)),
            out5((1, G, halves, 8, QCOLS)),
        ],
        scratch_shapes=[pltpu.VMEM((2, KV_WIDTH // 128, TOK_TILE, 128), F32)],
        compiler_params=_cparams(2),
        name="prep",
    )(slab, slab, slab, slab, slab, slab, slab, slab, cos_t, sin_t)


def _compress_kernel(h_ref, pos_ref, w1_ref, b1_ref, w2_ref, nat_ref, tr_ref):
    h = h_ref[0, 0, 0]
    nh = h.shape[0]
    w1 = w1_ref[0]
    a = jnp.dot((h + pos_ref[0, 0]).astype(BF16), w1[:, :CMP_HIDDEN], preferred_element_type=F32)
    b = jnp.dot((h + pos_ref[0, 1]).astype(BF16), w1[:, CMP_HIDDEN:], preferred_element_type=F32)
    hid = a + pltpu.roll(b, nh - 1, 0) + b1_ref[0]
    out = jnp.dot(jax.nn.gelu(hid).astype(BF16), w2_ref[0], preferred_element_type=F32)
    nat_ref[0, 0, 0] = out.astype(BF16)
    tr_ref[0, 0, 0] = out.T[:HEAD_DIM, :].astype(BF16)


def _compress(hkv, pos, w1, b1, w2):
    _, bsz, G, nh, _ = hkv.shape
    return pl.pallas_call(
        _compress_kernel,
        out_shape=[jax.ShapeDtypeStruct((2, bsz, G, nh, 128), BF16),
                   jax.ShapeDtypeStruct((2, bsz, G, HEAD_DIM, nh), BF16)],
        grid=(2, bsz, G),
        in_specs=[
            pl.BlockSpec((1, 1, 1, nh, CMP_STRIDE * HEAD_DIM), lambda k, b, g: (k, b, g, 0, 0)),
            pl.BlockSpec((1, 2, 1, CMP_STRIDE * HEAD_DIM), lambda k, b, g: (k, 0, 0, 0)),
            pl.BlockSpec((1, CMP_STRIDE * HEAD_DIM, 2 * CMP_HIDDEN), lambda k, b, g: (k, 0, 0)),
            pl.BlockSpec((1, 1, CMP_HIDDEN), lambda k, b, g: (k, 0, 0)),
            pl.BlockSpec((1, CMP_HIDDEN, 128), lambda k, b, g: (k, 0, 0)),
        ],
        out_specs=[pl.BlockSpec((1, 1, 1, nh, 128), lambda k, b, g: (k, b, g, 0, 0)),
                   pl.BlockSpec((1, 1, 1, HEAD_DIM, nh), lambda k, b, g: (k, b, g, 0, 0))],
        compiler_params=_cparams(3),
        name="compress",
    )(hkv, pos, w1, b1, w2)


def _attn_kernel(qt_ref, qn_ref, kc_ref, vct_ref, ks_ref, vst_ref, kw_ref, vwt_ref, g_ref, tri_ref, o_ref, p4_ref,
                 imp_ref, oc_ref, bias_ref, biasm_ref, sbuf_a_ref, sbuf_b_ref, sd_ref, *, nh, ns, nst, nq, top_n):
    qi = pl.program_id(2)
    slot = qi & 1
    s0 = qi * Q_TILE
    q64 = qt_ref[0, 0, 0]
    lane_tok = lax.broadcasted_iota(jnp.int32, (1, QCOLS), 1) & (Q_TILE - 1)
    zero_half = jnp.zeros((128 - HEAD_DIM, QCOLS), BF16)
    assert KEY_TILE == 2 * Q_TILE

    def cmp_scores(q64x):
        return jnp.dot(kc_ref[0, 0, 0], jnp.concatenate([q64x, zero_half], axis=0),
                       preferred_element_type=F32)

    def importance(tile, sc, dst):
        t_row = tile * Q_TILE + lane_tok
        cend = lax.broadcasted_iota(jnp.int32, (nh, QCOLS), 0) * CMP_STRIDE + (CMP_BLOCK - 1)
        sm = jnp.where(cend <= t_row, sc, NEG)
        mc = jnp.max(sm, axis=0, keepdims=True)
        pc = jnp.exp2(sm - mc)
        lc = jnp.sum(pc, axis=0, keepdims=True)
        pn = pc * jnp.where(t_row >= CMP_BLOCK - 1, 1.0 / lc, 0.0)
        oc_ref[dst] = jnp.dot(vct_ref[0, 0, 0], pn.astype(BF16), preferred_element_type=F32)
        p4 = pn[:, 0:Q_TILE]
        for h in range(1, HEADS_PER_GROUP):
            p4 = p4 + pn[:, h * Q_TILE:(h + 1) * Q_TILE]
        p4_ref[0:8, :] = jnp.zeros((8, Q_TILE), F32)
        p4_ref[8:8 + nh, :] = p4
        imp = p4_ref[pl.ds(7, ns, stride=4), :]
        for cc in range(4):
            imp = imp + p4_ref[pl.ds(8 + cc, ns, stride=4), :]
        imp_ref[...] = imp

    def pick_blocks(tile, dst):
        imp = imp_ref[...]
        blk = lax.broadcasted_iota(jnp.int32, (ns, Q_TILE), 0)
        tok = tile * Q_TILE + lax.broadcasted_iota(jnp.int32, (ns, Q_TILE), 1)
        cur = lax.shift_right_logical(tok, int(math.log2(SLC_BLOCK)))
        future = blk * SLC_BLOCK > tok
        forced = (blk == 0) | (blk == cur) | (blk == cur - 1)
        val = jnp.where(forced, -2.0, jnp.where(future, -1.0, imp))
        for _ in range(top_n - 3):
            best = jnp.max(val, axis=0, keepdims=True)
            first = jnp.min(jnp.where(val == best, blk, ns), axis=0, keepdims=True)
            val = jnp.where(blk == first, -2.0, val)
        bias = jnp.where((val == -2.0) & jnp.logical_not(future), 0.0, NEG)
        full_blocks = lax.shift_right_logical(tile, 1) * (KEY_TILE // SLC_BLOCK)
        bias_main = jnp.where(blk < full_blocks, bias, NEG)
        for st in range(nst):
            for table, ref in ((bias, bias_ref), (bias_main, biasm_ref)):
                rows = table[st * BF16_ROWS:(st + 1) * BF16_ROWS, :]
                ref[dst, st] = jnp.concatenate([rows] * HEADS_PER_GROUP, axis=1).astype(BF16)

    @pl.when(qi == 0)
    def _():
        importance(0, cmp_scores(q64), 0)
        pick_blocks(0, 0)

    next_tile = jnp.minimum(qi + 1, nq - 1)
    sc_next = cmp_scores(qn_ref[0, 0, 0])

    zero_rows = jnp.zeros((128 - HEAD_DIM - BF16_ROWS, QCOLS), BF16)
    eye = jnp.where(lax.broadcasted_iota(jnp.int32, (Q_TILE, Q_TILE), 0)
                    == lax.broadcasted_iota(jnp.int32, (Q_TILE, Q_TILE), 1), 1.0, 0.0).astype(BF16)
    causal_tri = tri_ref[0]
    window_tri = tri_ref[1]

    def edge_scores(k_tile, rows16, tri):
        q_aug = jnp.concatenate([q64, rows16, zero_rows, tri], axis=0)
        return jnp.dot(jnp.concatenate([k_tile, eye], axis=1), q_aug, preferred_element_type=F32)

    n_back = WINDOW // Q_TILE
    row16 = lax.broadcasted_iota(jnp.int32, (BF16_ROWS, QCOLS), 0)
    back = (qi - row16) & (WIN_POS - 1)
    rows_w = jnp.where((row16 < WIN_POS) & (back <= n_back) & (qi - back < 0), NEG, 0.0).astype(BF16)
    win_tile = lambda d: jnp.where(qi - d < 0, qi - d + nq, qi - d)
    sw = jnp.concatenate(
        [edge_scores(kw_ref[0, 0, win_tile(0)], rows_w, causal_tri),
         jnp.dot(jnp.concatenate([kw_ref[0, 0, win_tile(d)] for d in range(1, n_back)], axis=0),
                 jnp.concatenate([q64, rows_w, zero_rows], axis=0), preferred_element_type=F32),
         edge_scores(kw_ref[0, 0, win_tile(n_back)], rows_w, window_tri)], axis=0)
    vw = jnp.concatenate([vwt_ref[0, 0, win_tile(d)] for d in range(n_back + 1)], axis=1)

    n_full = lax.shift_right_logical(qi, 1)

    def produce_tile(stc, j, dst_ref):
        q_aug = jnp.concatenate([q64, biasm_ref[slot, stc], zero_rows], axis=0)
        s = jnp.dot(ks_ref[0, 0, stc * SEL_UNROLL + j], q_aug, preferred_element_type=F32)
        dst_ref[j * KEY_TILE:(j + 1) * KEY_TILE, :] = s
        return jnp.max(s, axis=0, keepdims=True)

    def produce(st, dst_ref):
        stc = jnp.minimum(st, nst - 1)
        mx = produce_tile(stc, 0, dst_ref)
        for j in range(1, SEL_UNROLL):
            mx = jnp.maximum(mx, produce_tile(stc, j, dst_ref))
        return mx

    def consume_tile(kt, j, cur_ref, m_new):
        p = jnp.exp2(cur_ref[j * KEY_TILE:(j + 1) * KEY_TILE, :] - m_new).astype(BF16)
        return jnp.dot(vst_ref[0, 0, kt], p, preferred_element_type=F32)

    def step(i, carry, cur_ref, next_ref):
        m, acc, mx = carry
        m_new = jnp.maximum(m, mx)
        stc = jnp.minimum(i + 1, nst - 1)
        mx_next, pv = None, None
        for j in range(SEL_UNROLL):
            mxj = produce_tile(stc, j, next_ref)
            mx_next = mxj if mx_next is None else jnp.maximum(mx_next, mxj)
            pvj = consume_tile(i * SEL_UNROLL + j, j, cur_ref, m_new)
            pv = pvj if pv is None else pv + pvj
        return m_new, jnp.exp2(m - m_new) * acc + pv, mx_next

    def sel_body(i, carry):
        return lax.cond((i & 1) == 0,
                        lambda c: step(i, c, sbuf_a_ref, sbuf_b_ref),
                        lambda c: step(i, c, sbuf_b_ref, sbuf_a_ref), carry)

    n_iter = lax.shift_right_logical(n_full + (SEL_UNROLL - 1), int(math.log2(SEL_UNROLL)))
    last = jnp.maximum(n_iter - 1, 0)
    init = (jnp.full((1, QCOLS), NEG, F32), jnp.zeros((V_ROWS, QCOLS), F32), produce(0, sbuf_a_ref))

    odd = (qi & 1) == 1
    rows_d = bias_ref[slot, lax.shift_right_logical(n_full, int(math.log2(SEL_UNROLL)))]
    all_neg = jnp.full((Q_TILE, QCOLS), NEG, BF16)
    tri_a = jnp.where(odd, jnp.zeros((Q_TILE, QCOLS), BF16), causal_tri)
    tri_b = jnp.where(odd, causal_tri, all_neg)
    sd_a = edge_scores(ks_ref[0, 0, n_full, 0:Q_TILE, :], rows_d, tri_a)
    sd_b = edge_scores(ks_ref[0, 0, n_full, Q_TILE:KEY_TILE, :], rows_d, tri_b)
    sd_ref[0:Q_TILE, :] = sd_a
    sd_ref[Q_TILE:KEY_TILE, :] = sd_b
    mx_d = jnp.maximum(jnp.max(sd_a, axis=0, keepdims=True), jnp.max(sd_b, axis=0, keepdims=True))

    importance(next_tile, sc_next, 1 - slot)

    pw = jnp.exp2(sw - jnp.max(sw, axis=0, keepdims=True))
    acc_w = jnp.dot(vw, pw.astype(BF16), preferred_element_type=F32)
    o_w = acc_w[:HEAD_DIM, :] * (1.0 / acc_w[HEAD_DIM:HEAD_DIM + 1, :])

    m_s, acc_s, mx_s = lax.fori_loop(0, last, sel_body, init)

    def finish(cur_ref):
        m_new = jnp.maximum(jnp.maximum(m_s, mx_s), mx_d)
        p_diag = jnp.exp2(sd_ref[...] - m_new).astype(BF16)
        pv = jnp.dot(vst_ref[0, 0, n_full], p_diag, preferred_element_type=F32)
        for j in range(SEL_UNROLL):
            pv = pv + consume_tile(last * SEL_UNROLL + j, j, cur_ref, m_new)
        pick_blocks(next_tile, 1 - slot)
        return jnp.exp2(m_s - m_new) * acc_s + pv

    acc_s = lax.cond((last & 1) == 0, lambda: finish(sbuf_a_ref), lambda: finish(sbuf_b_ref))
    o_s = acc_s[:HEAD_DIM, :] * (1.0 / acc_s[HEAD_DIM:HEAD_DIM + 1, :])

    gts = g_ref[0, 0, 0]
    o_ref[0, 0, 0] = (gts[0:1, :] * oc_ref[slot] + gts[1:2, :] * o_s + gts[2:3, :] * o_w).astype(o_ref.dtype)


def _attn(q_t, kc_nat, vc_tr, ks, vst, kw, vwt, gts, s):
    bsz, G, nq = q_t.shape[:3]
    nh = s // CMP_STRIDE
    ns = s // SLC_BLOCK
    nkt = s // KEY_TILE
    nst = nkt // SEL_UNROLL
    assert s % (SEL_UNROLL * KEY_TILE) == 0
    top_n = min(SLC_TOP_N, ns)
    per_q = lambda blk: pl.BlockSpec(blk, lambda b, g, i: (b, g, i, 0, 0))
    per_bg = lambda blk: pl.BlockSpec(blk, lambda b, g, i: (b, g, 0, 0, 0))
    kk = np.arange(Q_TILE)[:, None]
    tl = np.tile(np.arange(Q_TILE), HEADS_PER_GROUP)[None, :]
    tri = jnp.asarray(np.stack([np.where(kk <= tl, 0.0, NEG), np.where(kk > tl, 0.0, NEG)]), BF16)
    return pl.pallas_call(
        functools.partial(_attn_kernel, nh=nh, ns=ns, nst=nst, nq=nq, top_n=top_n),
        out_shape=jax.ShapeDtypeStruct((bsz, G, nq, HEAD_DIM, QCOLS), BF16),
        grid=(bsz, G, nq),
        in_specs=[
            per_q((1, 1, 1, HEAD_DIM, QCOLS)),
            pl.BlockSpec((1, 1, 1, HEAD_DIM, QCOLS), lambda b, g, i: (b, g, jnp.minimum(i + 1, nq - 1), 0, 0)),
            pl.BlockSpec((1, 1, 1, nh, 128), lambda b, g, i: (0, b, g, 0, 0)),
            pl.BlockSpec((1, 1, 1, HEAD_DIM, nh), lambda b, g, i: (1, b, g, 0, 0)),
            per_bg((1, 1, nkt, KEY_TILE, 128)),
            per_bg((1, 1, nkt, V_ROWS, KEY_TILE)),
            per_bg((1, 1, nq, Q_TILE, 128)),
            per_bg((1, 1, nq, V_ROWS, Q_TILE)),
            per_q((1, 1, 1, 8, QCOLS)),
            pl.BlockSpec((2, Q_TILE, QCOLS), lambda b, g, i: (0, 0, 0)),
        ],
        out_specs=per_q((1, 1, 1, HEAD_DIM, QCOLS)),
        scratch_shapes=[pltpu.VMEM((nh + 8, Q_TILE), F32),
                        pltpu.VMEM((ns, Q_TILE), F32),
                        pltpu.VMEM((2, HEAD_DIM, QCOLS), F32),
                        pltpu.VMEM((2, nst, BF16_ROWS, QCOLS), BF16),
                        pltpu.VMEM((2, nst, BF16_ROWS, QCOLS), BF16),
                        pltpu.VMEM((SEL_UNROLL * KEY_TILE, QCOLS), F32),
                        pltpu.VMEM((SEL_UNROLL * KEY_TILE, QCOLS), F32),
                        pltpu.VMEM((KEY_TILE, QCOLS), F32)],
        compiler_params=_cparams(3),
        name="attn",
    )(q_t, q_t, kc_nat, vc_tr, ks, vst, kw, vwt, gts, tri)


def _final_kernel(x_ref, gs_ref, ot_ref, z_ref, g_ref, wba_ref, wout_ref, fnw_ref, o_ref, *, last):
    halves = TOK_TILE // Q_TILE
    parts = []
    for hf in range(halves):
        rows = []
        for g in range(ATTN_GROUPS):
            blk = ot_ref[0, g, hf].astype(F32)
            for h in range(HEADS_PER_GROUP):
                rows.append(blk[:, h * Q_TILE:(h + 1) * Q_TILE])
        parts.append(jnp.concatenate(rows, axis=0).T)
    o = jnp.concatenate(parts, axis=0)
    y_attn = jnp.dot((o * _silu(z_ref[...])).astype(BF16), wba_ref[...], preferred_element_type=F32)
    merged = gs_ref[...].astype(F32) + jax.nn.sigmoid(g_ref[...]) * y_attn
    xo = x_ref[...] + jnp.dot(merged.astype(BF16), wout_ref[...], preferred_element_type=F32)
    if last:
        ms = jnp.mean(xo * xo, axis=-1, keepdims=True)
        xo = (xo * lax.rsqrt(ms + EPS)) * fnw_ref[...]
    o_ref[...] = xo


def _final(x2, gs, o_t, slab, wba, wout, fnw, bsz, s, last):
    nt = s // TOK_TILE
    halves = TOK_TILE // Q_TILE
    rows = lambda b, i: b * nt + i
    const = lambda shape: pl.BlockSpec(shape, lambda b, i: (0,) * len(shape))
    return pl.pallas_call(
        functools.partial(_final_kernel, last=last),
        out_shape=jax.ShapeDtypeStruct((bsz * s, D_MODEL), F32),
        grid=(bsz, nt),
        in_specs=[
            pl.BlockSpec((TOK_TILE, D_MODEL), lambda b, i: (rows(b, i), 0)),
            pl.BlockSpec((TOK_TILE, D_MODEL), lambda b, i: (rows(b, i), 0)),
            pl.BlockSpec((1, ATTN_GROUPS, halves, HEAD_DIM, QCOLS), lambda b, i: (b, 0, i, 0, 0)),
            pl.BlockSpec((TOK_TILE, ATTN_WIDTH), lambda b, i: (rows(b, i), SLAB_ZATT // ATTN_WIDTH)),
            pl.BlockSpec((TOK_TILE, D_MODEL), lambda b, i: (rows(b, i), SLAB_GATT // D_MODEL)),
            const((ATTN_WIDTH, D_MODEL)), const((D_MODEL, D_MODEL)), const((1, D_MODEL)),
        ],
        out_specs=pl.BlockSpec((TOK_TILE, D_MODEL), lambda b, i: (rows(b, i), 0)),
        compiler_params=_cparams(2),
        name="final",
    )(x2, gs, o_t, slab, slab, wba, wout, fnw)


_W_OFF = np.cumsum([0, SSM_D_INNER, SSM_CONV_DIM, SSM_HEADS, ATTN_WIDTH, 6 * KV_WIDTH, 3 * ATTN_HEADS,
                    ATTN_WIDTH, D_MODEL, D_MODEL]).tolist()
_W_ALIGNED = (_W_OFF[-1] // 128) * 128


def _wslab_kernel(w_ref, tail_ref, o_ref):
    x = w_ref[...]
    rows = x.shape[0]
    lane = lax.broadcasted_iota(jnp.int32, (rows, 128), 1)

    def window(lo, hi):
        first = (lo // 128) * 128
        width = -(-(hi - first) // 128) * 128
        win = x[:, first:first + width]
        return win if lo == first else pltpu.roll(win, width - (lo - first), 1)

    qkv = window(_W_OFF[3], _W_OFF[5])
    gates3 = window(_W_OFF[6], _W_ALIGNED)
    n3 = _W_OFF[9] - _W_OFF[6]
    last = jnp.where(lane < 128 - (_W_OFF[9] - _W_ALIGNED), gates3[:, n3 - 128:n3],
                     pltpu.roll(tail_ref[...], 128 - (_W_OFF[9] - _W_ALIGNED), 1))
    small = jnp.where(lane < SSM_HEADS, x[:, _W_OFF[2]:_W_OFF[2] + 128],
                      jnp.where(lane < GATE_LANE0 + 3 * ATTN_HEADS,
                                x[:, (_W_OFF[5] // 128) * 128:(_W_OFF[5] // 128) * 128 + 128], 0.0))
    assert _W_OFF[2] % 128 == 0 and _W_OFF[5] % 128 == GATE_LANE0
    o_ref[...] = jnp.concatenate(
        [x[:, _W_OFF[1]:_W_OFF[2]], x[:, _W_OFF[0]:_W_OFF[1]], qkv[:, :ATTN_WIDTH], gates3[:, :n3 - 128], last,
         qkv[:, ATTN_WIDTH:ATTN_WIDTH + 6 * KV_WIDTH], small], axis=1).astype(BF16)


def _slab_weights(w_all, layer):
    rows = 128
    _, d, n = w_all.shape
    tail = jnp.pad(w_all[layer, :, _W_ALIGNED:], ((0, 0), (0, 128 - (n - _W_ALIGNED))))
    return pl.pallas_call(
        _wslab_kernel,
        out_shape=jax.ShapeDtypeStruct((d, SLAB_W), BF16),
        grid=(d // rows,),
        in_specs=[pl.BlockSpec((None, rows, _W_ALIGNED), lambda i: (layer, i, 0)),
                  pl.BlockSpec((rows, 128), lambda i: (i, 0))],
        out_specs=pl.BlockSpec((rows, SLAB_W), lambda i: (i, 0)),
        compiler_params=_cparams(1),
        name="wslab",
    )(w_all, tail)


def _rope_tables(s):
    half = HEAD_DIM // 2
    inv_freq = 1.0 / (ROPE_THETA ** (np.arange(half, dtype=np.float64) * 2.0 / HEAD_DIM))
    ang = np.arange(s, dtype=np.float64)[:, None] * inv_freq[None, :]
    cos, sin = np.cos(ang), np.sin(ang)
    cos_t = np.concatenate([cos, cos] * (128 // HEAD_DIM), axis=1)
    sin_t = np.concatenate([-sin, sin] * (128 // HEAD_DIM), axis=1)
    return jnp.asarray(cos_t, F32), jnp.asarray(sin_t, F32)


def _pad_lanes(v, width=128):
    v = v.reshape(1, -1)
    return jnp.concatenate([v, jnp.zeros((1, width - v.shape[1]), v.dtype)], axis=1)


def kernel(x, norm_w, w_in, conv_w, conv_b, dt_bias, a_log, d_skip, ssm_norm_w, cmp_pos_k, cmp_pos_v, cmp_k_w1,
           cmp_k_b1, cmp_k_w2, cmp_v_w1, cmp_v_b1, cmp_v_w2, w_branch_ssm, w_branch_attn, w_out, final_norm_w):
    bsz, s, _ = x.shape
    depth = norm_w.shape[0]
    cos_t, sin_t = _rope_tables(s)
    x2 = x.reshape(bsz * s, D_MODEL)
    for layer in range(depth):
        slab = _proj(x2, norm_w[layer].reshape(1, -1), _slab_weights(w_in, layer))

        gs = _ssd(slab, bsz, s, conv_w[layer], conv_b[layer].reshape(1, -1), _pad_lanes(dt_bias[layer]),
                  _pad_lanes(a_log[layer]), jnp.repeat(d_skip[layer], SSM_HEAD_DIM).reshape(1, -1),
                  ssm_norm_w[layer].reshape(1, -1), w_branch_ssm[layer].astype(BF16))

        q_t, ks, vst, kw, vwt, hkv, gts = _prep(slab, bsz, s, cos_t, sin_t)

        pos = jnp.stack([cmp_pos_k[layer], cmp_pos_v[layer]]).reshape(2, 2, 1, CMP_STRIDE * HEAD_DIM)
        w1 = jnp.stack([cmp_k_w1[layer], cmp_v_w1[layer]])
        w1 = jnp.concatenate([w1[:, :CMP_STRIDE * HEAD_DIM], w1[:, CMP_STRIDE * HEAD_DIM:]], axis=2).astype(BF16)
        b1 = jnp.stack([cmp_k_b1[layer], cmp_v_b1[layer]]).reshape(2, 1, CMP_HIDDEN)
        w2 = jnp.stack([cmp_k_w2[layer], cmp_v_w2[layer]])
        w2 = jnp.concatenate([w2, jnp.zeros((2, CMP_HIDDEN, 128 - HEAD_DIM), w2.dtype)], axis=2).astype(BF16)
        c_nat, c_tr = _compress(hkv, pos, w1, b1, w2)

        o_t = _attn(q_t, c_nat, c_tr, ks, vst, kw, vwt, gts, s)

        x2 = _final(x2, gs, o_t, slab, w_branch_attn[layer].astype(BF16), w_out[layer].astype(BF16),
                    final_norm_w.reshape(1, -1), bsz, s, layer == depth - 1)
    return x2.reshape(bsz, s, D_MODEL)
```

```python
import functools
import math

import numpy as np
import jax
import jax.numpy as jnp
from jax import lax
from jax.experimental import pallas as pl
from jax.experimental.pallas import tpu as pltpu

F32 = jnp.float32
BF16 = jnp.bfloat16

EPS = 1e-6
ROPE_THETA = 10000.0
NEG = -1e30

D_MODEL = 1024
SSM_HEADS = 16
SSM_HEAD_DIM = 64
SSM_D_INNER = 1024
SSM_GROUPS = 4
SSM_STATE = 128
SSM_CONV = 4
SSM_CHUNK = 256
SSM_CONV_DIM = 2048
ATTN_HEADS = 16
ATTN_GROUPS = 4
HEAD_DIM = 64
ATTN_WIDTH = 1024
KV_WIDTH = 256
CMP_BLOCK = 32
CMP_STRIDE = 16
CMP_HIDDEN = 256
SLC_BLOCK = 64
SLC_TOP_N = 16
WINDOW = 512
Q_TILE = 128
KEY_TILE = 256
TOK_TILE = 256
HEADS_PER_GROUP = ATTN_HEADS // ATTN_GROUPS
QCOLS = HEADS_PER_GROUP * Q_TILE
SEL_UNROLL = 4
WIN_POS = 8
BF16_ROWS = 16
V_ROWS = HEAD_DIM + BF16_ROWS
LOG2E = math.log2(math.e)

SLAB_XBC = 0
SLAB_ZSSM = 2048
SLAB_Q = 3072
SLAB_ZATT = 4096
SLAB_GSSM = 5120
SLAB_GATT = 6144
SLAB_KV = 7168
SLAB_SMALL = 8704
SLAB_W = 8832
GATE_LANE0 = SSM_HEADS

VMEM_LIMIT = 56 * 1024 * 1024


def _cparams(n_grid):
    return pltpu.CompilerParams(dimension_semantics=("arbitrary",) * n_grid,
                                vmem_limit_bytes=VMEM_LIMIT)


def _silu(x):
    h = 0.5 * x
    return h + h * jnp.tanh(h)


def _proj_kernel(x_ref, nw_ref, w_ref, o_ref):
    x = x_ref[...]
    ms = jnp.mean(x * x, axis=-1, keepdims=True)
    h = (x * lax.rsqrt(ms + EPS)) * nw_ref[...]
    o_ref[...] = lax.dot_general(h.astype(BF16), w_ref[...], (((1,), (1,)), ((), ())),
                                 preferred_element_type=F32)


def _proj(x2, norm_w, w_slab_t):
    t = x2.shape[0]
    tm, tn = 1024, SLAB_W // 3
    return pl.pallas_call(
        _proj_kernel,
        out_shape=jax.ShapeDtypeStruct((t, SLAB_W), F32),
        grid=(SLAB_W // tn, t // tm),
        in_specs=[pl.BlockSpec((tm, D_MODEL), lambda j, i: (i, 0)),
                  pl.BlockSpec((1, D_MODEL), lambda j, i: (0, 0)),
                  pl.BlockSpec((tn, D_MODEL), lambda j, i: (j, 0))],
        out_specs=pl.BlockSpec((tm, tn), lambda j, i: (i, j)),
        compiler_params=_cparams(2),
        name="proj",
    )(x2, norm_w, w_slab_t)


def _split3(v):
    v1 = v.astype(BF16)
    r1 = v - v1.astype(F32)
    v2 = r1.astype(BF16)
    v3 = (r1 - v2.astype(F32)).astype(BF16)
    return v1, v2, v3


def _dot3(parts, rhs, lhs_side=True):
    out = None
    for p in parts:
        d = (jnp.dot(p, rhs, preferred_element_type=F32) if lhs_side
             else jnp.dot(rhs, p, preferred_element_type=F32))
        out = d if out is None else out + d
    return out


def _ssd_kernel(xbc_ref, prev_ref, small_ref, z_ref, g_ref, convw_ref, convb_ref, dtb_ref, alog_ref,
                dskip_ref, nw_ref, wb_ref, tri_ref, exp_ref, o_ref, st_ref, ext_ref):
    c = pl.program_id(1)
    L = SSM_CHUNK

    @pl.when(c == 0)
    def _():
        st_ref[...] = jnp.zeros_like(st_ref)

    prev = jnp.where(c > 0, prev_ref[...], 0.0)
    cols = []
    for j in range(SSM_CONV_DIM // 128):
        lanes = slice(j * 128, (j + 1) * 128)
        ext_ref[j, 0:8, :] = prev[:, lanes]
        ext_ref[j, 8:8 + L, :] = xbc_ref[:, lanes]
        acc = jnp.broadcast_to(convb_ref[:, lanes], (L, 128))
        for k in range(SSM_CONV):
            r0 = 8 - (SSM_CONV - 1) + k
            acc = acc + convw_ref[k:k + 1, lanes] * ext_ref[j, r0:r0 + L, :]
        cols.append(_silu(acc))
    xbc = jnp.concatenate(cols, axis=1)
    xs = xbc[:, :SSM_D_INNER]
    bm = xbc[:, SSM_D_INNER:SSM_D_INNER + SSM_GROUPS * SSM_STATE]
    cm = xbc[:, SSM_D_INNER + SSM_GROUPS * SSM_STATE:]

    lane = lax.broadcasted_iota(jnp.int32, (L, 128), 1)
    head_lane = lane < SSM_HEADS
    u = small_ref[...] + dtb_ref[...]
    dt = jnp.maximum(u, 0.0) + jnp.log1p(jnp.exp(-jnp.abs(u)))
    dt = jnp.where(head_lane, dt, 0.0)
    a = dt * -jnp.exp(alog_ref[...])
    cs = _dot3(_split3(a), tri_ref[...], lhs_side=False)
    cs_last = cs[L - 1:L, :]
    ecs = jnp.exp(cs)
    ds = jnp.exp(cs_last - cs)
    packed = (dt + pltpu.roll(jnp.where(head_lane, ecs, 0.0), SSM_HEADS, 1)
              + pltpu.roll(jnp.where(head_lane, ds, 0.0), 2 * SSM_HEADS, 1))
    expanded = _dot3(_split3(packed)[:2], exp_ref[...])
    dt_e = expanded[:, :SSM_D_INNER]
    ecs_e = expanded[:, SSM_D_INNER:2 * SSM_D_INNER]
    ds_e = expanded[:, 2 * SSM_D_INNER:]
    xdt = xs * dt_e
    xds = xdt * ds_e
    cs_t = cs.T

    row = lax.broadcasted_iota(jnp.int32, (L, L), 0)
    col = lax.broadcasted_iota(jnp.int32, (L, L), 1)
    tril = row >= col
    half = lax.broadcasted_iota(jnp.int32, (L, 128), 1) < SSM_HEAD_DIM
    gw = SSM_D_INNER // SSM_GROUPS

    y_groups = []
    for g in range(SSM_GROUPS):
        bg = bm[:, g * SSM_STATE:(g + 1) * SSM_STATE]
        cg = cm[:, g * SSM_STATE:(g + 1) * SSM_STATE].astype(BF16)
        bg_t = bg.T.astype(BF16)
        cb = jnp.dot(cg, bg_t, preferred_element_type=F32)
        st = st_ref[g]
        y_off = jnp.dot(cg, st.astype(BF16), preferred_element_type=F32) * ecs_e[:, g * gw:(g + 1) * gw]
        st_ref[g] = (ecs_e[L - 1:L, g * gw:(g + 1) * gw] * st
                     + jnp.dot(bg_t, xds[:, g * gw:(g + 1) * gw].astype(BF16), preferred_element_type=F32))
        pairs = []
        for pr in range(2):
            rhs = xdt[:, g * gw + pr * 128:g * gw + (pr + 1) * 128].astype(BF16)
            ys = []
            for hh in range(2):
                h = g * 4 + pr * 2 + hh
                seg = cs[:, h:h + 1] - cs_t[h:h + 1, :]
                dec = jnp.exp(jnp.where(tril, seg, NEG))
                ys.append(jnp.dot((cb * dec).astype(BF16), rhs, preferred_element_type=F32))
            pairs.append(jnp.where(half, ys[0], ys[1]))
        y_groups.append(jnp.concatenate(pairs, axis=1) + y_off)
    y = jnp.concatenate(y_groups, axis=1) + dskip_ref[...] * xs

    v = y * _silu(z_ref[...])
    ms = jnp.mean(v * v, axis=-1, keepdims=True)
    vn = (v * lax.rsqrt(ms + EPS)) * nw_ref[...]
    ys = jnp.dot(vn.astype(BF16), wb_ref[...], preferred_element_type=F32)
    o_ref[...] = (jax.nn.sigmoid(g_ref[...]) * ys).astype(o_ref.dtype)


def _ssd(slab, bsz, s, conv_w, conv_b, dtb, alog, dskip_e, nw, wb):
    nt = s // SSM_CHUNK
    L = SSM_CHUNK
    tri = jnp.asarray(np.tril(np.ones((L, L), np.float32)), BF16)
    e = np.zeros((128, 3 * SSM_D_INNER), np.float32)
    for k in range(3):
        for h in range(SSM_HEADS):
            e[k * SSM_HEADS + h, k * SSM_D_INNER + h * SSM_HEAD_DIM:k * SSM_D_INNER + (h + 1) * SSM_HEAD_DIM] = 1.0
    e = jnp.asarray(e, BF16)
    rows = lambda b, c: b * nt + c
    const = lambda shape: pl.BlockSpec(shape, lambda b, c: (0,) * len(shape))
    return pl.pallas_call(
        _ssd_kernel,
        out_shape=jax.ShapeDtypeStruct((bsz * s, SSM_D_INNER), BF16),
        grid=(bsz, nt),
        in_specs=[
            pl.BlockSpec((L, SSM_CONV_DIM), lambda b, c: (rows(b, c), SLAB_XBC // SSM_CONV_DIM)),
            pl.BlockSpec((8, SSM_CONV_DIM), lambda b, c: (jnp.maximum(rows(b, c) * (L // 8) - 1, 0), 0)),
            pl.BlockSpec((L, 128), lambda b, c: (rows(b, c), SLAB_SMALL // 128)),
            pl.BlockSpec((L, SSM_D_INNER), lambda b, c: (rows(b, c), SLAB_ZSSM // SSM_D_INNER)),
            pl.BlockSpec((L, D_MODEL), lambda b, c: (rows(b, c), SLAB_GSSM // D_MODEL)),
            const((SSM_CONV, SSM_CONV_DIM)), const((1, SSM_CONV_DIM)), const((1, 128)), const((1, 128)),
            const((1, SSM_D_INNER)), const((1, SSM_D_INNER)), const((SSM_D_INNER, D_MODEL)),
            const((L, L)), const((128, 3 * SSM_D_INNER)),
        ],
        out_specs=pl.BlockSpec((L, D_MODEL), lambda b, c: (rows(b, c), 0)),
        scratch_shapes=[pltpu.VMEM((SSM_GROUPS, SSM_STATE, SSM_D_INNER // SSM_GROUPS), F32),
                        pltpu.VMEM((SSM_CONV_DIM // 128, L + 8, 128), F32)],
        compiler_params=_cparams(2),
        name="ssd",
    )(slab, slab, slab, slab, slab, conv_w, conv_b, dtb, alog, dskip_e, nw, wb, tri, e)


def _prep_kernel(q_ref, kc_ref, vc_ref, ks_ref, vs_ref, kw_ref, vw_ref, small_ref, cos_ref, sin_ref,
                 qt_ref, kso_ref, vst_ref, kwo_ref, vwt_ref, hkv_ref, gt_ref, stage_ref):
    cos = cos_ref[...]
    sin = sin_ref[...]

    def rope(x):
        w = x.shape[1]
        cf = jnp.concatenate([cos] * (w // 128), axis=1)
        sf = jnp.concatenate([sin] * (w // 128), axis=1)
        lane = lax.broadcasted_iota(jnp.int32, x.shape, 1)
        first = (lane % HEAD_DIM) < (HEAD_DIM // 2)
        swapped = jnp.where(first, pltpu.roll(x, w - HEAD_DIM // 2, 1), pltpu.roll(x, HEAD_DIM // 2, 1))
        return x * cf + swapped * sf

    halves = TOK_TILE // Q_TILE
    lane128 = lax.broadcasted_iota(jnp.int32, (TOK_TILE, 128), 1)
    row128 = lax.broadcasted_iota(jnp.int32, (TOK_TILE, 128), 0)
    low64 = lane128 < HEAD_DIM
    blocks_per_tile = KEY_TILE // SLC_BLOCK
    tile_in_super = pl.program_id(1) & (SEL_UNROLL - 1)
    block_in_super = tile_in_super * blocks_per_tile + lax.shift_right_logical(row128, int(math.log2(SLC_BLOCK)))
    block_onehot = jnp.where(lane128 - HEAD_DIM == block_in_super, 1.0, 0.0)
    win_pos = ((pl.program_id(1) * halves + lax.shift_right_logical(row128, int(math.log2(Q_TILE))))
               & (WIN_POS - 1))
    win_onehot = jnp.where(lane128 - HEAD_DIM == win_pos, 1.0, 0.0)
    assert TOK_TILE == KEY_TILE and SEL_UNROLL * blocks_per_tile == BF16_ROWS

    def group_cols(x, g, rest):
        r = x if g == 0 else pltpu.roll(x, KV_WIDTH - HEAD_DIM * g, 1)
        return jnp.where(low64, r[:, :128], rest)

    def values_t(v_t, g, lo, hi):
        n = hi - lo
        extra = jnp.where(lax.broadcasted_iota(jnp.int32, (V_ROWS - HEAD_DIM, n), 0) == 0, 1.0, 0.0)
        return jnp.concatenate([v_t[g * HEAD_DIM:(g + 1) * HEAD_DIM, lo:hi], extra], axis=0).astype(BF16)

    qr = rope(q_ref[...]) * (HEAD_DIM ** -0.5 * LOG2E)
    for hf in range(halves):
        q_t = qr[hf * Q_TILE:(hf + 1) * Q_TILE, :].T
        for g in range(ATTN_GROUPS):
            base = g * HEADS_PER_GROUP * HEAD_DIM
            blk = jnp.concatenate([q_t[base + h * HEAD_DIM:base + (h + 1) * HEAD_DIM, :]
                                   for h in range(HEADS_PER_GROUP)], axis=1)
            qt_ref[0, g, hf] = blk.astype(BF16)

    ksr = rope(ks_ref[...])
    vs_t = vs_ref[...].T
    for g in range(ATTN_GROUPS):
        kso_ref[0, g, 0] = group_cols(ksr, g, block_onehot).astype(BF16)
        vst_ref[0, g, 0] = values_t(vs_t, g, 0, TOK_TILE)

    kwr = rope(kw_ref[...])
    vw_t = vw_ref[...].T
    for g in range(ATTN_GROUPS):
        kg = group_cols(kwr, g, win_onehot).astype(BF16)
        for hf in range(halves):
            kwo_ref[0, g, hf] = kg[hf * Q_TILE:(hf + 1) * Q_TILE, :]
            vwt_ref[0, g, hf] = values_t(vw_t, g, hf * Q_TILE, (hf + 1) * Q_TILE)

    per_half = CMP_STRIDE
    n_half = TOK_TILE // per_half
    lane_h = lax.broadcasted_iota(jnp.int32, (n_half, 128), 1) < HEAD_DIM
    for kv, x in enumerate((rope(kc_ref[...]), vc_ref[...])):
        for c in range(KV_WIDTH // 128):
            stage_ref[kv, c] = x[:, c * 128:(c + 1) * 128]
        for c in range(KV_WIDTH // 128):
            pieces = [stage_ref[kv, c, pl.ds(l, n_half, stride=per_half), :] for l in range(per_half)]
            swapped = [pltpu.roll(p, HEAD_DIM, 1) for p in pieces]
            for hi in range(128 // HEAD_DIM):
                tiles = []
                for l in range(0, per_half, 2):
                    if hi == 0:
                        tiles.append(jnp.where(lane_h, pieces[l], swapped[l + 1]))
                    else:
                        tiles.append(jnp.where(lane_h, swapped[l], pieces[l + 1]))
                hkv_ref[kv, 0, c * (128 // HEAD_DIM) + hi] = jnp.concatenate(tiles, axis=1)

    gates_t = jax.nn.sigmoid(small_ref[...]).T
    zeros_g = jnp.zeros((8 - 3, QCOLS), F32)
    for hf in range(halves):
        for g in range(ATTN_GROUPS):
            rows_c = []
            for cc in range(3):
                rows_c.append(jnp.concatenate(
                    [gates_t[GATE_LANE0 + (g * HEADS_PER_GROUP + r) * 3 + cc:GATE_LANE0 + (g * HEADS_PER_GROUP + r) * 3 + cc + 1,
                             hf * Q_TILE:(hf + 1) * Q_TILE] for r in range(HEADS_PER_GROUP)], axis=1))
            gt_ref[0, g, hf] = jnp.concatenate(rows_c + [zeros_g], axis=0)


def _prep(slab, bsz, s, cos_t, sin_t):
    nt = s // TOK_TILE
    nq = s // Q_TILE
    halves = TOK_TILE // Q_TILE
    G = ATTN_GROUPS
    rows = lambda b, i: b * nt + i
    kvspec = lambda k: pl.BlockSpec((TOK_TILE, KV_WIDTH), lambda b, i: (rows(b, i), SLAB_KV // KV_WIDTH + k))
    out5 = lambda blk: pl.BlockSpec(blk, lambda b, i: (b, 0, i, 0, 0))
    return pl.pallas_call(
        _prep_kernel,
        out_shape=[
            jax.ShapeDtypeStruct((bsz, G, nq, HEAD_DIM, QCOLS), BF16),
            jax.ShapeDtypeStruct((bsz, G, nt, KEY_TILE, 128), BF16),
            jax.ShapeDtypeStruct((bsz, G, nt, V_ROWS, KEY_TILE), BF16),
            jax.ShapeDtypeStruct((bsz, G, nq, Q_TILE, 128), BF16),
            jax.ShapeDtypeStruct((bsz, G, nq, V_ROWS, Q_TILE), BF16),
            jax.ShapeDtypeStruct((2, bsz, G, s // CMP_STRIDE, CMP_STRIDE * HEAD_DIM), F32),
            jax.ShapeDtypeStruct((bsz, G, nq, 8, QCOLS), F32),
        ],
        grid=(bsz, nt),
        in_specs=[
            pl.BlockSpec((TOK_TILE, ATTN_WIDTH), lambda b, i: (rows(b, i), SLAB_Q // ATTN_WIDTH)),
            kvspec(0), kvspec(1), kvspec(2), kvspec(3), kvspec(4), kvspec(5),
            pl.BlockSpec((TOK_TILE, 128), lambda b, i: (rows(b, i), SLAB_SMALL // 128)),
            pl.BlockSpec((TOK_TILE, 128), lambda b, i: (i, 0)),
            pl.BlockSpec((TOK_TILE, 128), lambda b, i: (i, 0)),
        ],
        out_specs=[
            out5((1, G, halves, HEAD_DIM, QCOLS)),
            out5((1, G, 1, KEY_TILE, 128)),
            out5((1, G, 1, V_ROWS, KEY_TILE)),
            out5((1, G, halves, Q_TILE, 128)),
            out5((1, G, halves, V_ROWS, Q_TILE)),
            pl.BlockSpec((2, 1, G, TOK_TILE // CMP_STRIDE, CMP_STRIDE * HEAD_DIM), lambda b, i: (0, b, 0, i, 0)),
            out5((1, G, halves, 8, QCOLS)),
        ],
        scratch_shapes=[pltpu.VMEM((2, KV_WIDTH // 128, TOK_TILE, 128), F32)],
        compiler_params=_cparams(2),
        name="prep",
    )(slab, slab, slab, slab, slab, slab, slab, slab, cos_t, sin_t)


def _compress_kernel(h_ref, pos_ref, w1_ref, b1_ref, w2_ref, nat_ref, tr_ref):
    h = h_ref[0, 0, 0]
    nh = h.shape[0]
    w1 = w1_ref[0]
    a = jnp.dot((h + pos_ref[0, 0]).astype(BF16), w1[:, :CMP_HIDDEN], preferred_element_type=F32)
    b = jnp.dot((h + pos_ref[0, 1]).astype(BF16), w1[:, CMP_HIDDEN:], preferred_element_type=F32)
    hid = a + pltpu.roll(b, nh - 1, 0) + b1_ref[0]
    out = jnp.dot(jax.nn.gelu(hid).astype(BF16), w2_ref[0], preferred_element_type=F32)
    nat_ref[0, 0, 0] = out.astype(BF16)
    tr_ref[0, 0, 0] = out.T[:HEAD_DIM, :].astype(BF16)


def _compress(hkv, pos, w1, b1, w2):
    _, bsz, G, nh, _ = hkv.shape
    return pl.pallas_call(
        _compress_kernel,
        out_shape=[jax.ShapeDtypeStruct((2, bsz, G, nh, 128), BF16),
                   jax.ShapeDtypeStruct((2, bsz, G, HEAD_DIM, nh), BF16)],
        grid=(2, bsz, G),
        in_specs=[
            pl.BlockSpec((1, 1, 1, nh, CMP_STRIDE * HEAD_DIM), lambda k, b, g: (k, b, g, 0, 0)),
            pl.BlockSpec((1, 2, 1, CMP_STRIDE * HEAD_DIM), lambda k, b, g: (k, 0, 0, 0)),
            pl.BlockSpec((1, CMP_STRIDE * HEAD_DIM, 2 * CMP_HIDDEN), lambda k, b, g: (k, 0, 0)),
            pl.BlockSpec((1, 1, CMP_HIDDEN), lambda k, b, g: (k, 0, 0)),
            pl.BlockSpec((1, CMP_HIDDEN, 128), lambda k, b, g: (k, 0, 0)),
        ],
        out_specs=[pl.BlockSpec((1, 1, 1, nh, 128), lambda k, b, g: (k, b, g, 0, 0)),
                   pl.BlockSpec((1, 1, 1, HEAD_DIM, nh), lambda k, b, g: (k, b, g, 0, 0))],
        compiler_params=_cparams(3),
        name="compress",
    )(hkv, pos, w1, b1, w2)


def _attn_kernel(qt_ref, qn_ref, kc_ref, vct_ref, ks_ref, vst_ref, kw_ref, vwt_ref, g_ref, tri_ref, o_ref, p4_ref,
                 imp_ref, oc_ref, bias_ref, biasm_ref, sbuf_a_ref, sbuf_b_ref, sd_ref, *, nh, ns, nst, nq, top_n):
    qi = pl.program_id(2)
    slot = qi & 1
    s0 = qi * Q_TILE
    q64 = qt_ref[0, 0, 0]
    lane_tok = lax.broadcasted_iota(jnp.int32, (1, QCOLS), 1) & (Q_TILE - 1)
    zero_half = jnp.zeros((128 - HEAD_DIM, QCOLS), BF16)
    assert KEY_TILE == 2 * Q_TILE

    def cmp_scores(q64x):
        return jnp.dot(kc_ref[0, 0, 0], jnp.concatenate([q64x, zero_half], axis=0),
                       preferred_element_type=F32)

    def importance(tile, sc, dst):
        t_row = tile * Q_TILE + lane_tok
        cend = lax.broadcasted_iota(jnp.int32, (nh, QCOLS), 0) * CMP_STRIDE + (CMP_BLOCK - 1)
        sm = jnp.where(cend <= t_row, sc, NEG)
        mc = jnp.max(sm, axis=0, keepdims=True)
        pc = jnp.exp2(sm - mc)
        lc = jnp.sum(pc, axis=0, keepdims=True)
        pn = pc * jnp.where(t_row >= CMP_BLOCK - 1, 1.0 / lc, 0.0)
        oc_ref[dst] = jnp.dot(vct_ref[0, 0, 0], pn.astype(BF16), preferred_element_type=F32)
        p4 = pn[:, 0:Q_TILE]
        for h in range(1, HEADS_PER_GROUP):
            p4 = p4 + pn[:, h * Q_TILE:(h + 1) * Q_TILE]
        p4_ref[0:8, :] = jnp.zeros((8, Q_TILE), F32)
        p4_ref[8:8 + nh, :] = p4
        imp = p4_ref[pl.ds(7, ns, stride=4), :]
        for cc in range(4):
            imp = imp + p4_ref[pl.ds(8 + cc, ns, stride=4), :]
        imp_ref[...] = imp

    def pick_blocks(tile, dst):
        imp = imp_ref[...]
        blk = lax.broadcasted_iota(jnp.int32, (ns, Q_TILE), 0)
        tok = tile * Q_TILE + lax.broadcasted_iota(jnp.int32, (ns, Q_TILE), 1)
        cur = lax.shift_right_logical(tok, int(math.log2(SLC_BLOCK)))
        future = blk * SLC_BLOCK > tok
        forced = (blk == 0) | (blk == cur) | (blk == cur - 1)
        val = jnp.where(forced, -2.0, jnp.where(future, -1.0, imp))
        for _ in range(top_n - 3):
            best = jnp.max(val, axis=0, keepdims=True)
            first = jnp.min(jnp.where(val == best, blk, ns), axis=0, keepdims=True)
            val = jnp.where(blk == first, -2.0, val)
        bias = jnp.where((val == -2.0) & jnp.logical_not(future), 0.0, NEG)
        full_blocks = lax.shift_right_logical(tile, 1) * (KEY_TILE // SLC_BLOCK)
        bias_main = jnp.where(blk < full_blocks, bias, NEG)
        for st in range(nst):
            for table, ref in ((bias, bias_ref), (bias_main, biasm_ref)):
                rows = table[st * BF16_ROWS:(st + 1) * BF16_ROWS, :]
                ref[dst, st] = jnp.concatenate([rows] * HEADS_PER_GROUP, axis=1).astype(BF16)

    @pl.when(qi == 0)
    def _():
        importance(0, cmp_scores(q64), 0)
        pick_blocks(0, 0)

    next_tile = jnp.minimum(qi + 1, nq - 1)
    sc_next = cmp_scores(qn_ref[0, 0, 0])

    zero_rows = jnp.zeros((128 - HEAD_DIM - BF16_ROWS, QCOLS), BF16)
    eye = jnp.where(lax.broadcasted_iota(jnp.int32, (Q_TILE, Q_TILE), 0)
                    == lax.broadcasted_iota(jnp.int32, (Q_TILE, Q_TILE), 1), 1.0, 0.0).astype(BF16)
    causal_tri = tri_ref[0]
    window_tri = tri_ref[1]

    def edge_scores(k_tile, rows16, tri):
        q_aug = jnp.concatenate([q64, rows16, zero_rows, tri], axis=0)
        return jnp.dot(jnp.concatenate([k_tile, eye], axis=1), q_aug, preferred_element_type=F32)

    n_back = WINDOW // Q_TILE
    row16 = lax.broadcasted_iota(jnp.int32, (BF16_ROWS, QCOLS), 0)
    back = (qi - row16) & (WIN_POS - 1)
    rows_w = jnp.where((row16 < WIN_POS) & (back <= n_back) & (qi - back < 0), NEG, 0.0).astype(BF16)
    win_tile = lambda d: jnp.where(qi - d < 0, qi - d + nq, qi - d)
    sw = jnp.concatenate(
        [edge_scores(kw_ref[0, 0, win_tile(0)], rows_w, causal_tri),
         jnp.dot(jnp.concatenate([kw_ref[0, 0, win_tile(d)] for d in range(1, n_back)], axis=0),
                 jnp.concatenate([q64, rows_w, zero_rows], axis=0), preferred_element_type=F32),
         edge_scores(kw_ref[0, 0, win_tile(n_back)], rows_w, window_tri)], axis=0)
    vw = jnp.concatenate([vwt_ref[0, 0, win_tile(d)] for d in range(n_back + 1)], axis=1)

    n_full = lax.shift_right_logical(qi, 1)

    def produce_tile(stc, j, dst_ref):
        q_aug = jnp.concatenate([q64, biasm_ref[slot, stc], zero_rows], axis=0)
        s = jnp.dot(ks_ref[0, 0, stc * SEL_UNROLL + j], q_aug, preferred_element_type=F32)
        dst_ref[j * KEY_TILE:(j + 1) * KEY_TILE, :] = s
        return jnp.max(s, axis=0, keepdims=True)

    def produce(st, dst_ref):
        stc = jnp.minimum(st, nst - 1)
        mx = produce_tile(stc, 0, dst_ref)
        for j in range(1, SEL_UNROLL):
            mx = jnp.maximum(mx, produce_tile(stc, j, dst_ref))
        return mx

    def consume_tile(kt, j, cur_ref, m_new):
        p = jnp.exp2(cur_ref[j * KEY_TILE:(j + 1) * KEY_TILE, :] - m_new).astype(BF16)
        return jnp.dot(vst_ref[0, 0, kt], p, preferred_element_type=F32)

    def step(i, carry, cur_ref, next_ref):
        m, acc, mx = carry
        m_new = jnp.maximum(m, mx)
        stc = jnp.minimum(i + 1, nst - 1)
        mx_next, pv = None, None
        for j in range(SEL_UNROLL):
            mxj = produce_tile(stc, j, next_ref)
            mx_next = mxj if mx_next is None else jnp.maximum(mx_next, mxj)
            pvj = consume_tile(i * SEL_UNROLL + j, j, cur_ref, m_new)
            pv = pvj if pv is None else pv + pvj
        return m_new, jnp.exp2(m - m_new) * acc + pv, mx_next

    def sel_body(i, carry):
        return lax.cond((i & 1) == 0,
                        lambda c: step(i, c, sbuf_a_ref, sbuf_b_ref),
                        lambda c: step(i, c, sbuf_b_ref, sbuf_a_ref), carry)

    n_iter = lax.shift_right_logical(n_full + (SEL_UNROLL - 1), int(math.log2(SEL_UNROLL)))
    last = jnp.maximum(n_iter - 1, 0)
    init = (jnp.full((1, QCOLS), NEG, F32), jnp.zeros((V_ROWS, QCOLS), F32), produce(0, sbuf_a_ref))

    odd = (qi & 1) == 1
    rows_d = bias_ref[slot, lax.shift_right_logical(n_full, int(math.log2(SEL_UNROLL)))]
    all_neg = jnp.full((Q_TILE, QCOLS), NEG, BF16)
    tri_a = jnp.where(odd, jnp.zeros((Q_TILE, QCOLS), BF16), causal_tri)
    tri_b = jnp.where(odd, causal_tri, all_neg)
    sd_a = edge_scores(ks_ref[0, 0, n_full, 0:Q_TILE, :], rows_d, tri_a)
    sd_b = edge_scores(ks_ref[0, 0, n_full, Q_TILE:KEY_TILE, :], rows_d, tri_b)
    sd_ref[0:Q_TILE, :] = sd_a
    sd_ref[Q_TILE:KEY_TILE, :] = sd_b
    mx_d = jnp.maximum(jnp.max(sd_a, axis=0, keepdims=True), jnp.max(sd_b, axis=0, keepdims=True))

    importance(next_tile, sc_next, 1 - slot)

    pw = jnp.exp2(sw - jnp.max(sw, axis=0, keepdims=True))
    acc_w = jnp.dot(vw, pw.astype(BF16), preferred_element_type=F32)
    o_w = acc_w[:HEAD_DIM, :] * (1.0 / acc_w[HEAD_DIM:HEAD_DIM + 1, :])

    m_s, acc_s, mx_s = lax.fori_loop(0, last, sel_body, init)

    def finish(cur_ref):
        m_new = jnp.maximum(jnp.maximum(m_s, mx_s), mx_d)
        p_diag = jnp.exp2(sd_ref[...] - m_new).astype(BF16)
        pv = jnp.dot(vst_ref[0, 0, n_full], p_diag, preferred_element_type=F32)
        for j in range(SEL_UNROLL):
            pv = pv + consume_tile(last * SEL_UNROLL + j, j, cur_ref, m_new)
        pick_blocks(next_tile, 1 - slot)
        return jnp.exp2(m_s - m_new) * acc_s + pv

    acc_s = lax.cond((last & 1) == 0, lambda: finish(sbuf_a_ref), lambda: finish(sbuf_b_ref))
    o_s = acc_s[:HEAD_DIM, :] * (1.0 / acc_s[HEAD_DIM:HEAD_DIM + 1, :])

    gts = g_ref[0, 0, 0]
    o_ref[0, 0, 0] = (gts[0:1, :] * oc_ref[slot] + gts[1:2, :] * o_s + gts[2:3, :] * o_w).astype(o_ref.dtype)


def _attn(q_t, kc_nat, vc_tr, ks, vst, kw, vwt, gts, s):
    bsz, G, nq = q_t.shape[:3]
    nh = s // CMP_STRIDE
    ns = s // SLC_BLOCK
    nkt = s // KEY_TILE
    nst = nkt // SEL_UNROLL
    assert s % (SEL_UNROLL * KEY_TILE) == 0
    top_n = min(SLC_TOP_N, ns)
    per_q = lambda blk: pl.BlockSpec(blk, lambda b, g, i: (b, g, i, 0, 0))
    per_bg = lambda blk: pl.BlockSpec(blk, lambda b, g, i: (b, g, 0, 0, 0))
    kk = np.arange(Q_TILE)[:, None]
    tl = np.tile(np.arange(Q_TILE), HEADS_PER_GROUP)[None, :]
    tri = jnp.asarray(np.stack([np.where(kk <= tl, 0.0, NEG), np.where(kk > tl, 0.0, NEG)]), BF16)
    return pl.pallas_call(
        functools.partial(_attn_kernel, nh=nh, ns=ns, nst=nst, nq=nq, top_n=top_n),
        out_shape=jax.ShapeDtypeStruct((bsz, G, nq, HEAD_DIM, QCOLS), BF16),
        grid=(bsz, G, nq),
        in_specs=[
            per_q((1, 1, 1, HEAD_DIM, QCOLS)),
            pl.BlockSpec((1, 1, 1, HEAD_DIM, QCOLS), lambda b, g, i: (b, g, jnp.minimum(i + 1, nq - 1), 0, 0)),
            pl.BlockSpec((1, 1, 1, nh, 128), lambda b, g, i: (0, b, g, 0, 0)),
            pl.BlockSpec((1, 1, 1, HEAD_DIM, nh), lambda b, g, i: (1, b, g, 0, 0)),
            per_bg((1, 1, nkt, KEY_TILE, 128)),
            per_bg((1, 1, nkt, V_ROWS, KEY_TILE)),
            per_bg((1, 1, nq, Q_TILE, 128)),
            per_bg((1, 1, nq, V_ROWS, Q_TILE)),
            per_q((1, 1, 1, 8, QCOLS)),
            pl.BlockSpec((2, Q_TILE, QCOLS), lambda b, g, i: (0, 0, 0)),
        ],
        out_specs=per_q((1, 1, 1, HEAD_DIM, QCOLS)),
        scratch_shapes=[pltpu.VMEM((nh + 8, Q_TILE), F32),
                        pltpu.VMEM((ns, Q_TILE), F32),
                        pltpu.VMEM((2, HEAD_DIM, QCOLS), F32),
                        pltpu.VMEM((2, nst, BF16_ROWS, QCOLS), BF16),
                        pltpu.VMEM((2, nst, BF16_ROWS, QCOLS), BF16),
                        pltpu.VMEM((SEL_UNROLL * KEY_TILE, QCOLS), F32),
                        pltpu.VMEM((SEL_UNROLL * KEY_TILE, QCOLS), F32),
                        pltpu.VMEM((KEY_TILE, QCOLS), F32)],
        compiler_params=_cparams(3),
        name="attn",
    )(q_t, q_t, kc_nat, vc_tr, ks, vst, kw, vwt, gts, tri)


def _final_kernel(x_ref, gs_ref, ot_ref, z_ref, g_ref, wba_ref, wout_ref, fnw_ref, o_ref, *, last):
    halves = TOK_TILE // Q_TILE
    parts = []
    for hf in range(halves):
        rows = []
        for g in range(ATTN_GROUPS):
            blk = ot_ref[0, g, hf].astype(F32)
            for h in range(HEADS_PER_GROUP):
                rows.append(blk[:, h * Q_TILE:(h + 1) * Q_TILE])
        parts.append(jnp.concatenate(rows, axis=0).T)
    o = jnp.concatenate(parts, axis=0)
    y_attn = jnp.dot((o * _silu(z_ref[...])).astype(BF16), wba_ref[...], preferred_element_type=F32)
    merged = gs_ref[...].astype(F32) + jax.nn.sigmoid(g_ref[...]) * y_attn
    xo = x_ref[...] + jnp.dot(merged.astype(BF16), wout_ref[...], preferred_element_type=F32)
    if last:
        ms = jnp.mean(xo * xo, axis=-1, keepdims=True)
        xo = (xo * lax.rsqrt(ms + EPS)) * fnw_ref[...]
    o_ref[...] = xo


def _final(x2, gs, o_t, slab, wba, wout, fnw, bsz, s, last):
    nt = s // TOK_TILE
    halves = TOK_TILE // Q_TILE
    rows = lambda b, i: b * nt + i
    const = lambda shape: pl.BlockSpec(shape, lambda b, i: (0,) * len(shape))
    return pl.pallas_call(
        functools.partial(_final_kernel, last=last),
        out_shape=jax.ShapeDtypeStruct((bsz * s, D_MODEL), F32),
        grid=(bsz, nt),
        in_specs=[
            pl.BlockSpec((TOK_TILE, D_MODEL), lambda b, i: (rows(b, i), 0)),
            pl.BlockSpec((TOK_TILE, D_MODEL), lambda b, i: (rows(b, i), 0)),
            pl.BlockSpec((1, ATTN_GROUPS, halves, HEAD_DIM, QCOLS), lambda b, i: (b, 0, i, 0, 0)),
            pl.BlockSpec((TOK_TILE, ATTN_WIDTH), lambda b, i: (rows(b, i), SLAB_ZATT // ATTN_WIDTH)),
            pl.BlockSpec((TOK_TILE, D_MODEL), lambda b, i: (rows(b, i), SLAB_GATT // D_MODEL)),
            const((ATTN_WIDTH, D_MODEL)), const((D_MODEL, D_MODEL)), const((1, D_MODEL)),
        ],
        out_specs=pl.BlockSpec((TOK_TILE, D_MODEL), lambda b, i: (rows(b, i), 0)),
        compiler_params=_cparams(2),
        name="final",
    )(x2, gs, o_t, slab, slab, wba, wout, fnw)


_W_OFF = np.cumsum([0, SSM_D_INNER, SSM_CONV_DIM, SSM_HEADS, ATTN_WIDTH, 6 * KV_WIDTH, 3 * ATTN_HEADS,
                    ATTN_WIDTH, D_MODEL, D_MODEL]).tolist()


def _wslab_kernel(w_ref, o_ref):
    x = w_ref[...]
    seg = lambda i, j: x[_W_OFF[i]:_W_OFF[j], :]
    pad = jnp.zeros((128 - SSM_HEADS - 3 * ATTN_HEADS, x.shape[1]), x.dtype)
    o_ref[...] = jnp.concatenate(
        [seg(1, 2), seg(0, 1), seg(3, 4), seg(6, 9), seg(4, 5), seg(2, 3), seg(5, 6), pad], axis=0).astype(BF16)


def _slab_weights(w_all, layer):
    w_t = jnp.swapaxes(w_all, 1, 2)
    _, n, d = w_t.shape
    assert all(o % BF16_ROWS == 0 for o in _W_OFF)
    return pl.pallas_call(
        _wslab_kernel,
        out_shape=jax.ShapeDtypeStruct((SLAB_W, d), BF16),
        grid=(d // 128,),
        in_specs=[pl.BlockSpec((None, n, 128), lambda c: (layer, 0, c))],
        out_specs=pl.BlockSpec((SLAB_W, 128), lambda c: (0, c)),
        compiler_params=_cparams(1),
        name="wslab",
    )(w_t)


def _rope_tables(s):
    half = HEAD_DIM // 2
    inv_freq = 1.0 / (ROPE_THETA ** (np.arange(half, dtype=np.float64) * 2.0 / HEAD_DIM))
    ang = np.arange(s, dtype=np.float64)[:, None] * inv_freq[None, :]
    cos, sin = np.cos(ang), np.sin(ang)
    cos_t = np.concatenate([cos, cos] * (128 // HEAD_DIM), axis=1)
    sin_t = np.concatenate([-sin, sin] * (128 // HEAD_DIM), axis=1)
    return jnp.asarray(cos_t, F32), jnp.asarray(sin_t, F32)


def _pad_lanes(v, width=128):
    v = v.reshape(1, -1)
    return jnp.concatenate([v, jnp.zeros((1, width - v.shape[1]), v.dtype)], axis=1)


def kernel(x, norm_w, w_in, conv_w, conv_b, dt_bias, a_log, d_skip, ssm_norm_w, cmp_pos_k, cmp_pos_v, cmp_k_w1,
           cmp_k_b1, cmp_k_w2, cmp_v_w1, cmp_v_b1, cmp_v_w2, w_branch_ssm, w_branch_attn, w_out, final_norm_w):
    bsz, s, _ = x.shape
    depth = norm_w.shape[0]
    cos_t, sin_t = _rope_tables(s)
    x2 = x.reshape(bsz * s, D_MODEL)
    for layer in range(depth):
        slab = _proj(x2, norm_w[layer].reshape(1, -1), _slab_weights(w_in, layer))

        gs = _ssd(slab, bsz, s, conv_w[layer], conv_b[layer].reshape(1, -1), _pad_lanes(dt_bias[layer]),
                  _pad_lanes(a_log[layer]), jnp.repeat(d_skip[layer], SSM_HEAD_DIM).reshape(1, -1),
                  ssm_norm_w[layer].reshape(1, -1), w_branch_ssm[layer].astype(BF16))

        q_t, ks, vst, kw, vwt, hkv, gts = _prep(slab, bsz, s, cos_t, sin_t)

        pos = jnp.stack([cmp_pos_k[layer], cmp_pos_v[layer]]).reshape(2, 2, 1, CMP_STRIDE * HEAD_DIM)
        w1 = jnp.stack([cmp_k_w1[layer], cmp_v_w1[layer]])
        w1 = jnp.concatenate([w1[:, :CMP_STRIDE * HEAD_DIM], w1[:, CMP_STRIDE * HEAD_DIM:]], axis=2).astype(BF16)
        b1 = jnp.stack([cmp_k_b1[layer], cmp_v_b1[layer]]).reshape(2, 1, CMP_HIDDEN)
        w2 = jnp.stack([cmp_k_w2[layer], cmp_v_w2[layer]])
        w2 = jnp.concatenate([w2, jnp.zeros((2, CMP_HIDDEN, 128 - HEAD_DIM), w2.dtype)], axis=2).astype(BF16)
        c_nat, c_tr = _compress(hkv, pos, w1, b1, w2)

        o_t = _attn(q_t, c_nat, c_tr, ks, vst, kw, vwt, gts, s)

        x2 = _final(x2, gs, o_t, slab, w_branch_attn[layer].astype(BF16), w_out[layer].astype(BF16),
                    final_norm_w.reshape(1, -1), bsz, s, layer == depth - 1)
    return x2.reshape(bsz, s, D_MODEL)
```

```python
import functools
import math

import numpy as np
import jax
import jax.numpy as jnp
from jax import lax
from jax.experimental import pallas as pl
from jax.experimental.pallas import tpu as pltpu

F32 = jnp.float32
BF16 = jnp.bfloat16

EPS = 1e-6
ROPE_THETA = 10000.0
NEG = -1e30

D_MODEL = 1024
SSM_HEADS = 16
SSM_HEAD_DIM = 64
SSM_D_INNER = 1024
SSM_GROUPS = 4
SSM_STATE = 128
SSM_CONV = 4
SSM_CHUNK = 256
SSM_CONV_DIM = 2048
ATTN_HEADS = 16
ATTN_GROUPS = 4
HEAD_DIM = 64
ATTN_WIDTH = 1024
KV_WIDTH = 256
CMP_BLOCK = 32
CMP_STRIDE = 16
CMP_HIDDEN = 256
SLC_BLOCK = 64
SLC_TOP_N = 16
WINDOW = 512
Q_TILE = 128
KEY_TILE = 256
TOK_TILE = 256
HEADS_PER_GROUP = ATTN_HEADS // ATTN_GROUPS
QCOLS = HEADS_PER_GROUP * Q_TILE
SEL_UNROLL = 4
GROUPS_PER_STEP = 2
WIN_POS = 8
BF16_ROWS = 16
V_ROWS = HEAD_DIM + BF16_ROWS
LOG2E = math.log2(math.e)

SLAB_XBC = 0
SLAB_ZSSM = 2048
SLAB_Q = 3072
SLAB_ZATT = 4096
SLAB_GSSM = 5120
SLAB_GATT = 6144
SLAB_KV = 7168
SLAB_SMALL = 8704
SLAB_W = 8832
GATE_LANE0 = SSM_HEADS

VMEM_LIMIT = 56 * 1024 * 1024


def _cparams(n_grid):
    return pltpu.CompilerParams(dimension_semantics=("arbitrary",) * n_grid,
                                vmem_limit_bytes=VMEM_LIMIT)


def _silu(x):
    h = 0.5 * x
    return h + h * jnp.tanh(h)


def _proj_kernel(x_ref, nw_ref, w_ref, o_ref):
    x = x_ref[...]
    ms = jnp.mean(x * x, axis=-1, keepdims=True)
    h = (x * lax.rsqrt(ms + EPS)) * nw_ref[...]
    o_ref[...] = lax.dot_general(h.astype(BF16), w_ref[...], (((1,), (1,)), ((), ())),
                                 preferred_element_type=F32)


def _proj(x2, norm_w, w_slab_t):
    t = x2.shape[0]
    tm, tn = 1024, SLAB_W // 3
    return pl.pallas_call(
        _proj_kernel,
        out_shape=jax.ShapeDtypeStruct((t, SLAB_W), F32),
        grid=(SLAB_W // tn, t // tm),
        in_specs=[pl.BlockSpec((tm, D_MODEL), lambda j, i: (i, 0)),
                  pl.BlockSpec((1, D_MODEL), lambda j, i: (0, 0)),
                  pl.BlockSpec((tn, D_MODEL), lambda j, i: (j, 0))],
        out_specs=pl.BlockSpec((tm, tn), lambda j, i: (i, j)),
        compiler_params=_cparams(2),
        name="proj",
    )(x2, norm_w, w_slab_t)


def _split3(v):
    v1 = v.astype(BF16)
    r1 = v - v1.astype(F32)
    v2 = r1.astype(BF16)
    v3 = (r1 - v2.astype(F32)).astype(BF16)
    return v1, v2, v3


def _dot3(parts, rhs, lhs_side=True):
    out = None
    for p in parts:
        d = (jnp.dot(p, rhs, preferred_element_type=F32) if lhs_side
             else jnp.dot(rhs, p, preferred_element_type=F32))
        out = d if out is None else out + d
    return out


def _ssd_kernel(xbc_ref, prev_ref, small_ref, z_ref, g_ref, convw_ref, convb_ref, dtb_ref, alog_ref,
                dskip_ref, nw_ref, wb_ref, tri_ref, exp_ref, o_ref, st_ref, ext_ref):
    c = pl.program_id(1)
    L = SSM_CHUNK

    @pl.when(c == 0)
    def _():
        st_ref[...] = jnp.zeros_like(st_ref)

    prev = jnp.where(c > 0, prev_ref[...], 0.0)
    cols = []
    for j in range(SSM_CONV_DIM // 128):
        lanes = slice(j * 128, (j + 1) * 128)
        ext_ref[j, 0:8, :] = prev[:, lanes]
        ext_ref[j, 8:8 + L, :] = xbc_ref[:, lanes]
        acc = jnp.broadcast_to(convb_ref[:, lanes], (L, 128))
        for k in range(SSM_CONV):
            r0 = 8 - (SSM_CONV - 1) + k
            acc = acc + convw_ref[k:k + 1, lanes] * ext_ref[j, r0:r0 + L, :]
        cols.append(_silu(acc))
    xbc = jnp.concatenate(cols, axis=1)
    xs = xbc[:, :SSM_D_INNER]
    bm = xbc[:, SSM_D_INNER:SSM_D_INNER + SSM_GROUPS * SSM_STATE]
    cm = xbc[:, SSM_D_INNER + SSM_GROUPS * SSM_STATE:]

    lane = lax.broadcasted_iota(jnp.int32, (L, 128), 1)
    head_lane = lane < SSM_HEADS
    u = small_ref[...] + dtb_ref[...]
    dt = jnp.maximum(u, 0.0) + jnp.log1p(jnp.exp(-jnp.abs(u)))
    dt = jnp.where(head_lane, dt, 0.0)
    a = dt * -jnp.exp(alog_ref[...])
    cs = _dot3(_split3(a), tri_ref[...], lhs_side=False)
    cs_last = cs[L - 1:L, :]
    ecs = jnp.exp(cs)
    ds = jnp.exp(cs_last - cs)
    packed = (dt + pltpu.roll(jnp.where(head_lane, ecs, 0.0), SSM_HEADS, 1)
              + pltpu.roll(jnp.where(head_lane, ds, 0.0), 2 * SSM_HEADS, 1))
    expanded = _dot3(_split3(packed)[:2], exp_ref[...])
    dt_e = expanded[:, :SSM_D_INNER]
    ecs_e = expanded[:, SSM_D_INNER:2 * SSM_D_INNER]
    ds_e = expanded[:, 2 * SSM_D_INNER:]
    xdt = xs * dt_e
    xds = xdt * ds_e
    cs_t = cs.T

    row = lax.broadcasted_iota(jnp.int32, (L, L), 0)
    col = lax.broadcasted_iota(jnp.int32, (L, L), 1)
    tril = row >= col
    half = lax.broadcasted_iota(jnp.int32, (L, 128), 1) < SSM_HEAD_DIM
    gw = SSM_D_INNER // SSM_GROUPS

    y_groups = []
    for g in range(SSM_GROUPS):
        bg = bm[:, g * SSM_STATE:(g + 1) * SSM_STATE]
        cg = cm[:, g * SSM_STATE:(g + 1) * SSM_STATE].astype(BF16)
        bg_t = bg.T.astype(BF16)
        cb = jnp.dot(cg, bg_t, preferred_element_type=F32)
        st = st_ref[g]
        y_off = jnp.dot(cg, st.astype(BF16), preferred_element_type=F32) * ecs_e[:, g * gw:(g + 1) * gw]
        st_ref[g] = (ecs_e[L - 1:L, g * gw:(g + 1) * gw] * st
                     + jnp.dot(bg_t, xds[:, g * gw:(g + 1) * gw].astype(BF16), preferred_element_type=F32))
        pairs = []
        for pr in range(2):
            rhs = xdt[:, g * gw + pr * 128:g * gw + (pr + 1) * 128].astype(BF16)
            ys = []
            for hh in range(2):
                h = g * 4 + pr * 2 + hh
                seg = cs[:, h:h + 1] - cs_t[h:h + 1, :]
                dec = jnp.exp(jnp.where(tril, seg, NEG))
                ys.append(jnp.dot((cb * dec).astype(BF16), rhs, preferred_element_type=F32))
            pairs.append(jnp.where(half, ys[0], ys[1]))
        y_groups.append(jnp.concatenate(pairs, axis=1) + y_off)
    y = jnp.concatenate(y_groups, axis=1) + dskip_ref[...] * xs

    v = y * _silu(z_ref[...])
    ms = jnp.mean(v * v, axis=-1, keepdims=True)
    vn = (v * lax.rsqrt(ms + EPS)) * nw_ref[...]
    ys = jnp.dot(vn.astype(BF16), wb_ref[...], preferred_element_type=F32)
    o_ref[...] = (jax.nn.sigmoid(g_ref[...]) * ys).astype(o_ref.dtype)


def _ssd(slab, bsz, s, conv_w, conv_b, dtb, alog, dskip_e, nw, wb):
    nt = s // SSM_CHUNK
    L = SSM_CHUNK
    tri = jnp.asarray(np.tril(np.ones((L, L), np.float32)), BF16)
    e = np.zeros((128, 3 * SSM_D_INNER), np.float32)
    for k in range(3):
        for h in range(SSM_HEADS):
            e[k * SSM_HEADS + h, k * SSM_D_INNER + h * SSM_HEAD_DIM:k * SSM_D_INNER + (h + 1) * SSM_HEAD_DIM] = 1.0
    e = jnp.asarray(e, BF16)
    rows = lambda b, c: b * nt + c
    const = lambda shape: pl.BlockSpec(shape, lambda b, c: (0,) * len(shape))
    return pl.pallas_call(
        _ssd_kernel,
        out_shape=jax.ShapeDtypeStruct((bsz * s, SSM_D_INNER), BF16),
        grid=(bsz, nt),
        in_specs=[
            pl.BlockSpec((L, SSM_CONV_DIM), lambda b, c: (rows(b, c), SLAB_XBC // SSM_CONV_DIM)),
            pl.BlockSpec((8, SSM_CONV_DIM), lambda b, c: (jnp.maximum(rows(b, c) * (L // 8) - 1, 0), 0)),
            pl.BlockSpec((L, 128), lambda b, c: (rows(b, c), SLAB_SMALL // 128)),
            pl.BlockSpec((L, SSM_D_INNER), lambda b, c: (rows(b, c), SLAB_ZSSM // SSM_D_INNER)),
            pl.BlockSpec((L, D_MODEL), lambda b, c: (rows(b, c), SLAB_GSSM // D_MODEL)),
            const((SSM_CONV, SSM_CONV_DIM)), const((1, SSM_CONV_DIM)), const((1, 128)), const((1, 128)),
            const((1, SSM_D_INNER)), const((1, SSM_D_INNER)), const((SSM_D_INNER, D_MODEL)),
            const((L, L)), const((128, 3 * SSM_D_INNER)),
        ],
        out_specs=pl.BlockSpec((L, D_MODEL), lambda b, c: (rows(b, c), 0)),
        scratch_shapes=[pltpu.VMEM((SSM_GROUPS, SSM_STATE, SSM_D_INNER // SSM_GROUPS), F32),
                        pltpu.VMEM((SSM_CONV_DIM // 128, L + 8, 128), F32)],
        compiler_params=_cparams(2),
        name="ssd",
    )(slab, slab, slab, slab, slab, conv_w, conv_b, dtb, alog, dskip_e, nw, wb, tri, e)


def _prep_kernel(q_ref, kc_ref, vc_ref, ks_ref, vs_ref, kw_ref, vw_ref, small_ref, cos_ref, sin_ref,
                 qt_ref, kso_ref, vst_ref, kwo_ref, vwt_ref, hkv_ref, gt_ref, stage_ref):
    cos = cos_ref[...]
    sin = sin_ref[...]

    def rope(x):
        w = x.shape[1]
        cf = jnp.concatenate([cos] * (w // 128), axis=1)
        sf = jnp.concatenate([sin] * (w // 128), axis=1)
        lane = lax.broadcasted_iota(jnp.int32, x.shape, 1)
        first = (lane % HEAD_DIM) < (HEAD_DIM // 2)
        swapped = jnp.where(first, pltpu.roll(x, w - HEAD_DIM // 2, 1), pltpu.roll(x, HEAD_DIM // 2, 1))
        return x * cf + swapped * sf

    halves = TOK_TILE // Q_TILE
    lane128 = lax.broadcasted_iota(jnp.int32, (TOK_TILE, 128), 1)
    row128 = lax.broadcasted_iota(jnp.int32, (TOK_TILE, 128), 0)
    low64 = lane128 < HEAD_DIM
    blocks_per_tile = KEY_TILE // SLC_BLOCK
    tile_in_super = pl.program_id(1) & (SEL_UNROLL - 1)
    block_in_super = tile_in_super * blocks_per_tile + lax.shift_right_logical(row128, int(math.log2(SLC_BLOCK)))
    block_onehot = jnp.where(lane128 - HEAD_DIM == block_in_super, 1.0, 0.0)
    win_pos = ((pl.program_id(1) * halves + lax.shift_right_logical(row128, int(math.log2(Q_TILE))))
               & (WIN_POS - 1))
    win_onehot = jnp.where(lane128 - HEAD_DIM == win_pos, 1.0, 0.0)
    assert TOK_TILE == KEY_TILE and SEL_UNROLL * blocks_per_tile == BF16_ROWS

    def group_cols(x, g, rest):
        r = x if g == 0 else pltpu.roll(x, KV_WIDTH - HEAD_DIM * g, 1)
        return jnp.where(low64, r[:, :128], rest)

    def values_t(v_t, g, lo, hi):
        n = hi - lo
        extra = jnp.where(lax.broadcasted_iota(jnp.int32, (V_ROWS - HEAD_DIM, n), 0) == 0, 1.0, 0.0)
        return jnp.concatenate([v_t[g * HEAD_DIM:(g + 1) * HEAD_DIM, lo:hi], extra], axis=0).astype(BF16)

    qr = rope(q_ref[...]) * (HEAD_DIM ** -0.5 * LOG2E)
    for hf in range(halves):
        q_t = qr[hf * Q_TILE:(hf + 1) * Q_TILE, :].T
        for g in range(ATTN_GROUPS):
            base = g * HEADS_PER_GROUP * HEAD_DIM
            blk = jnp.concatenate([q_t[base + h * HEAD_DIM:base + (h + 1) * HEAD_DIM, :]
                                   for h in range(HEADS_PER_GROUP)], axis=1)
            qt_ref[0, g, hf] = blk.astype(BF16)

    ksr = rope(ks_ref[...])
    vs_t = vs_ref[...].T
    for g in range(ATTN_GROUPS):
        kso_ref[0, g, 0] = group_cols(ksr, g, block_onehot).astype(BF16)
        vst_ref[0, g, 0] = values_t(vs_t, g, 0, TOK_TILE)

    kwr = rope(kw_ref[...])
    vw_t = vw_ref[...].T
    for g in range(ATTN_GROUPS):
        kg = group_cols(kwr, g, win_onehot).astype(BF16)
        for hf in range(halves):
            kwo_ref[0, g, hf] = kg[hf * Q_TILE:(hf + 1) * Q_TILE, :]
            vwt_ref[0, g, hf] = values_t(vw_t, g, hf * Q_TILE, (hf + 1) * Q_TILE)

    per_half = CMP_STRIDE
    n_half = TOK_TILE // per_half
    lane_h = lax.broadcasted_iota(jnp.int32, (n_half, 128), 1) < HEAD_DIM
    for kv, x in enumerate((rope(kc_ref[...]), vc_ref[...])):
        for c in range(KV_WIDTH // 128):
            stage_ref[kv, c] = x[:, c * 128:(c + 1) * 128]
        for c in range(KV_WIDTH // 128):
            pieces = [stage_ref[kv, c, pl.ds(l, n_half, stride=per_half), :] for l in range(per_half)]
            swapped = [pltpu.roll(p, HEAD_DIM, 1) for p in pieces]
            for hi in range(128 // HEAD_DIM):
                tiles = []
                for l in range(0, per_half, 2):
                    if hi == 0:
                        tiles.append(jnp.where(lane_h, pieces[l], swapped[l + 1]))
                    else:
                        tiles.append(jnp.where(lane_h, swapped[l], pieces[l + 1]))
                hkv_ref[kv, 0, c * (128 // HEAD_DIM) + hi] = jnp.concatenate(tiles, axis=1)

    gates_t = jax.nn.sigmoid(small_ref[...]).T
    zeros_g = jnp.zeros((8 - 3, QCOLS), F32)
    for hf in range(halves):
        for g in range(ATTN_GROUPS):
            rows_c = []
            for cc in range(3):
                rows_c.append(jnp.concatenate(
                    [gates_t[GATE_LANE0 + (g * HEADS_PER_GROUP + r) * 3 + cc:GATE_LANE0 + (g * HEADS_PER_GROUP + r) * 3 + cc + 1,
                             hf * Q_TILE:(hf + 1) * Q_TILE] for r in range(HEADS_PER_GROUP)], axis=1))
            gt_ref[0, g, hf] = jnp.concatenate(rows_c + [zeros_g], axis=0)


def _prep(slab, bsz, s, cos_t, sin_t):
    nt = s // TOK_TILE
    nq = s // Q_TILE
    halves = TOK_TILE // Q_TILE
    G = ATTN_GROUPS
    rows = lambda b, i: b * nt + i
    kvspec = lambda k: pl.BlockSpec((TOK_TILE, KV_WIDTH), lambda b, i: (rows(b, i), SLAB_KV // KV_WIDTH + k))
    out5 = lambda blk: pl.BlockSpec(blk, lambda b, i: (b, 0, i, 0, 0))
    return pl.pallas_call(
        _prep_kernel,
        out_shape=[
            jax.ShapeDtypeStruct((bsz, G, nq, HEAD_DIM, QCOLS), BF16),
            jax.ShapeDtypeStruct((bsz, G, nt, KEY_TILE, 128), BF16),
            jax.ShapeDtypeStruct((bsz, G, nt, V_ROWS, KEY_TILE), BF16),
            jax.ShapeDtypeStruct((bsz, G, nq, Q_TILE, 128), BF16),
            jax.ShapeDtypeStruct((bsz, G, nq, V_ROWS, Q_TILE), BF16),
            jax.ShapeDtypeStruct((2, bsz, G, s // CMP_STRIDE, CMP_STRIDE * HEAD_DIM), F32),
            jax.ShapeDtypeStruct((bsz, G, nq, 8, QCOLS), F32),
        ],
        grid=(bsz, nt),
        in_specs=[
            pl.BlockSpec((TOK_TILE, ATTN_WIDTH), lambda b, i: (rows(b, i), SLAB_Q // ATTN_WIDTH)),
            kvspec(0), kvspec(1), kvspec(2), kvspec(3), kvspec(4), kvspec(5),
            pl.BlockSpec((TOK_TILE, 128), lambda b, i: (rows(b, i), SLAB_SMALL // 128)),
            pl.BlockSpec((TOK_TILE, 128), lambda b, i: (i, 0)),
            pl.BlockSpec((TOK_TILE, 128), lambda b, i: (i, 0)),
        ],
        out_specs=[
            out5((1, G, halves, HEAD_DIM, QCOLS)),
            out5((1, G, 1, KEY_TILE, 128)),
            out5((1, G, 1, V_ROWS, KEY_TILE)),
            out5((1, G, halves, Q_TILE, 128)),
            out5((1, G, halves, V_ROWS, Q_TILE)),
            pl.BlockSpec((2, 1, G, TOK_TILE // CMP_STRIDE, CMP_STRIDE * HEAD_DIM), lambda b, i: (0, b, 0, i, 0)),
            out5((1, G, halves, 8, QCOLS)),
        ],
        scratch_shapes=[pltpu.VMEM((2, KV_WIDTH // 128, TOK_TILE, 128), F32)],
        compiler_params=_cparams(2),
        name="prep",
    )(slab, slab, slab, slab, slab, slab, slab, slab, cos_t, sin_t)


def _compress_kernel(h_ref, pos_ref, w1_ref, b1_ref, w2_ref, nat_ref, tr_ref):
    h = h_ref[0, 0, 0]
    nh = h.shape[0]
    w1 = w1_ref[0]
    a = jnp.dot((h + pos_ref[0, 0]).astype(BF16), w1[:, :CMP_HIDDEN], preferred_element_type=F32)
    b = jnp.dot((h + pos_ref[0, 1]).astype(BF16), w1[:, CMP_HIDDEN:], preferred_element_type=F32)
    hid = a + pltpu.roll(b, nh - 1, 0) + b1_ref[0]
    out = jnp.dot(jax.nn.gelu(hid).astype(BF16), w2_ref[0], preferred_element_type=F32)
    nat_ref[0, 0, 0] = out.astype(BF16)
    tr_ref[0, 0, 0] = out.T[:HEAD_DIM, :].astype(BF16)


def _compress(hkv, pos, w1, b1, w2):
    _, bsz, G, nh, _ = hkv.shape
    return pl.pallas_call(
        _compress_kernel,
        out_shape=[jax.ShapeDtypeStruct((2, bsz, G, nh, 128), BF16),
                   jax.ShapeDtypeStruct((2, bsz, G, HEAD_DIM, nh), BF16)],
        grid=(2, bsz, G),
        in_specs=[
            pl.BlockSpec((1, 1, 1, nh, CMP_STRIDE * HEAD_DIM), lambda k, b, g: (k, b, g, 0, 0)),
            pl.BlockSpec((1, 2, 1, CMP_STRIDE * HEAD_DIM), lambda k, b, g: (k, 0, 0, 0)),
            pl.BlockSpec((1, CMP_STRIDE * HEAD_DIM, 2 * CMP_HIDDEN), lambda k, b, g: (k, 0, 0)),
            pl.BlockSpec((1, 1, CMP_HIDDEN), lambda k, b, g: (k, 0, 0)),
            pl.BlockSpec((1, CMP_HIDDEN, 128), lambda k, b, g: (k, 0, 0)),
        ],
        out_specs=[pl.BlockSpec((1, 1, 1, nh, 128), lambda k, b, g: (k, b, g, 0, 0)),
                   pl.BlockSpec((1, 1, 1, HEAD_DIM, nh), lambda k, b, g: (k, b, g, 0, 0))],
        compiler_params=_cparams(3),
        name="compress",
    )(hkv, pos, w1, b1, w2)


def _attn_kernel(qt_ref, qn_ref, kc_ref, vct_ref, ks_ref, vst_ref, kw_ref, vwt_ref, g_ref, tri_ref, o_ref, p4_ref,
                 imp_ref, oc_ref, bias_ref, biasm_ref, sbuf_a_ref, sbuf_b_ref, sd_ref, **static):
    for gg in range(GROUPS_PER_STEP):
        one = lambda ref, axis: ref.at[(slice(None),) * axis + (slice(gg, gg + 1),)]
        _attn_group(one(qt_ref, 1), one(qn_ref, 1), one(kc_ref, 2), one(vct_ref, 2), one(ks_ref, 1), one(vst_ref, 1),
                    one(kw_ref, 1), one(vwt_ref, 1), one(g_ref, 1), tri_ref, one(o_ref, 1), p4_ref, imp_ref,
                    oc_ref.at[gg], bias_ref.at[gg], biasm_ref.at[gg], sbuf_a_ref, sbuf_b_ref, sd_ref, **static)


def _attn_group(qt_ref, qn_ref, kc_ref, vct_ref, ks_ref, vst_ref, kw_ref, vwt_ref, g_ref, tri_ref, o_ref, p4_ref,
                imp_ref, oc_ref, bias_ref, biasm_ref, sbuf_a_ref, sbuf_b_ref, sd_ref, *, nh, ns, nst, nq, top_n):
    qi = pl.program_id(2)
    slot = qi & 1
    s0 = qi * Q_TILE
    q64 = qt_ref[0, 0, 0]
    lane_tok = lax.broadcasted_iota(jnp.int32, (1, QCOLS), 1) & (Q_TILE - 1)
    zero_half = jnp.zeros((128 - HEAD_DIM, QCOLS), BF16)
    assert KEY_TILE == 2 * Q_TILE

    def cmp_scores(q64x):
        return jnp.dot(kc_ref[0, 0, 0], jnp.concatenate([q64x, zero_half], axis=0),
                       preferred_element_type=F32)

    def importance(tile, sc, dst):
        t_row = tile * Q_TILE + lane_tok
        cend = lax.broadcasted_iota(jnp.int32, (nh, QCOLS), 0) * CMP_STRIDE + (CMP_BLOCK - 1)
        sm = jnp.where(cend <= t_row, sc, NEG)
        mc = jnp.max(sm, axis=0, keepdims=True)
        pc = jnp.exp2(sm - mc)
        lc = jnp.sum(pc, axis=0, keepdims=True)
        pn = pc * jnp.where(t_row >= CMP_BLOCK - 1, 1.0 / lc, 0.0)
        oc_ref[dst] = jnp.dot(vct_ref[0, 0, 0], pn.astype(BF16), preferred_element_type=F32)
        p4 = pn[:, 0:Q_TILE]
        for h in range(1, HEADS_PER_GROUP):
            p4 = p4 + pn[:, h * Q_TILE:(h + 1) * Q_TILE]
        p4_ref[0:8, :] = jnp.zeros((8, Q_TILE), F32)
        p4_ref[8:8 + nh, :] = p4
        imp = p4_ref[pl.ds(7, ns, stride=4), :]
        for cc in range(4):
            imp = imp + p4_ref[pl.ds(8 + cc, ns, stride=4), :]
        imp_ref[...] = imp

    def pick_blocks(tile, dst):
        imp = imp_ref[...]
        blk = lax.broadcasted_iota(jnp.int32, (ns, Q_TILE), 0)
        tok = tile * Q_TILE + lax.broadcasted_iota(jnp.int32, (ns, Q_TILE), 1)
        cur = lax.shift_right_logical(tok, int(math.log2(SLC_BLOCK)))
        future = blk * SLC_BLOCK > tok
        forced = (blk == 0) | (blk == cur) | (blk == cur - 1)
        val = jnp.where(forced, -2.0, jnp.where(future, -1.0, imp))
        for _ in range(top_n - 3):
            best = jnp.max(val, axis=0, keepdims=True)
            first = jnp.min(jnp.where(val == best, blk, ns), axis=0, keepdims=True)
            val = jnp.where(blk == first, -2.0, val)
        bias = jnp.where((val == -2.0) & jnp.logical_not(future), 0.0, NEG)
        full_blocks = lax.shift_right_logical(tile, 1) * (KEY_TILE // SLC_BLOCK)
        bias_main = jnp.where(blk < full_blocks, bias, NEG)
        for st in range(nst):
            for table, ref in ((bias, bias_ref), (bias_main, biasm_ref)):
                rows = table[st * BF16_ROWS:(st + 1) * BF16_ROWS, :]
                ref[dst, st] = jnp.concatenate([rows] * HEADS_PER_GROUP, axis=1).astype(BF16)

    @pl.when(qi == 0)
    def _():
        importance(0, cmp_scores(q64), 0)
        pick_blocks(0, 0)

    next_tile = jnp.minimum(qi + 1, nq - 1)
    sc_next = cmp_scores(qn_ref[0, 0, 0])

    zero_rows = jnp.zeros((128 - HEAD_DIM - BF16_ROWS, QCOLS), BF16)
    eye = jnp.where(lax.broadcasted_iota(jnp.int32, (Q_TILE, Q_TILE), 0)
                    == lax.broadcasted_iota(jnp.int32, (Q_TILE, Q_TILE), 1), 1.0, 0.0).astype(BF16)
    causal_tri = tri_ref[0]
    window_tri = tri_ref[1]

    def edge_scores(k_tile, rows16, tri):
        q_aug = jnp.concatenate([q64, rows16, zero_rows, tri], axis=0)
        return jnp.dot(jnp.concatenate([k_tile, eye], axis=1), q_aug, preferred_element_type=F32)

    n_back = WINDOW // Q_TILE
    row16 = lax.broadcasted_iota(jnp.int32, (BF16_ROWS, QCOLS), 0)
    back = (qi - row16) & (WIN_POS - 1)
    rows_w = jnp.where((row16 < WIN_POS) & (back <= n_back) & (qi - back < 0), NEG, 0.0).astype(BF16)
    win_tile = lambda d: jnp.where(qi - d < 0, qi - d + nq, qi - d)
    sw = jnp.concatenate(
        [edge_scores(kw_ref[0, 0, win_tile(0)], rows_w, causal_tri),
         jnp.dot(jnp.concatenate([kw_ref[0, 0, win_tile(d)] for d in range(1, n_back)], axis=0),
                 jnp.concatenate([q64, rows_w, zero_rows], axis=0), preferred_element_type=F32),
         edge_scores(kw_ref[0, 0, win_tile(n_back)], rows_w, window_tri)], axis=0)
    vw = jnp.concatenate([vwt_ref[0, 0, win_tile(d)] for d in range(n_back + 1)], axis=1)

    n_full = lax.shift_right_logical(qi, 1)

    def produce_tile(stc, j, dst_ref):
        q_aug = jnp.concatenate([q64, biasm_ref[slot, stc], zero_rows], axis=0)
        s = jnp.dot(ks_ref[0, 0, stc * SEL_UNROLL + j], q_aug, preferred_element_type=F32)
        dst_ref[j * KEY_TILE:(j + 1) * KEY_TILE, :] = s
        return jnp.max(s, axis=0, keepdims=True)

    def produce(st, dst_ref):
        stc = jnp.minimum(st, nst - 1)
        mx = produce_tile(stc, 0, dst_ref)
        for j in range(1, SEL_UNROLL):
            mx = jnp.maximum(mx, produce_tile(stc, j, dst_ref))
        return mx

    def consume_tile(kt, j, cur_ref, m_new):
        p = jnp.exp2(cur_ref[j * KEY_TILE:(j + 1) * KEY_TILE, :] - m_new).astype(BF16)
        return jnp.dot(vst_ref[0, 0, kt], p, preferred_element_type=F32)

    def step(i, carry, cur_ref, next_ref):
        m, acc, mx = carry
        m_new = jnp.maximum(m, mx)
        stc = jnp.minimum(i + 1, nst - 1)
        mx_next, pv = None, None
        for j in range(SEL_UNROLL):
            mxj = produce_tile(stc, j, next_ref)
            mx_next = mxj if mx_next is None else jnp.maximum(mx_next, mxj)
            pvj = consume_tile(i * SEL_UNROLL + j, j, cur_ref, m_new)
            pv = pvj if pv is None else pv + pvj
        return m_new, jnp.exp2(m - m_new) * acc + pv, mx_next

    def sel_body(i, carry):
        return lax.cond((i & 1) == 0,
                        lambda c: step(i, c, sbuf_a_ref, sbuf_b_ref),
                        lambda c: step(i, c, sbuf_b_ref, sbuf_a_ref), carry)

    n_iter = lax.shift_right_logical(n_full + (SEL_UNROLL - 1), int(math.log2(SEL_UNROLL)))
    last = jnp.maximum(n_iter - 1, 0)
    init = (jnp.full((1, QCOLS), NEG, F32), jnp.zeros((V_ROWS, QCOLS), F32), produce(0, sbuf_a_ref))

    odd = (qi & 1) == 1
    rows_d = bias_ref[slot, lax.shift_right_logical(n_full, int(math.log2(SEL_UNROLL)))]
    all_neg = jnp.full((Q_TILE, QCOLS), NEG, BF16)
    tri_a = jnp.where(odd, jnp.zeros((Q_TILE, QCOLS), BF16), causal_tri)
    tri_b = jnp.where(odd, causal_tri, all_neg)
    sd_a = edge_scores(ks_ref[0, 0, n_full, 0:Q_TILE, :], rows_d, tri_a)
    sd_b = edge_scores(ks_ref[0, 0, n_full, Q_TILE:KEY_TILE, :], rows_d, tri_b)
    sd_ref[0:Q_TILE, :] = sd_a
    sd_ref[Q_TILE:KEY_TILE, :] = sd_b
    mx_d = jnp.maximum(jnp.max(sd_a, axis=0, keepdims=True), jnp.max(sd_b, axis=0, keepdims=True))

    importance(next_tile, sc_next, 1 - slot)

    pw = jnp.exp2(sw - jnp.max(sw, axis=0, keepdims=True))
    acc_w = jnp.dot(vw, pw.astype(BF16), preferred_element_type=F32)
    o_w = acc_w[:HEAD_DIM, :] * (1.0 / acc_w[HEAD_DIM:HEAD_DIM + 1, :])

    m_s, acc_s, mx_s = lax.fori_loop(0, last, sel_body, init)

    def finish(cur_ref):
        m_new = jnp.maximum(jnp.maximum(m_s, mx_s), mx_d)
        p_diag = jnp.exp2(sd_ref[...] - m_new).astype(BF16)
        pv = jnp.dot(vst_ref[0, 0, n_full], p_diag, preferred_element_type=F32)
        for j in range(SEL_UNROLL):
            pv = pv + consume_tile(last * SEL_UNROLL + j, j, cur_ref, m_new)
        pick_blocks(next_tile, 1 - slot)
        return jnp.exp2(m_s - m_new) * acc_s + pv

    acc_s = lax.cond((last & 1) == 0, lambda: finish(sbuf_a_ref), lambda: finish(sbuf_b_ref))
    o_s = acc_s[:HEAD_DIM, :] * (1.0 / acc_s[HEAD_DIM:HEAD_DIM + 1, :])

    gts = g_ref[0, 0, 0]
    o_ref[0, 0, 0] = (gts[0:1, :] * oc_ref[slot] + gts[1:2, :] * o_s + gts[2:3, :] * o_w).astype(o_ref.dtype)


def _attn(q_t, kc_nat, vc_tr, ks, vst, kw, vwt, gts, s):
    bsz, G, nq = q_t.shape[:3]
    nh = s // CMP_STRIDE
    ns = s // SLC_BLOCK
    nkt = s // KEY_TILE
    nst = nkt // SEL_UNROLL
    assert s % (SEL_UNROLL * KEY_TILE) == 0
    top_n = min(SLC_TOP_N, ns)
    gps = GROUPS_PER_STEP
    assert G % gps == 0
    per_q = lambda blk: pl.BlockSpec(blk, lambda b, g, i: (b, g, i, 0, 0))
    per_bg = lambda blk: pl.BlockSpec(blk, lambda b, g, i: (b, g, 0, 0, 0))
    kk = np.arange(Q_TILE)[:, None]
    tl = np.tile(np.arange(Q_TILE), HEADS_PER_GROUP)[None, :]
    tri = jnp.asarray(np.stack([np.where(kk <= tl, 0.0, NEG), np.where(kk > tl, 0.0, NEG)]), BF16)
    return pl.pallas_call(
        functools.partial(_attn_kernel, nh=nh, ns=ns, nst=nst, nq=nq, top_n=top_n),
        out_shape=jax.ShapeDtypeStruct((bsz, G, nq, HEAD_DIM, QCOLS), BF16),
        grid=(bsz, G // gps, nq),
        in_specs=[
            per_q((1, gps, 1, HEAD_DIM, QCOLS)),
            pl.BlockSpec((1, gps, 1, HEAD_DIM, QCOLS), lambda b, g, i: (b, g, jnp.minimum(i + 1, nq - 1), 0, 0)),
            pl.BlockSpec((1, 1, gps, nh, 128), lambda b, g, i: (0, b, g, 0, 0)),
            pl.BlockSpec((1, 1, gps, HEAD_DIM, nh), lambda b, g, i: (1, b, g, 0, 0)),
            per_bg((1, gps, nkt, KEY_TILE, 128)),
            per_bg((1, gps, nkt, V_ROWS, KEY_TILE)),
            per_bg((1, gps, nq, Q_TILE, 128)),
            per_bg((1, gps, nq, V_ROWS, Q_TILE)),
            per_q((1, gps, 1, 8, QCOLS)),
            pl.BlockSpec((2, Q_TILE, QCOLS), lambda b, g, i: (0, 0, 0)),
        ],
        out_specs=per_q((1, gps, 1, HEAD_DIM, QCOLS)),
        scratch_shapes=[pltpu.VMEM((nh + 8, Q_TILE), F32),
                        pltpu.VMEM((ns, Q_TILE), F32),
                        pltpu.VMEM((gps, 2, HEAD_DIM, QCOLS), F32),
                        pltpu.VMEM((gps, 2, nst, BF16_ROWS, QCOLS), BF16),
                        pltpu.VMEM((gps, 2, nst, BF16_ROWS, QCOLS), BF16),
                        pltpu.VMEM((SEL_UNROLL * KEY_TILE, QCOLS), F32),
                        pltpu.VMEM((SEL_UNROLL * KEY_TILE, QCOLS), F32),
                        pltpu.VMEM((KEY_TILE, QCOLS), F32)],
        compiler_params=_cparams(3),
        name="attn",
    )(q_t, q_t, kc_nat, vc_tr, ks, vst, kw, vwt, gts, tri)


def _final_kernel(x_ref, gs_ref, ot_ref, z_ref, g_ref, wba_ref, wout_ref, fnw_ref, o_ref, *, last):
    halves = TOK_TILE // Q_TILE
    parts = []
    for hf in range(halves):
        rows = []
        for g in range(ATTN_GROUPS):
            blk = ot_ref[0, g, hf].astype(F32)
            for h in range(HEADS_PER_GROUP):
                rows.append(blk[:, h * Q_TILE:(h + 1) * Q_TILE])
        parts.append(jnp.concatenate(rows, axis=0).T)
    o = jnp.concatenate(parts, axis=0)
    y_attn = jnp.dot((o * _silu(z_ref[...])).astype(BF16), wba_ref[...], preferred_element_type=F32)
    merged = gs_ref[...].astype(F32) + jax.nn.sigmoid(g_ref[...]) * y_attn
    xo = x_ref[...] + jnp.dot(merged.astype(BF16), wout_ref[...], preferred_element_type=F32)
    if last:
        ms = jnp.mean(xo * xo, axis=-1, keepdims=True)
        xo = (xo * lax.rsqrt(ms + EPS)) * fnw_ref[...]
    o_ref[...] = xo


def _final(x2, gs, o_t, slab, wba, wout, fnw, bsz, s, last):
    nt = s // TOK_TILE
    halves = TOK_TILE // Q_TILE
    rows = lambda b, i: b * nt + i
    const = lambda shape: pl.BlockSpec(shape, lambda b, i: (0,) * len(shape))
    return pl.pallas_call(
        functools.partial(_final_kernel, last=last),
        out_shape=jax.ShapeDtypeStruct((bsz * s, D_MODEL), F32),
        grid=(bsz, nt),
        in_specs=[
            pl.BlockSpec((TOK_TILE, D_MODEL), lambda b, i: (rows(b, i), 0)),
            pl.BlockSpec((TOK_TILE, D_MODEL), lambda b, i: (rows(b, i), 0)),
            pl.BlockSpec((1, ATTN_GROUPS, halves, HEAD_DIM, QCOLS), lambda b, i: (b, 0, i, 0, 0)),
            pl.BlockSpec((TOK_TILE, ATTN_WIDTH), lambda b, i: (rows(b, i), SLAB_ZATT // ATTN_WIDTH)),
            pl.BlockSpec((TOK_TILE, D_MODEL), lambda b, i: (rows(b, i), SLAB_GATT // D_MODEL)),
            const((ATTN_WIDTH, D_MODEL)), const((D_MODEL, D_MODEL)), const((1, D_MODEL)),
        ],
        out_specs=pl.BlockSpec((TOK_TILE, D_MODEL), lambda b, i: (rows(b, i), 0)),
        compiler_params=_cparams(2),
        name="final",
    )(x2, gs, o_t, slab, slab, wba, wout, fnw)


_W_OFF = np.cumsum([0, SSM_D_INNER, SSM_CONV_DIM, SSM_HEADS, ATTN_WIDTH, 6 * KV_WIDTH, 3 * ATTN_HEADS,
                    ATTN_WIDTH, D_MODEL, D_MODEL]).tolist()


def _wslab_kernel(w_ref, o_ref):
    x = w_ref[...]
    seg = lambda i, j: x[_W_OFF[i]:_W_OFF[j], :]
    pad = jnp.zeros((128 - SSM_HEADS - 3 * ATTN_HEADS, x.shape[1]), x.dtype)
    o_ref[...] = jnp.concatenate(
        [seg(1, 2), seg(0, 1), seg(3, 4), seg(6, 9), seg(4, 5), seg(2, 3), seg(5, 6), pad], axis=0).astype(BF16)


def _slab_weights(w_all, layer):
    w_t = jnp.swapaxes(w_all, 1, 2)
    _, n, d = w_t.shape
    assert all(o % BF16_ROWS == 0 for o in _W_OFF)
    return pl.pallas_call(
        _wslab_kernel,
        out_shape=jax.ShapeDtypeStruct((SLAB_W, d), BF16),
        grid=(d // 128,),
        in_specs=[pl.BlockSpec((None, n, 128), lambda c: (layer, 0, c))],
        out_specs=pl.BlockSpec((SLAB_W, 128), lambda c: (0, c)),
        compiler_params=_cparams(1),
        name="wslab",
    )(w_t)


def _rope_tables(s):
    half = HEAD_DIM // 2
    inv_freq = 1.0 / (ROPE_THETA ** (np.arange(half, dtype=np.float64) * 2.0 / HEAD_DIM))
    ang = np.arange(s, dtype=np.float64)[:, None] * inv_freq[None, :]
    cos, sin = np.cos(ang), np.sin(ang)
    cos_t = np.concatenate([cos, cos] * (128 // HEAD_DIM), axis=1)
    sin_t = np.concatenate([-sin, sin] * (128 // HEAD_DIM), axis=1)
    return jnp.asarray(cos_t, F32), jnp.asarray(sin_t, F32)


def _pad_lanes(v, width=128):
    v = v.reshape(1, -1)
    return jnp.concatenate([v, jnp.zeros((1, width - v.shape[1]), v.dtype)], axis=1)


def kernel(x, norm_w, w_in, conv_w, conv_b, dt_bias, a_log, d_skip, ssm_norm_w, cmp_pos_k, cmp_pos_v, cmp_k_w1,
           cmp_k_b1, cmp_k_w2, cmp_v_w1, cmp_v_b1, cmp_v_w2, w_branch_ssm, w_branch_attn, w_out, final_norm_w):
    bsz, s, _ = x.shape
    depth = norm_w.shape[0]
    cos_t, sin_t = _rope_tables(s)
    x2 = x.reshape(bsz * s, D_MODEL)
    for layer in range(depth):
        slab = _proj(x2, norm_w[layer].reshape(1, -1), _slab_weights(w_in, layer))

        gs = _ssd(slab, bsz, s, conv_w[layer], conv_b[layer].reshape(1, -1), _pad_lanes(dt_bias[layer]),
                  _pad_lanes(a_log[layer]), jnp.repeat(d_skip[layer], SSM_HEAD_DIM).reshape(1, -1),
                  ssm_norm_w[layer].reshape(1, -1), w_branch_ssm[layer].astype(BF16))

        q_t, ks, vst, kw, vwt, hkv, gts = _prep(slab, bsz, s, cos_t, sin_t)

        pos = jnp.stack([cmp_pos_k[layer], cmp_pos_v[layer]]).reshape(2, 2, 1, CMP_STRIDE * HEAD_DIM)
        w1 = jnp.stack([cmp_k_w1[layer], cmp_v_w1[layer]])
        w1 = jnp.concatenate([w1[:, :CMP_STRIDE * HEAD_DIM], w1[:, CMP_STRIDE * HEAD_DIM:]], axis=2).astype(BF16)
        b1 = jnp.stack([cmp_k_b1[layer], cmp_v_b1[layer]]).reshape(2, 1, CMP_HIDDEN)
        w2 = jnp.stack([cmp_k_w2[layer], cmp_v_w2[layer]])
        w2 = jnp.concatenate([w2, jnp.zeros((2, CMP_HIDDEN, 128 - HEAD_DIM), w2.dtype)], axis=2).astype(BF16)
        c_nat, c_tr = _compress(hkv, pos, w1, b1, w2)

        o_t = _attn(q_t, c_nat, c_tr, ks, vst, kw, vwt, gts, s)

        x2 = _final(x2, gs, o_t, slab, w_branch_attn[layer].astype(BF16), w_out[layer].astype(BF16),
                    final_norm_w.reshape(1, -1), bsz, s, layer == depth - 1)
    return x2.reshape(bsz, s, D_MODEL)
```

```python
import functools
import math

import numpy as np
import jax
import jax.numpy as jnp
from jax import lax
from jax.experimental import pallas as pl
from jax.experimental.pallas import tpu as pltpu

F32 = jnp.float32
BF16 = jnp.bfloat16

EPS = 1e-6
ROPE_THETA = 10000.0
NEG = -1e30

D_MODEL = 1024
SSM_HEADS = 16
SSM_HEAD_DIM = 64
SSM_D_INNER = 1024
SSM_GROUPS = 4
SSM_STATE = 128
SSM_CONV = 4
SSM_CHUNK = 256
SSM_CONV_DIM = 2048
ATTN_HEADS = 16
ATTN_GROUPS = 4
HEAD_DIM = 64
ATTN_WIDTH = 1024
KV_WIDTH = 256
CMP_BLOCK = 32
CMP_STRIDE = 16
CMP_HIDDEN = 256
SLC_BLOCK = 64
SLC_TOP_N = 16
WINDOW = 512
Q_TILE = 128
KEY_TILE = 256
TOK_TILE = 256
HEADS_PER_GROUP = ATTN_HEADS // ATTN_GROUPS
QCOLS = HEADS_PER_GROUP * Q_TILE
SEL_UNROLL = 4
GROUPS_PER_STEP = 2
WIN_POS = 8
BF16_ROWS = 16
V_ROWS = HEAD_DIM + BF16_ROWS
LOG2E = math.log2(math.e)

SLAB_XBC = 0
SLAB_ZSSM = 2048
SLAB_Q = 3072
SLAB_ZATT = 4096
SLAB_GSSM = 5120
SLAB_GATT = 6144
SLAB_KV = 7168
SLAB_SMALL = 8704
SLAB_W = 8832
GATE_LANE0 = SSM_HEADS

VMEM_LIMIT = 56 * 1024 * 1024


def _cparams(n_grid):
    return pltpu.CompilerParams(dimension_semantics=("arbitrary",) * n_grid,
                                vmem_limit_bytes=VMEM_LIMIT)


def _silu(x):
    h = 0.5 * x
    return h + h * jnp.tanh(h)


def _proj_kernel(x_ref, nw_ref, w_ref, o_ref):
    x = x_ref[...]
    ms = jnp.mean(x * x, axis=-1, keepdims=True)
    h = (x * lax.rsqrt(ms + EPS)) * nw_ref[...]
    o_ref[...] = lax.dot_general(h.astype(BF16), w_ref[...], (((1,), (1,)), ((), ())),
                                 preferred_element_type=F32)


def _proj(x2, norm_w, w_slab_t):
    t = x2.shape[0]
    tm, tn = 1024, SLAB_W // 3
    return pl.pallas_call(
        _proj_kernel,
        out_shape=jax.ShapeDtypeStruct((t, SLAB_W), F32),
        grid=(SLAB_W // tn, t // tm),
        in_specs=[pl.BlockSpec((tm, D_MODEL), lambda j, i: (i, 0)),
                  pl.BlockSpec((1, D_MODEL), lambda j, i: (0, 0)),
                  pl.BlockSpec((tn, D_MODEL), lambda j, i: (j, 0))],
        out_specs=pl.BlockSpec((tm, tn), lambda j, i: (i, j)),
        compiler_params=_cparams(2),
        name="proj",
    )(x2, norm_w, w_slab_t)


def _split3(v):
    v1 = v.astype(BF16)
    r1 = v - v1.astype(F32)
    v2 = r1.astype(BF16)
    v3 = (r1 - v2.astype(F32)).astype(BF16)
    return v1, v2, v3


def _dot3(parts, rhs, lhs_side=True):
    out = None
    for p in parts:
        d = (jnp.dot(p, rhs, preferred_element_type=F32) if lhs_side
             else jnp.dot(rhs, p, preferred_element_type=F32))
        out = d if out is None else out + d
    return out


def _ssd_kernel(xbc_ref, prev_ref, small_ref, z_ref, g_ref, convw_ref, convb_ref, dtb_ref, alog_ref,
                dskip_ref, nw_ref, wb_ref, tri_ref, exp_ref, o_ref, st_ref, ext_ref):
    @pl.when(pl.program_id(0) == 0)
    def _():
        st_ref[...] = jnp.zeros_like(st_ref)

    for b in range(xbc_ref.shape[0]):
        _ssd_chunk(xbc_ref.at[b], prev_ref.at[b], small_ref.at[b], z_ref.at[b], g_ref.at[b], convw_ref, convb_ref,
                   dtb_ref, alog_ref, dskip_ref, nw_ref, wb_ref, tri_ref, exp_ref, o_ref.at[b], st_ref.at[b],
                   ext_ref.at[b])


def _ssd_chunk(xbc_ref, prev_ref, small_ref, z_ref, g_ref, convw_ref, convb_ref, dtb_ref, alog_ref,
               dskip_ref, nw_ref, wb_ref, tri_ref, exp_ref, o_ref, st_ref, ext_ref):
    c = pl.program_id(0)
    L = SSM_CHUNK

    prev = jnp.where(c > 0, prev_ref[...], 0.0)
    cols = []
    for j in range(SSM_CONV_DIM // 128):
        lanes = slice(j * 128, (j + 1) * 128)
        ext_ref[j, 0:8, :] = prev[:, lanes]
        ext_ref[j, 8:8 + L, :] = xbc_ref[:, lanes]
        acc = jnp.broadcast_to(convb_ref[:, lanes], (L, 128))
        for k in range(SSM_CONV):
            r0 = 8 - (SSM_CONV - 1) + k
            acc = acc + convw_ref[k:k + 1, lanes] * ext_ref[j, r0:r0 + L, :]
        cols.append(_silu(acc))
    xbc = jnp.concatenate(cols, axis=1)
    xs = xbc[:, :SSM_D_INNER]
    bm = xbc[:, SSM_D_INNER:SSM_D_INNER + SSM_GROUPS * SSM_STATE]
    cm = xbc[:, SSM_D_INNER + SSM_GROUPS * SSM_STATE:]

    lane = lax.broadcasted_iota(jnp.int32, (L, 128), 1)
    head_lane = lane < SSM_HEADS
    u = small_ref[...] + dtb_ref[...]
    dt = jnp.maximum(u, 0.0) + jnp.log1p(jnp.exp(-jnp.abs(u)))
    dt = jnp.where(head_lane, dt, 0.0)
    a = dt * -jnp.exp(alog_ref[...])
    cs = _dot3(_split3(a), tri_ref[...], lhs_side=False)
    cs_last = cs[L - 1:L, :]
    ecs = jnp.exp(cs)
    ds = jnp.exp(cs_last - cs)
    packed = (dt + pltpu.roll(jnp.where(head_lane, ecs, 0.0), SSM_HEADS, 1)
              + pltpu.roll(jnp.where(head_lane, ds, 0.0), 2 * SSM_HEADS, 1))
    expanded = _dot3(_split3(packed)[:2], exp_ref[...])
    dt_e = expanded[:, :SSM_D_INNER]
    ecs_e = expanded[:, SSM_D_INNER:2 * SSM_D_INNER]
    ds_e = expanded[:, 2 * SSM_D_INNER:]
    xdt = xs * dt_e
    xds = xdt * ds_e
    cs_t = cs.T

    row = lax.broadcasted_iota(jnp.int32, (L, L), 0)
    col = lax.broadcasted_iota(jnp.int32, (L, L), 1)
    tril = row >= col
    half = lax.broadcasted_iota(jnp.int32, (L, 128), 1) < SSM_HEAD_DIM
    gw = SSM_D_INNER // SSM_GROUPS

    y_groups = []
    for g in range(SSM_GROUPS):
        bg = bm[:, g * SSM_STATE:(g + 1) * SSM_STATE]
        cg = cm[:, g * SSM_STATE:(g + 1) * SSM_STATE].astype(BF16)
        bg_t = bg.T.astype(BF16)
        cb = jnp.dot(cg, bg_t, preferred_element_type=F32)
        st = st_ref[g]
        y_off = jnp.dot(cg, st.astype(BF16), preferred_element_type=F32) * ecs_e[:, g * gw:(g + 1) * gw]
        st_ref[g] = (ecs_e[L - 1:L, g * gw:(g + 1) * gw] * st
                     + jnp.dot(bg_t, xds[:, g * gw:(g + 1) * gw].astype(BF16), preferred_element_type=F32))
        pairs = []
        for pr in range(2):
            rhs = xdt[:, g * gw + pr * 128:g * gw + (pr + 1) * 128].astype(BF16)
            ys = []
            for hh in range(2):
                h = g * 4 + pr * 2 + hh
                seg = cs[:, h:h + 1] - cs_t[h:h + 1, :]
                dec = jnp.exp(jnp.where(tril, seg, NEG))
                ys.append(jnp.dot((cb * dec).astype(BF16), rhs, preferred_element_type=F32))
            pairs.append(jnp.where(half, ys[0], ys[1]))
        y_groups.append(jnp.concatenate(pairs, axis=1) + y_off)
    y = jnp.concatenate(y_groups, axis=1) + dskip_ref[...] * xs

    v = y * _silu(z_ref[...])
    ms = jnp.mean(v * v, axis=-1, keepdims=True)
    vn = (v * lax.rsqrt(ms + EPS)) * nw_ref[...]
    ys = jnp.dot(vn.astype(BF16), wb_ref[...], preferred_element_type=F32)
    o_ref[...] = (jax.nn.sigmoid(g_ref[...]) * ys).astype(o_ref.dtype)


def _ssd(slab, bsz, s, conv_w, conv_b, dtb, alog, dskip_e, nw, wb):
    nt = s // SSM_CHUNK
    L = SSM_CHUNK
    tri = jnp.asarray(np.tril(np.ones((L, L), np.float32)), BF16)
    e = np.zeros((128, 3 * SSM_D_INNER), np.float32)
    for k in range(3):
        for h in range(SSM_HEADS):
            e[k * SSM_HEADS + h, k * SSM_D_INNER + h * SSM_HEAD_DIM:k * SSM_D_INNER + (h + 1) * SSM_HEAD_DIM] = 1.0
    e = jnp.asarray(e, BF16)
    const = lambda shape: pl.BlockSpec(shape, lambda c: (0,) * len(shape))
    slab3 = slab.reshape(bsz, s, SLAB_W)
    out = pl.pallas_call(
        _ssd_kernel,
        out_shape=jax.ShapeDtypeStruct((bsz, s, SSM_D_INNER), BF16),
        grid=(nt,),
        in_specs=[
            pl.BlockSpec((bsz, L, SSM_CONV_DIM), lambda c: (0, c, SLAB_XBC // SSM_CONV_DIM)),
            pl.BlockSpec((bsz, 8, SSM_CONV_DIM), lambda c: (0, jnp.maximum(c * (L // 8) - 1, 0), 0)),
            pl.BlockSpec((bsz, L, 128), lambda c: (0, c, SLAB_SMALL // 128)),
            pl.BlockSpec((bsz, L, SSM_D_INNER), lambda c: (0, c, SLAB_ZSSM // SSM_D_INNER)),
            pl.BlockSpec((bsz, L, D_MODEL), lambda c: (0, c, SLAB_GSSM // D_MODEL)),
            const((SSM_CONV, SSM_CONV_DIM)), const((1, SSM_CONV_DIM)), const((1, 128)), const((1, 128)),
            const((1, SSM_D_INNER)), const((1, SSM_D_INNER)), const((SSM_D_INNER, D_MODEL)),
            const((L, L)), const((128, 3 * SSM_D_INNER)),
        ],
        out_specs=pl.BlockSpec((bsz, L, D_MODEL), lambda c: (0, c, 0)),
        scratch_shapes=[pltpu.VMEM((bsz, SSM_GROUPS, SSM_STATE, SSM_D_INNER // SSM_GROUPS), F32),
                        pltpu.VMEM((bsz, SSM_CONV_DIM // 128, L + 8, 128), F32)],
        compiler_params=_cparams(1),
        name="ssd",
    )(slab3, slab3, slab3, slab3, slab3, conv_w, conv_b, dtb, alog, dskip_e, nw, wb, tri, e)
    return out.reshape(bsz * s, SSM_D_INNER)


def _prep_kernel(q_ref, kc_ref, vc_ref, ks_ref, vs_ref, kw_ref, vw_ref, small_ref, cos_ref, sin_ref,
                 qt_ref, kso_ref, vst_ref, kwo_ref, vwt_ref, hkv_ref, gt_ref, stage_ref):
    cos = cos_ref[...]
    sin = sin_ref[...]

    def rope(x):
        w = x.shape[1]
        cf = jnp.concatenate([cos] * (w // 128), axis=1)
        sf = jnp.concatenate([sin] * (w // 128), axis=1)
        lane = lax.broadcasted_iota(jnp.int32, x.shape, 1)
        first = (lane % HEAD_DIM) < (HEAD_DIM // 2)
        swapped = jnp.where(first, pltpu.roll(x, w - HEAD_DIM // 2, 1), pltpu.roll(x, HEAD_DIM // 2, 1))
        return x * cf + swapped * sf

    halves = TOK_TILE // Q_TILE
    lane128 = lax.broadcasted_iota(jnp.int32, (TOK_TILE, 128), 1)
    row128 = lax.broadcasted_iota(jnp.int32, (TOK_TILE, 128), 0)
    low64 = lane128 < HEAD_DIM
    blocks_per_tile = KEY_TILE // SLC_BLOCK
    tile_in_super = pl.program_id(1) & (SEL_UNROLL - 1)
    block_in_super = tile_in_super * blocks_per_tile + lax.shift_right_logical(row128, int(math.log2(SLC_BLOCK)))
    block_onehot = jnp.where(lane128 - HEAD_DIM == block_in_super, 1.0, 0.0)
    win_pos = ((pl.program_id(1) * halves + lax.shift_right_logical(row128, int(math.log2(Q_TILE))))
               & (WIN_POS - 1))
    win_onehot = jnp.where(lane128 - HEAD_DIM == win_pos, 1.0, 0.0)
    assert TOK_TILE == KEY_TILE and SEL_UNROLL * blocks_per_tile == BF16_ROWS

    def group_cols(x, g, rest):
        r = x if g == 0 else pltpu.roll(x, KV_WIDTH - HEAD_DIM * g, 1)
        return jnp.where(low64, r[:, :128], rest)

    def values_t(v_t, g, lo, hi):
        n = hi - lo
        extra = jnp.where(lax.broadcasted_iota(jnp.int32, (V_ROWS - HEAD_DIM, n), 0) == 0, 1.0, 0.0)
        return jnp.concatenate([v_t[g * HEAD_DIM:(g + 1) * HEAD_DIM, lo:hi], extra], axis=0).astype(BF16)

    qr = rope(q_ref[...]) * (HEAD_DIM ** -0.5 * LOG2E)
    for hf in range(halves):
        q_t = qr[hf * Q_TILE:(hf + 1) * Q_TILE, :].T
        for g in range(ATTN_GROUPS):
            base = g * HEADS_PER_GROUP * HEAD_DIM
            blk = jnp.concatenate([q_t[base + h * HEAD_DIM:base + (h + 1) * HEAD_DIM, :]
                                   for h in range(HEADS_PER_GROUP)], axis=1)
            qt_ref[0, g, hf] = blk.astype(BF16)

    ksr = rope(ks_ref[...])
    vs_t = vs_ref[...].T
    for g in range(ATTN_GROUPS):
        kso_ref[0, g, 0] = group_cols(ksr, g, block_onehot).astype(BF16)
        vst_ref[0, g, 0] = values_t(vs_t, g, 0, TOK_TILE)

    kwr = rope(kw_ref[...])
    vw_t = vw_ref[...].T
    for g in range(ATTN_GROUPS):
        kg = group_cols(kwr, g, win_onehot).astype(BF16)
        for hf in range(halves):
            kwo_ref[0, g, hf] = kg[hf * Q_TILE:(hf + 1) * Q_TILE, :]
            vwt_ref[0, g, hf] = values_t(vw_t, g, hf * Q_TILE, (hf + 1) * Q_TILE)

    per_half = CMP_STRIDE
    n_half = TOK_TILE // per_half
    lane_h = lax.broadcasted_iota(jnp.int32, (n_half, 128), 1) < HEAD_DIM
    for kv, x in enumerate((rope(kc_ref[...]), vc_ref[...])):
        for c in range(KV_WIDTH // 128):
            stage_ref[kv, c] = x[:, c * 128:(c + 1) * 128]
        for c in range(KV_WIDTH // 128):
            pieces = [stage_ref[kv, c, pl.ds(l, n_half, stride=per_half), :] for l in range(per_half)]
            swapped = [pltpu.roll(p, HEAD_DIM, 1) for p in pieces]
            for hi in range(128 // HEAD_DIM):
                tiles = []
                for l in range(0, per_half, 2):
                    if hi == 0:
                        tiles.append(jnp.where(lane_h, pieces[l], swapped[l + 1]))
                    else:
                        tiles.append(jnp.where(lane_h, swapped[l], pieces[l + 1]))
                hkv_ref[kv, 0, c * (128 // HEAD_DIM) + hi] = jnp.concatenate(tiles, axis=1)

    gates_t = jax.nn.sigmoid(small_ref[...]).T
    zeros_g = jnp.zeros((8 - 3, QCOLS), F32)
    for hf in range(halves):
        for g in range(ATTN_GROUPS):
            rows_c = []
            for cc in range(3):
                rows_c.append(jnp.concatenate(
                    [gates_t[GATE_LANE0 + (g * HEADS_PER_GROUP + r) * 3 + cc:GATE_LANE0 + (g * HEADS_PER_GROUP + r) * 3 + cc + 1,
                             hf * Q_TILE:(hf + 1) * Q_TILE] for r in range(HEADS_PER_GROUP)], axis=1))
            gt_ref[0, g, hf] = jnp.concatenate(rows_c + [zeros_g], axis=0)


def _prep(slab, bsz, s, cos_t, sin_t):
    nt = s // TOK_TILE
    nq = s // Q_TILE
    halves = TOK_TILE // Q_TILE
    G = ATTN_GROUPS
    rows = lambda b, i: b * nt + i
    kvspec = lambda k: pl.BlockSpec((TOK_TILE, KV_WIDTH), lambda b, i: (rows(b, i), SLAB_KV // KV_WIDTH + k))
    out5 = lambda blk: pl.BlockSpec(blk, lambda b, i: (b, 0, i, 0, 0))
    return pl.pallas_call(
        _prep_kernel,
        out_shape=[
            jax.ShapeDtypeStruct((bsz, G, nq, HEAD_DIM, QCOLS), BF16),
            jax.ShapeDtypeStruct((bsz, G, nt, KEY_TILE, 128), BF16),
            jax.ShapeDtypeStruct((bsz, G, nt, V_ROWS, KEY_TILE), BF16),
            jax.ShapeDtypeStruct((bsz, G, nq, Q_TILE, 128), BF16),
            jax.ShapeDtypeStruct((bsz, G, nq, V_ROWS, Q_TILE), BF16),
            jax.ShapeDtypeStruct((2, bsz, G, s // CMP_STRIDE, CMP_STRIDE * HEAD_DIM), F32),
            jax.ShapeDtypeStruct((bsz, G, nq, 8, QCOLS), F32),
        ],
        grid=(bsz, nt),
        in_specs=[
            pl.BlockSpec((TOK_TILE, ATTN_WIDTH), lambda b, i: (rows(b, i), SLAB_Q // ATTN_WIDTH)),
            kvspec(0), kvspec(1), kvspec(2), kvspec(3), kvspec(4), kvspec(5),
            pl.BlockSpec((TOK_TILE, 128), lambda b, i: (rows(b, i), SLAB_SMALL // 128)),
            pl.BlockSpec((TOK_TILE, 128), lambda b, i: (i, 0)),
            pl.BlockSpec((TOK_TILE, 128), lambda b, i: (i, 0)),
        ],
        out_specs=[
            out5((1, G, halves, HEAD_DIM, QCOLS)),
            out5((1, G, 1, KEY_TILE, 128)),
            out5((1, G, 1, V_ROWS, KEY_TILE)),
            out5((1, G, halves, Q_TILE, 128)),
            out5((1, G, halves, V_ROWS, Q_TILE)),
            pl.BlockSpec((2, 1, G, TOK_TILE // CMP_STRIDE, CMP_STRIDE * HEAD_DIM), lambda b, i: (0, b, 0, i, 0)),
            out5((1, G, halves, 8, QCOLS)),
        ],
        scratch_shapes=[pltpu.VMEM((2, KV_WIDTH // 128, TOK_TILE, 128), F32)],
        compiler_params=_cparams(2),
        name="prep",
    )(slab, slab, slab, slab, slab, slab, slab, slab, cos_t, sin_t)


def _compress_kernel(h_ref, pos_ref, w1_ref, b1_ref, w2_ref, nat_ref, tr_ref):
    h = h_ref[0, 0, 0]
    nh = h.shape[0]
    w1 = w1_ref[0]
    a = jnp.dot((h + pos_ref[0, 0]).astype(BF16), w1[:, :CMP_HIDDEN], preferred_element_type=F32)
    b = jnp.dot((h + pos_ref[0, 1]).astype(BF16), w1[:, CMP_HIDDEN:], preferred_element_type=F32)
    hid = a + pltpu.roll(b, nh - 1, 0) + b1_ref[0]
    out = jnp.dot(jax.nn.gelu(hid).astype(BF16), w2_ref[0], preferred_element_type=F32)
    nat_ref[0, 0, 0] = out.astype(BF16)
    tr_ref[0, 0, 0] = out.T[:HEAD_DIM, :].astype(BF16)


def _compress(hkv, pos, w1, b1, w2):
    _, bsz, G, nh, _ = hkv.shape
    return pl.pallas_call(
        _compress_kernel,
        out_shape=[jax.ShapeDtypeStruct((2, bsz, G, nh, 128), BF16),
                   jax.ShapeDtypeStruct((2, bsz, G, HEAD_DIM, nh), BF16)],
        grid=(2, bsz, G),
        in_specs=[
            pl.BlockSpec((1, 1, 1, nh, CMP_STRIDE * HEAD_DIM), lambda k, b, g: (k, b, g, 0, 0)),
            pl.BlockSpec((1, 2, 1, CMP_STRIDE * HEAD_DIM), lambda k, b, g: (k, 0, 0, 0)),
            pl.BlockSpec((1, CMP_STRIDE * HEAD_DIM, 2 * CMP_HIDDEN), lambda k, b, g: (k, 0, 0)),
            pl.BlockSpec((1, 1, CMP_HIDDEN), lambda k, b, g: (k, 0, 0)),
            pl.BlockSpec((1, CMP_HIDDEN, 128), lambda k, b, g: (k, 0, 0)),
        ],
        out_specs=[pl.BlockSpec((1, 1, 1, nh, 128), lambda k, b, g: (k, b, g, 0, 0)),
                   pl.BlockSpec((1, 1, 1, HEAD_DIM, nh), lambda k, b, g: (k, b, g, 0, 0))],
        compiler_params=_cparams(3),
        name="compress",
    )(hkv, pos, w1, b1, w2)


def _attn_kernel(qt_ref, qn_ref, kc_ref, vct_ref, ks_ref, vst_ref, kw_ref, vwt_ref, g_ref, tri_ref, o_ref, p4_ref,
                 imp_ref, oc_ref, bias_ref, biasm_ref, sbuf_a_ref, sbuf_b_ref, sd_ref, **static):
    for gg in range(GROUPS_PER_STEP):
        one = lambda ref, axis: ref.at[(slice(None),) * axis + (slice(gg, gg + 1),)]
        _attn_group(one(qt_ref, 1), one(qn_ref, 1), one(kc_ref, 2), one(vct_ref, 2), one(ks_ref, 1), one(vst_ref, 1),
                    one(kw_ref, 1), one(vwt_ref, 1), one(g_ref, 1), tri_ref, one(o_ref, 1), p4_ref, imp_ref,
                    oc_ref.at[gg], bias_ref.at[gg], biasm_ref.at[gg], sbuf_a_ref, sbuf_b_ref, sd_ref, **static)


def _attn_group(qt_ref, qn_ref, kc_ref, vct_ref, ks_ref, vst_ref, kw_ref, vwt_ref, g_ref, tri_ref, o_ref, p4_ref,
                imp_ref, oc_ref, bias_ref, biasm_ref, sbuf_a_ref, sbuf_b_ref, sd_ref, *, nh, ns, nst, nq, top_n):
    qi = pl.program_id(2)
    slot = qi & 1
    s0 = qi * Q_TILE
    q64 = qt_ref[0, 0, 0]
    lane_tok = lax.broadcasted_iota(jnp.int32, (1, QCOLS), 1) & (Q_TILE - 1)
    zero_half = jnp.zeros((128 - HEAD_DIM, QCOLS), BF16)
    assert KEY_TILE == 2 * Q_TILE

    def cmp_scores(q64x):
        return jnp.dot(kc_ref[0, 0, 0], jnp.concatenate([q64x, zero_half], axis=0),
                       preferred_element_type=F32)

    def importance(tile, sc, dst):
        t_row = tile * Q_TILE + lane_tok
        cend = lax.broadcasted_iota(jnp.int32, (nh, QCOLS), 0) * CMP_STRIDE + (CMP_BLOCK - 1)
        sm = jnp.where(cend <= t_row, sc, NEG)
        mc = jnp.max(sm, axis=0, keepdims=True)
        pc = jnp.exp2(sm - mc)
        lc = jnp.sum(pc, axis=0, keepdims=True)
        pn = pc * jnp.where(t_row >= CMP_BLOCK - 1, 1.0 / lc, 0.0)
        oc_ref[dst] = jnp.dot(vct_ref[0, 0, 0], pn.astype(BF16), preferred_element_type=F32)
        p4 = pn[:, 0:Q_TILE]
        for h in range(1, HEADS_PER_GROUP):
            p4 = p4 + pn[:, h * Q_TILE:(h + 1) * Q_TILE]
        p4_ref[0:8, :] = jnp.zeros((8, Q_TILE), F32)
        p4_ref[8:8 + nh, :] = p4
        imp = p4_ref[pl.ds(7, ns, stride=4), :]
        for cc in range(4):
            imp = imp + p4_ref[pl.ds(8 + cc, ns, stride=4), :]
        imp_ref[...] = imp

    rounds = top_n - 3
    blk = lax.broadcasted_iota(jnp.int32, (ns, Q_TILE), 0)

    def future_blocks(tile):
        return blk * SLC_BLOCK > tile * Q_TILE + lax.broadcasted_iota(jnp.int32, (ns, Q_TILE), 1)

    def pick_start(tile):
        tok = tile * Q_TILE + lax.broadcasted_iota(jnp.int32, (ns, Q_TILE), 1)
        cur = lax.shift_right_logical(tok, int(math.log2(SLC_BLOCK)))
        forced = (blk == 0) | (blk == cur) | (blk == cur - 1)
        return jnp.where(forced, -2.0, jnp.where(future_blocks(tile), -1.0, imp_ref[...]))

    def pick_round(val):
        best = jnp.max(val, axis=0, keepdims=True)
        first = jnp.min(jnp.where(val == best, blk, ns), axis=0, keepdims=True)
        return jnp.where(blk == first, -2.0, val)

    def write_tables(tile, val, dst):
        bias = jnp.where((val == -2.0) & jnp.logical_not(future_blocks(tile)), 0.0, NEG)
        full_blocks = lax.shift_right_logical(tile, 1) * (KEY_TILE // SLC_BLOCK)
        bias_main = jnp.where(blk < full_blocks, bias, NEG)
        for st in range(nst):
            for table, ref in ((bias, bias_ref), (bias_main, biasm_ref)):
                rows = table[st * BF16_ROWS:(st + 1) * BF16_ROWS, :]
                ref[dst, st] = jnp.concatenate([rows] * HEADS_PER_GROUP, axis=1).astype(BF16)

    def pick_blocks(tile, dst):
        val = pick_start(tile)
        for _ in range(rounds):
            val = pick_round(val)
        write_tables(tile, val, dst)

    @pl.when(qi == 0)
    def _():
        importance(0, cmp_scores(q64), 0)
        pick_blocks(0, 0)

    next_tile = jnp.minimum(qi + 1, nq - 1)
    sc_next = cmp_scores(qn_ref[0, 0, 0])

    zero_rows = jnp.zeros((128 - HEAD_DIM - BF16_ROWS, QCOLS), BF16)
    eye = jnp.where(lax.broadcasted_iota(jnp.int32, (Q_TILE, Q_TILE), 0)
                    == lax.broadcasted_iota(jnp.int32, (Q_TILE, Q_TILE), 1), 1.0, 0.0).astype(BF16)
    causal_tri = tri_ref[0]
    window_tri = tri_ref[1]

    def edge_scores(k_tile, rows16, tri):
        q_aug = jnp.concatenate([q64, rows16, zero_rows, tri], axis=0)
        return jnp.dot(jnp.concatenate([k_tile, eye], axis=1), q_aug, preferred_element_type=F32)

    n_back = WINDOW // Q_TILE
    row16 = lax.broadcasted_iota(jnp.int32, (BF16_ROWS, QCOLS), 0)
    back = (qi - row16) & (WIN_POS - 1)
    rows_w = jnp.where((row16 < WIN_POS) & (back <= n_back) & (qi - back < 0), NEG, 0.0).astype(BF16)
    win_tile = lambda d: jnp.where(qi - d < 0, qi - d + nq, qi - d)
    sw = jnp.concatenate(
        [edge_scores(kw_ref[0, 0, win_tile(0)], rows_w, causal_tri),
         jnp.dot(jnp.concatenate([kw_ref[0, 0, win_tile(d)] for d in range(1, n_back)], axis=0),
                 jnp.concatenate([q64, rows_w, zero_rows], axis=0), preferred_element_type=F32),
         edge_scores(kw_ref[0, 0, win_tile(n_back)], rows_w, window_tri)], axis=0)
    vw = jnp.concatenate([vwt_ref[0, 0, win_tile(d)] for d in range(n_back + 1)], axis=1)

    n_full = lax.shift_right_logical(qi, 1)

    def produce_tile(stc, j, dst_ref):
        q_aug = jnp.concatenate([q64, biasm_ref[slot, stc], zero_rows], axis=0)
        s = jnp.dot(ks_ref[0, 0, stc * SEL_UNROLL + j], q_aug, preferred_element_type=F32)
        dst_ref[j * KEY_TILE:(j + 1) * KEY_TILE, :] = s
        return jnp.max(s, axis=0, keepdims=True)

    def produce(st, dst_ref):
        stc = jnp.minimum(st, nst - 1)
        mx = produce_tile(stc, 0, dst_ref)
        for j in range(1, SEL_UNROLL):
            mx = jnp.maximum(mx, produce_tile(stc, j, dst_ref))
        return mx

    def consume_tile(kt, j, cur_ref, m_new):
        p = jnp.exp2(cur_ref[j * KEY_TILE:(j + 1) * KEY_TILE, :] - m_new).astype(BF16)
        return jnp.dot(vst_ref[0, 0, kt], p, preferred_element_type=F32)

    def step(i, carry, cur_ref, next_ref):
        m, acc, mx = carry
        m_new = jnp.maximum(m, mx)
        stc = jnp.minimum(i + 1, nst - 1)
        mx_next, pv = None, None
        for j in range(SEL_UNROLL):
            mxj = produce_tile(stc, j, next_ref)
            mx_next = mxj if mx_next is None else jnp.maximum(mx_next, mxj)
            pvj = consume_tile(i * SEL_UNROLL + j, j, cur_ref, m_new)
            pv = pvj if pv is None else pv + pvj
        return m_new, jnp.exp2(m - m_new) * acc + pv, mx_next

    def sel_body(i, carry):
        return lax.cond((i & 1) == 0,
                        lambda c: step(i, c, sbuf_a_ref, sbuf_b_ref),
                        lambda c: step(i, c, sbuf_b_ref, sbuf_a_ref), carry)

    n_iter = lax.shift_right_logical(n_full + (SEL_UNROLL - 1), int(math.log2(SEL_UNROLL)))
    last = jnp.maximum(n_iter - 1, 0)
    init = (jnp.full((1, QCOLS), NEG, F32), jnp.zeros((V_ROWS, QCOLS), F32), produce(0, sbuf_a_ref))

    odd = (qi & 1) == 1
    rows_d = bias_ref[slot, lax.shift_right_logical(n_full, int(math.log2(SEL_UNROLL)))]
    all_neg = jnp.full((Q_TILE, QCOLS), NEG, BF16)
    tri_a = jnp.where(odd, jnp.zeros((Q_TILE, QCOLS), BF16), causal_tri)
    tri_b = jnp.where(odd, causal_tri, all_neg)
    sd_a = edge_scores(ks_ref[0, 0, n_full, 0:Q_TILE, :], rows_d, tri_a)
    sd_b = edge_scores(ks_ref[0, 0, n_full, Q_TILE:KEY_TILE, :], rows_d, tri_b)
    sd_ref[0:Q_TILE, :] = sd_a
    sd_ref[Q_TILE:KEY_TILE, :] = sd_b
    mx_d = jnp.maximum(jnp.max(sd_a, axis=0, keepdims=True), jnp.max(sd_b, axis=0, keepdims=True))

    importance(next_tile, sc_next, 1 - slot)

    pw = jnp.exp2(sw - jnp.max(sw, axis=0, keepdims=True))
    acc_w = jnp.dot(vw, pw.astype(BF16), preferred_element_type=F32)
    o_w = acc_w[:HEAD_DIM, :] * (1.0 / acc_w[HEAD_DIM:HEAD_DIM + 1, :])

    m_s, acc_s, mx_s = lax.fori_loop(0, last, sel_body, init)

    def finish(cur_ref):
        m_new = jnp.maximum(jnp.maximum(m_s, mx_s), mx_d)
        p_diag = jnp.exp2(sd_ref[...] - m_new).astype(BF16)
        pv = jnp.dot(vst_ref[0, 0, n_full], p_diag, preferred_element_type=F32)
        for j in range(SEL_UNROLL):
            pv = pv + consume_tile(last * SEL_UNROLL + j, j, cur_ref, m_new)
        pick_blocks(next_tile, 1 - slot)
        return jnp.exp2(m_s - m_new) * acc_s + pv

    acc_s = lax.cond((last & 1) == 0, lambda: finish(sbuf_a_ref), lambda: finish(sbuf_b_ref))
    o_s = acc_s[:HEAD_DIM, :] * (1.0 / acc_s[HEAD_DIM:HEAD_DIM + 1, :])

    gts = g_ref[0, 0, 0]
    o_ref[0, 0, 0] = (gts[0:1, :] * oc_ref[slot] + gts[1:2, :] * o_s + gts[2:3, :] * o_w).astype(o_ref.dtype)


def _attn(q_t, kc_nat, vc_tr, ks, vst, kw, vwt, gts, s):
    bsz, G, nq = q_t.shape[:3]
    nh = s // CMP_STRIDE
    ns = s // SLC_BLOCK
    nkt = s // KEY_TILE
    nst = nkt // SEL_UNROLL
    assert s % (SEL_UNROLL * KEY_TILE) == 0
    top_n = min(SLC_TOP_N, ns)
    gps = GROUPS_PER_STEP
    assert G % gps == 0
    per_q = lambda blk: pl.BlockSpec(blk, lambda b, g, i: (b, g, i, 0, 0))
    per_bg = lambda blk: pl.BlockSpec(blk, lambda b, g, i: (b, g, 0, 0, 0))
    kk = np.arange(Q_TILE)[:, None]
    tl = np.tile(np.arange(Q_TILE), HEADS_PER_GROUP)[None, :]
    tri = jnp.asarray(np.stack([np.where(kk <= tl, 0.0, NEG), np.where(kk > tl, 0.0, NEG)]), BF16)
    return pl.pallas_call(
        functools.partial(_attn_kernel, nh=nh, ns=ns, nst=nst, nq=nq, top_n=top_n),
        out_shape=jax.ShapeDtypeStruct((bsz, G, nq, HEAD_DIM, QCOLS), BF16),
        grid=(bsz, G // gps, nq),
        in_specs=[
            per_q((1, gps, 1, HEAD_DIM, QCOLS)),
            pl.BlockSpec((1, gps, 1, HEAD_DIM, QCOLS), lambda b, g, i: (b, g, jnp.minimum(i + 1, nq - 1), 0, 0)),
            pl.BlockSpec((1, 1, gps, nh, 128), lambda b, g, i: (0, b, g, 0, 0)),
            pl.BlockSpec((1, 1, gps, HEAD_DIM, nh), lambda b, g, i: (1, b, g, 0, 0)),
            per_bg((1, gps, nkt, KEY_TILE, 128)),
            per_bg((1, gps, nkt, V_ROWS, KEY_TILE)),
            per_bg((1, gps, nq, Q_TILE, 128)),
            per_bg((1, gps, nq, V_ROWS, Q_TILE)),
            per_q((1, gps, 1, 8, QCOLS)),
            pl.BlockSpec((2, Q_TILE, QCOLS), lambda b, g, i: (0, 0, 0)),
        ],
        out_specs=per_q((1, gps, 1, HEAD_DIM, QCOLS)),
        scratch_shapes=[pltpu.VMEM((nh + 8, Q_TILE), F32),
                        pltpu.VMEM((ns, Q_TILE), F32),
                        pltpu.VMEM((gps, 2, HEAD_DIM, QCOLS), F32),
                        pltpu.VMEM((gps, 2, nst, BF16_ROWS, QCOLS), BF16),
                        pltpu.VMEM((gps, 2, nst, BF16_ROWS, QCOLS), BF16),
                        pltpu.VMEM((SEL_UNROLL * KEY_TILE, QCOLS), F32),
                        pltpu.VMEM((SEL_UNROLL * KEY_TILE, QCOLS), F32),
                        pltpu.VMEM((KEY_TILE, QCOLS), F32)],
        compiler_params=_cparams(3),
        name="attn",
    )(q_t, q_t, kc_nat, vc_tr, ks, vst, kw, vwt, gts, tri)


def _final_kernel(x_ref, gs_ref, ot_ref, z_ref, g_ref, wba_ref, wout_ref, fnw_ref, o_ref, *, last):
    halves = TOK_TILE // Q_TILE
    parts = []
    for hf in range(halves):
        rows = []
        for g in range(ATTN_GROUPS):
            blk = ot_ref[0, g, hf].astype(F32)
            for h in range(HEADS_PER_GROUP):
                rows.append(blk[:, h * Q_TILE:(h + 1) * Q_TILE])
        parts.append(jnp.concatenate(rows, axis=0).T)
    o = jnp.concatenate(parts, axis=0)
    y_attn = jnp.dot((o * _silu(z_ref[...])).astype(BF16), wba_ref[...], preferred_element_type=F32)
    merged = gs_ref[...].astype(F32) + jax.nn.sigmoid(g_ref[...]) * y_attn
    xo = x_ref[...] + jnp.dot(merged.astype(BF16), wout_ref[...], preferred_element_type=F32)
    if last:
        ms = jnp.mean(xo * xo, axis=-1, keepdims=True)
        xo = (xo * lax.rsqrt(ms + EPS)) * fnw_ref[...]
    o_ref[...] = xo


def _final(x2, gs, o_t, slab, wba, wout, fnw, bsz, s, last):
    nt = s // TOK_TILE
    halves = TOK_TILE // Q_TILE
    rows = lambda b, i: b * nt + i
    const = lambda shape: pl.BlockSpec(shape, lambda b, i: (0,) * len(shape))
    return pl.pallas_call(
        functools.partial(_final_kernel, last=last),
        out_shape=jax.ShapeDtypeStruct((bsz * s, D_MODEL), F32),
        grid=(bsz, nt),
        in_specs=[
            pl.BlockSpec((TOK_TILE, D_MODEL), lambda b, i: (rows(b, i), 0)),
            pl.BlockSpec((TOK_TILE, D_MODEL), lambda b, i: (rows(b, i), 0)),
            pl.BlockSpec((1, ATTN_GROUPS, halves, HEAD_DIM, QCOLS), lambda b, i: (b, 0, i, 0, 0)),
            pl.BlockSpec((TOK_TILE, ATTN_WIDTH), lambda b, i: (rows(b, i), SLAB_ZATT // ATTN_WIDTH)),
            pl.BlockSpec((TOK_TILE, D_MODEL), lambda b, i: (rows(b, i), SLAB_GATT // D_MODEL)),
            const((ATTN_WIDTH, D_MODEL)), const((D_MODEL, D_MODEL)), const((1, D_MODEL)),
        ],
        out_specs=pl.BlockSpec((TOK_TILE, D_MODEL), lambda b, i: (rows(b, i), 0)),
        compiler_params=_cparams(2),
        name="final",
    )(x2, gs, o_t, slab, slab, wba, wout, fnw)


_W_OFF = np.cumsum([0, SSM_D_INNER, SSM_CONV_DIM, SSM_HEADS, ATTN_WIDTH, 6 * KV_WIDTH, 3 * ATTN_HEADS,
                    ATTN_WIDTH, D_MODEL, D_MODEL]).tolist()


def _wslab_kernel(w_ref, o_ref):
    x = w_ref[...]
    seg = lambda i, j: x[_W_OFF[i]:_W_OFF[j], :]
    pad = jnp.zeros((128 - SSM_HEADS - 3 * ATTN_HEADS, x.shape[1]), x.dtype)
    o_ref[...] = jnp.concatenate(
        [seg(1, 2), seg(0, 1), seg(3, 4), seg(6, 9), seg(4, 5), seg(2, 3), seg(5, 6), pad], axis=0).astype(BF16)


def _slab_weights(w_all, layer):
    w_t = jnp.swapaxes(w_all, 1, 2)
    _, n, d = w_t.shape
    assert all(o % BF16_ROWS == 0 for o in _W_OFF)
    return pl.pallas_call(
        _wslab_kernel,
        out_shape=jax.ShapeDtypeStruct((SLAB_W, d), BF16),
        grid=(d // 128,),
        in_specs=[pl.BlockSpec((None, n, 128), lambda c: (layer, 0, c))],
        out_specs=pl.BlockSpec((SLAB_W, 128), lambda c: (0, c)),
        compiler_params=_cparams(1),
        name="wslab",
    )(w_t)


def _rope_tables(s):
    half = HEAD_DIM // 2
    inv_freq = 1.0 / (ROPE_THETA ** (np.arange(half, dtype=np.float64) * 2.0 / HEAD_DIM))
    ang = np.arange(s, dtype=np.float64)[:, None] * inv_freq[None, :]
    cos, sin = np.cos(ang), np.sin(ang)
    cos_t = np.concatenate([cos, cos] * (128 // HEAD_DIM), axis=1)
    sin_t = np.concatenate([-sin, sin] * (128 // HEAD_DIM), axis=1)
    return jnp.asarray(cos_t, F32), jnp.asarray(sin_t, F32)


def _pad_lanes(v, width=128):
    v = v.reshape(1, -1)
    return jnp.concatenate([v, jnp.zeros((1, width - v.shape[1]), v.dtype)], axis=1)


def kernel(x, norm_w, w_in, conv_w, conv_b, dt_bias, a_log, d_skip, ssm_norm_w, cmp_pos_k, cmp_pos_v, cmp_k_w1,
           cmp_k_b1, cmp_k_w2, cmp_v_w1, cmp_v_b1, cmp_v_w2, w_branch_ssm, w_branch_attn, w_out, final_norm_w):
    bsz, s, _ = x.shape
    depth = norm_w.shape[0]
    cos_t, sin_t = _rope_tables(s)
    x2 = x.reshape(bsz * s, D_MODEL)
    for layer in range(depth):
        slab = _proj(x2, norm_w[layer].reshape(1, -1), _slab_weights(w_in, layer))

        gs = _ssd(slab, bsz, s, conv_w[layer], conv_b[layer].reshape(1, -1), _pad_lanes(dt_bias[layer]),
                  _pad_lanes(a_log[layer]), jnp.repeat(d_skip[layer], SSM_HEAD_DIM).reshape(1, -1),
                  ssm_norm_w[layer].reshape(1, -1), w_branch_ssm[layer].astype(BF16))

        q_t, ks, vst, kw, vwt, hkv, gts = _prep(slab, bsz, s, cos_t, sin_t)

        pos = jnp.stack([cmp_pos_k[layer], cmp_pos_v[layer]]).reshape(2, 2, 1, CMP_STRIDE * HEAD_DIM)
        w1 = jnp.stack([cmp_k_w1[layer], cmp_v_w1[layer]])
        w1 = jnp.concatenate([w1[:, :CMP_STRIDE * HEAD_DIM], w1[:, CMP_STRIDE * HEAD_DIM:]], axis=2).astype(BF16)
        b1 = jnp.stack([cmp_k_b1[layer], cmp_v_b1[layer]]).reshape(2, 1, CMP_HIDDEN)
        w2 = jnp.stack([cmp_k_w2[layer], cmp_v_w2[layer]])
        w2 = jnp.concatenate([w2, jnp.zeros((2, CMP_HIDDEN, 128 - HEAD_DIM), w2.dtype)], axis=2).astype(BF16)
        c_nat, c_tr = _compress(hkv, pos, w1, b1, w2)

        o_t = _attn(q_t, c_nat, c_tr, ks, vst, kw, vwt, gts, s)

        x2 = _final(x2, gs, o_t, slab, w_branch_attn[layer].astype(BF16), w_out[layer].astype(BF16),
                    final_norm_w.reshape(1, -1), bsz, s, layer == depth - 1)
    return x2.reshape(bsz, s, D_MODEL)
```

```python
import functools
import math

import numpy as np
import jax
import jax.numpy as jnp
from jax import lax
from jax.experimental import pallas as pl
from jax.experimental.pallas import tpu as pltpu

F32 = jnp.float32
BF16 = jnp.bfloat16

EPS = 1e-6
ROPE_THETA = 10000.0
NEG = -1e30

D_MODEL = 1024
SSM_HEADS = 16
SSM_HEAD_DIM = 64
SSM_D_INNER = 1024
SSM_GROUPS = 4
SSM_STATE = 128
SSM_CONV = 4
SSM_CHUNK = 256
SSM_CONV_DIM = 2048
ATTN_HEADS = 16
ATTN_GROUPS = 4
HEAD_DIM = 64
ATTN_WIDTH = 1024
KV_WIDTH = 256
CMP_BLOCK = 32
CMP_STRIDE = 16
CMP_HIDDEN = 256
SLC_BLOCK = 64
SLC_TOP_N = 16
WINDOW = 512
Q_TILE = 128
KEY_TILE = 256
TOK_TILE = 256
HEADS_PER_GROUP = ATTN_HEADS // ATTN_GROUPS
QCOLS = HEADS_PER_GROUP * Q_TILE
SEL_UNROLL = 4
GROUPS_PER_STEP = 2
WIN_POS = 8
BF16_ROWS = 16
V_ROWS = HEAD_DIM + BF16_ROWS
LOG2E = math.log2(math.e)

SLAB_XBC = 0
SLAB_ZSSM = 2048
SLAB_Q = 3072
SLAB_ZATT = 4096
SLAB_GSSM = 5120
SLAB_GATT = 6144
SLAB_KV = 7168
SLAB_SMALL = 8704
SLAB_W = 8832
GATE_LANE0 = SSM_HEADS

V7X_VMEM_MIB = 64
MIB = 1024 * 1024


def _cparams(n_grid, vmem_mib):
    assert vmem_mib < V7X_VMEM_MIB
    return pltpu.CompilerParams(dimension_semantics=("arbitrary",) * n_grid,
                                vmem_limit_bytes=vmem_mib * MIB)


def _silu(x):
    h = 0.5 * x
    return h + h * jnp.tanh(h)


def _proj_kernel(x_ref, nw_ref, w_ref, o_ref):
    x = x_ref[...]
    ms = jnp.mean(x * x, axis=-1, keepdims=True)
    h = (x * lax.rsqrt(ms + EPS)) * nw_ref[...]
    o_ref[...] = lax.dot_general(h.astype(BF16), w_ref[...], (((1,), (1,)), ((), ())),
                                 preferred_element_type=F32)


def _proj(x2, norm_w, w_slab_t):
    t = x2.shape[0]
    tm, tn = 1024, SLAB_W // 3
    return pl.pallas_call(
        _proj_kernel,
        out_shape=jax.ShapeDtypeStruct((t, SLAB_W), F32),
        grid=(SLAB_W // tn, t // tm),
        in_specs=[pl.BlockSpec((tm, D_MODEL), lambda j, i: (i, 0)),
                  pl.BlockSpec((1, D_MODEL), lambda j, i: (0, 0)),
                  pl.BlockSpec((tn, D_MODEL), lambda j, i: (j, 0))],
        out_specs=pl.BlockSpec((tm, tn), lambda j, i: (i, j)),
        compiler_params=_cparams(2, 52),
        name="proj",
    )(x2, norm_w, w_slab_t)


def _split3(v):
    v1 = v.astype(BF16)
    r1 = v - v1.astype(F32)
    v2 = r1.astype(BF16)
    v3 = (r1 - v2.astype(F32)).astype(BF16)
    return v1, v2, v3


def _dot3(parts, rhs, lhs_side=True):
    out = None
    for p in parts:
        d = (jnp.dot(p, rhs, preferred_element_type=F32) if lhs_side
             else jnp.dot(rhs, p, preferred_element_type=F32))
        out = d if out is None else out + d
    return out


def _ssd_kernel(xbc_ref, prev_ref, small_ref, z_ref, g_ref, convw_ref, convb_ref, dtb_ref, alog_ref,
                dskip_ref, nw_ref, wb_ref, tri_ref, exp_ref, o_ref, st_ref, ext_ref):
    @pl.when(pl.program_id(0) == 0)
    def _():
        st_ref[...] = jnp.zeros_like(st_ref)

    for b in range(xbc_ref.shape[0]):
        _ssd_chunk(xbc_ref.at[b], prev_ref.at[b], small_ref.at[b], z_ref.at[b], g_ref.at[b], convw_ref, convb_ref,
                   dtb_ref, alog_ref, dskip_ref, nw_ref, wb_ref, tri_ref, exp_ref, o_ref.at[b], st_ref.at[b],
                   ext_ref.at[b])


def _ssd_chunk(xbc_ref, prev_ref, small_ref, z_ref, g_ref, convw_ref, convb_ref, dtb_ref, alog_ref,
               dskip_ref, nw_ref, wb_ref, tri_ref, exp_ref, o_ref, st_ref, ext_ref):
    c = pl.program_id(0)
    L = SSM_CHUNK

    prev = jnp.where(c > 0, prev_ref[...], 0.0)
    cols = []
    for j in range(SSM_CONV_DIM // 128):
        lanes = slice(j * 128, (j + 1) * 128)
        ext_ref[j, 0:8, :] = prev[:, lanes]
        ext_ref[j, 8:8 + L, :] = xbc_ref[:, lanes]
        acc = jnp.broadcast_to(convb_ref[:, lanes], (L, 128))
        for k in range(SSM_CONV):
            r0 = 8 - (SSM_CONV - 1) + k
            acc = acc + convw_ref[k:k + 1, lanes] * ext_ref[j, r0:r0 + L, :]
        cols.append(_silu(acc))
    xbc = jnp.concatenate(cols, axis=1)
    xs = xbc[:, :SSM_D_INNER]
    bm = xbc[:, SSM_D_INNER:SSM_D_INNER + SSM_GROUPS * SSM_STATE]
    cm = xbc[:, SSM_D_INNER + SSM_GROUPS * SSM_STATE:]

    lane = lax.broadcasted_iota(jnp.int32, (L, 128), 1)
    head_lane = lane < SSM_HEADS
    u = small_ref[...] + dtb_ref[...]
    dt = jnp.maximum(u, 0.0) + jnp.log1p(jnp.exp(-jnp.abs(u)))
    dt = jnp.where(head_lane, dt, 0.0)
    a = dt * -jnp.exp(alog_ref[...])
    cs = _dot3(_split3(a), tri_ref[...], lhs_side=False)
    cs_last = cs[L - 1:L, :]
    ecs = jnp.exp(cs)
    ds = jnp.exp(cs_last - cs)
    packed = (dt + pltpu.roll(jnp.where(head_lane, ecs, 0.0), SSM_HEADS, 1)
              + pltpu.roll(jnp.where(head_lane, ds, 0.0), 2 * SSM_HEADS, 1))
    expanded = _dot3(_split3(packed)[:2], exp_ref[...])
    dt_e = expanded[:, :SSM_D_INNER]
    ecs_e = expanded[:, SSM_D_INNER:2 * SSM_D_INNER]
    ds_e = expanded[:, 2 * SSM_D_INNER:]
    xdt = xs * dt_e
    xds = xdt * ds_e
    cs_t = cs.T

    row = lax.broadcasted_iota(jnp.int32, (L, L), 0)
    col = lax.broadcasted_iota(jnp.int32, (L, L), 1)
    tril = row >= col
    half = lax.broadcasted_iota(jnp.int32, (L, 128), 1) < SSM_HEAD_DIM
    gw = SSM_D_INNER // SSM_GROUPS

    y_groups = []
    for g in range(SSM_GROUPS):
        bg = bm[:, g * SSM_STATE:(g + 1) * SSM_STATE]
        cg = cm[:, g * SSM_STATE:(g + 1) * SSM_STATE].astype(BF16)
        bg_t = bg.T.astype(BF16)
        cb = jnp.dot(cg, bg_t, preferred_element_type=F32)
        st = st_ref[g]
        y_off = jnp.dot(cg, st.astype(BF16), preferred_element_type=F32) * ecs_e[:, g * gw:(g + 1) * gw]
        st_ref[g] = (ecs_e[L - 1:L, g * gw:(g + 1) * gw] * st
                     + jnp.dot(bg_t, xds[:, g * gw:(g + 1) * gw].astype(BF16), preferred_element_type=F32))
        pairs = []
        for pr in range(2):
            rhs = xdt[:, g * gw + pr * 128:g * gw + (pr + 1) * 128].astype(BF16)
            ys = []
            for hh in range(2):
                h = g * 4 + pr * 2 + hh
                seg = cs[:, h:h + 1] - cs_t[h:h + 1, :]
                dec = jnp.exp(jnp.where(tril, seg, NEG))
                ys.append(jnp.dot((cb * dec).astype(BF16), rhs, preferred_element_type=F32))
            pairs.append(jnp.where(half, ys[0], ys[1]))
        y_groups.append(jnp.concatenate(pairs, axis=1) + y_off)
    y = jnp.concatenate(y_groups, axis=1) + dskip_ref[...] * xs

    v = y * _silu(z_ref[...])
    ms = jnp.mean(v * v, axis=-1, keepdims=True)
    vn = (v * lax.rsqrt(ms + EPS)) * nw_ref[...]
    ys = jnp.dot(vn.astype(BF16), wb_ref[...], preferred_element_type=F32)
    o_ref[...] = (jax.nn.sigmoid(g_ref[...]) * ys).astype(o_ref.dtype)


def _ssd(slab, bsz, s, conv_w, conv_b, dtb, alog, dskip_e, nw, wb):
    nt = s // SSM_CHUNK
    L = SSM_CHUNK
    tri = jnp.asarray(np.tril(np.ones((L, L), np.float32)), BF16)
    e = np.zeros((128, 3 * SSM_D_INNER), np.float32)
    for k in range(3):
        for h in range(SSM_HEADS):
            e[k * SSM_HEADS + h, k * SSM_D_INNER + h * SSM_HEAD_DIM:k * SSM_D_INNER + (h + 1) * SSM_HEAD_DIM] = 1.0
    e = jnp.asarray(e, BF16)
    const = lambda shape: pl.BlockSpec(shape, lambda c: (0,) * len(shape))
    slab3 = slab.reshape(bsz, s, SLAB_W)
    out = pl.pallas_call(
        _ssd_kernel,
        out_shape=jax.ShapeDtypeStruct((bsz, s, SSM_D_INNER), BF16),
        grid=(nt,),
        in_specs=[
            pl.BlockSpec((bsz, L, SSM_CONV_DIM), lambda c: (0, c, SLAB_XBC // SSM_CONV_DIM)),
            pl.BlockSpec((bsz, 8, SSM_CONV_DIM), lambda c: (0, jnp.maximum(c * (L // 8) - 1, 0), 0)),
            pl.BlockSpec((bsz, L, 128), lambda c: (0, c, SLAB_SMALL // 128)),
            pl.BlockSpec((bsz, L, SSM_D_INNER), lambda c: (0, c, SLAB_ZSSM // SSM_D_INNER)),
            pl.BlockSpec((bsz, L, D_MODEL), lambda c: (0, c, SLAB_GSSM // D_MODEL)),
            const((SSM_CONV, SSM_CONV_DIM)), const((1, SSM_CONV_DIM)), const((1, 128)), const((1, 128)),
            const((1, SSM_D_INNER)), const((1, SSM_D_INNER)), const((SSM_D_INNER, D_MODEL)),
            const((L, L)), const((128, 3 * SSM_D_INNER)),
        ],
        out_specs=pl.BlockSpec((bsz, L, D_MODEL), lambda c: (0, c, 0)),
        scratch_shapes=[pltpu.VMEM((bsz, SSM_GROUPS, SSM_STATE, SSM_D_INNER // SSM_GROUPS), F32),
                        pltpu.VMEM((bsz, SSM_CONV_DIM // 128, L + 8, 128), F32)],
        compiler_params=_cparams(1, 48),
        name="ssd",
    )(slab3, slab3, slab3, slab3, slab3, conv_w, conv_b, dtb, alog, dskip_e, nw, wb, tri, e)
    return out.reshape(bsz * s, SSM_D_INNER)


def _prep_kernel(q_ref, kc_ref, vc_ref, ks_ref, vs_ref, kw_ref, vw_ref, small_ref, cos_ref, sin_ref,
                 qt_ref, kso_ref, vst_ref, kwo_ref, vwt_ref, hkv_ref, gt_ref, stage_ref):
    cos = cos_ref[...]
    sin = sin_ref[...]

    def rope(x):
        w = x.shape[1]
        cf = jnp.concatenate([cos] * (w // 128), axis=1)
        sf = jnp.concatenate([sin] * (w // 128), axis=1)
        lane = lax.broadcasted_iota(jnp.int32, x.shape, 1)
        first = (lane % HEAD_DIM) < (HEAD_DIM // 2)
        swapped = jnp.where(first, pltpu.roll(x, w - HEAD_DIM // 2, 1), pltpu.roll(x, HEAD_DIM // 2, 1))
        return x * cf + swapped * sf

    halves = TOK_TILE // Q_TILE
    lane128 = lax.broadcasted_iota(jnp.int32, (TOK_TILE, 128), 1)
    row128 = lax.broadcasted_iota(jnp.int32, (TOK_TILE, 128), 0)
    low64 = lane128 < HEAD_DIM
    blocks_per_tile = KEY_TILE // SLC_BLOCK
    tile_in_super = pl.program_id(1) & (SEL_UNROLL - 1)
    block_in_super = tile_in_super * blocks_per_tile + lax.shift_right_logical(row128, int(math.log2(SLC_BLOCK)))
    block_onehot = jnp.where(lane128 - HEAD_DIM == block_in_super, 1.0, 0.0)
    win_pos = ((pl.program_id(1) * halves + lax.shift_right_logical(row128, int(math.log2(Q_TILE))))
               & (WIN_POS - 1))
    win_onehot = jnp.where(lane128 - HEAD_DIM == win_pos, 1.0, 0.0)
    assert TOK_TILE == KEY_TILE and SEL_UNROLL * blocks_per_tile == BF16_ROWS

    def group_cols(x, g, rest):
        r = x if g == 0 else pltpu.roll(x, KV_WIDTH - HEAD_DIM * g, 1)
        return jnp.where(low64, r[:, :128], rest)

    def values_t(v_t, g, lo, hi):
        n = hi - lo
        extra = jnp.where(lax.broadcasted_iota(jnp.int32, (V_ROWS - HEAD_DIM, n), 0) == 0, 1.0, 0.0)
        return jnp.concatenate([v_t[g * HEAD_DIM:(g + 1) * HEAD_DIM, lo:hi], extra], axis=0).astype(BF16)

    qr = rope(q_ref[...]) * (HEAD_DIM ** -0.5 * LOG2E)
    for hf in range(halves):
        q_t = qr[hf * Q_TILE:(hf + 1) * Q_TILE, :].T
        for g in range(ATTN_GROUPS):
            base = g * HEADS_PER_GROUP * HEAD_DIM
            blk = jnp.concatenate([q_t[base + h * HEAD_DIM:base + (h + 1) * HEAD_DIM, :]
                                   for h in range(HEADS_PER_GROUP)], axis=1)
            qt_ref[0, g, hf] = blk.astype(BF16)

    ksr = rope(ks_ref[...])
    vs_t = vs_ref[...].T
    for g in range(ATTN_GROUPS):
        kso_ref[0, g, 0] = group_cols(ksr, g, block_onehot).astype(BF16)
        vst_ref[0, g, 0] = values_t(vs_t, g, 0, TOK_TILE)

    kwr = rope(kw_ref[...])
    vw_t = vw_ref[...].T
    for g in range(ATTN_GROUPS):
        kg = group_cols(kwr, g, win_onehot).astype(BF16)
        for hf in range(halves):
            kwo_ref[0, g, hf] = kg[hf * Q_TILE:(hf + 1) * Q_TILE, :]
            vwt_ref[0, g, hf] = values_t(vw_t, g, hf * Q_TILE, (hf + 1) * Q_TILE)

    per_half = CMP_STRIDE
    n_half = TOK_TILE // per_half
    lane_h = lax.broadcasted_iota(jnp.int32, (n_half, 128), 1) < HEAD_DIM
    for kv, x in enumerate((rope(kc_ref[...]), vc_ref[...])):
        for c in range(KV_WIDTH // 128):
            stage_ref[kv, c] = x[:, c * 128:(c + 1) * 128]
        for c in range(KV_WIDTH // 128):
            pieces = [stage_ref[kv, c, pl.ds(l, n_half, stride=per_half), :] for l in range(per_half)]
            swapped = [pltpu.roll(p, HEAD_DIM, 1) for p in pieces]
            for hi in range(128 // HEAD_DIM):
                tiles = []
                for l in range(0, per_half, 2):
                    if hi == 0:
                        tiles.append(jnp.where(lane_h, pieces[l], swapped[l + 1]))
                    else:
                        tiles.append(jnp.where(lane_h, swapped[l], pieces[l + 1]))
                hkv_ref[kv, 0, c * (128 // HEAD_DIM) + hi] = jnp.concatenate(tiles, axis=1)

    gates_t = jax.nn.sigmoid(small_ref[...]).T
    zeros_g = jnp.zeros((8 - 3, QCOLS), F32)
    for hf in range(halves):
        for g in range(ATTN_GROUPS):
            rows_c = []
            for cc in range(3):
                rows_c.append(jnp.concatenate(
                    [gates_t[GATE_LANE0 + (g * HEADS_PER_GROUP + r) * 3 + cc:GATE_LANE0 + (g * HEADS_PER_GROUP + r) * 3 + cc + 1,
                             hf * Q_TILE:(hf + 1) * Q_TILE] for r in range(HEADS_PER_GROUP)], axis=1))
            gt_ref[0, g, hf] = jnp.concatenate(rows_c + [zeros_g], axis=0)


def _prep(slab, bsz, s, cos_t, sin_t):
    nt = s // TOK_TILE
    nq = s // Q_TILE
    halves = TOK_TILE // Q_TILE
    G = ATTN_GROUPS
    rows = lambda b, i: b * nt + i
    kvspec = lambda k: pl.BlockSpec((TOK_TILE, KV_WIDTH), lambda b, i: (rows(b, i), SLAB_KV // KV_WIDTH + k))
    out5 = lambda blk: pl.BlockSpec(blk, lambda b, i: (b, 0, i, 0, 0))
    return pl.pallas_call(
        _prep_kernel,
        out_shape=[
            jax.ShapeDtypeStruct((bsz, G, nq, HEAD_DIM, QCOLS), BF16),
            jax.ShapeDtypeStruct((bsz, G, nt, KEY_TILE, 128), BF16),
            jax.ShapeDtypeStruct((bsz, G, nt, V_ROWS, KEY_TILE), BF16),
            jax.ShapeDtypeStruct((bsz, G, nq, Q_TILE, 128), BF16),
            jax.ShapeDtypeStruct((bsz, G, nq, V_ROWS, Q_TILE), BF16),
            jax.ShapeDtypeStruct((2, bsz, G, s // CMP_STRIDE, CMP_STRIDE * HEAD_DIM), F32),
            jax.ShapeDtypeStruct((bsz, G, nq, 8, QCOLS), F32),
        ],
        grid=(bsz, nt),
        in_specs=[
            pl.BlockSpec((TOK_TILE, ATTN_WIDTH), lambda b, i: (rows(b, i), SLAB_Q // ATTN_WIDTH)),
            kvspec(0), kvspec(1), kvspec(2), kvspec(3), kvspec(4), kvspec(5),
            pl.BlockSpec((TOK_TILE, 128), lambda b, i: (rows(b, i), SLAB_SMALL // 128)),
            pl.BlockSpec((TOK_TILE, 128), lambda b, i: (i, 0)),
            pl.BlockSpec((TOK_TILE, 128), lambda b, i: (i, 0)),
        ],
        out_specs=[
            out5((1, G, halves, HEAD_DIM, QCOLS)),
            out5((1, G, 1, KEY_TILE, 128)),
            out5((1, G, 1, V_ROWS, KEY_TILE)),
            out5((1, G, halves, Q_TILE, 128)),
            out5((1, G, halves, V_ROWS, Q_TILE)),
            pl.BlockSpec((2, 1, G, TOK_TILE // CMP_STRIDE, CMP_STRIDE * HEAD_DIM), lambda b, i: (0, b, 0, i, 0)),
            out5((1, G, halves, 8, QCOLS)),
        ],
        scratch_shapes=[pltpu.VMEM((2, KV_WIDTH // 128, TOK_TILE, 128), F32)],
        compiler_params=_cparams(2, 32),
        name="prep",
    )(slab, slab, slab, slab, slab, slab, slab, slab, cos_t, sin_t)


def _compress_kernel(h_ref, pos_ref, w1_ref, b1_ref, w2_ref, nat_ref, tr_ref):
    h = h_ref[0, 0, 0]
    nh = h.shape[0]
    w1 = w1_ref[0]
    a = jnp.dot((h + pos_ref[0, 0]).astype(BF16), w1[:, :CMP_HIDDEN], preferred_element_type=F32)
    b = jnp.dot((h + pos_ref[0, 1]).astype(BF16), w1[:, CMP_HIDDEN:], preferred_element_type=F32)
    hid = a + pltpu.roll(b, nh - 1, 0) + b1_ref[0]
    out = jnp.dot(jax.nn.gelu(hid).astype(BF16), w2_ref[0], preferred_element_type=F32)
    nat_ref[0, 0, 0] = out.astype(BF16)
    tr_ref[0, 0, 0] = out.T[:HEAD_DIM, :].astype(BF16)


def _compress(hkv, pos, w1, b1, w2):
    _, bsz, G, nh, _ = hkv.shape
    return pl.pallas_call(
        _compress_kernel,
        out_shape=[jax.ShapeDtypeStruct((2, bsz, G, nh, 128), BF16),
                   jax.ShapeDtypeStruct((2, bsz, G, HEAD_DIM, nh), BF16)],
        grid=(2, bsz, G),
        in_specs=[
            pl.BlockSpec((1, 1, 1, nh, CMP_STRIDE * HEAD_DIM), lambda k, b, g: (k, b, g, 0, 0)),
            pl.BlockSpec((1, 2, 1, CMP_STRIDE * HEAD_DIM), lambda k, b, g: (k, 0, 0, 0)),
            pl.BlockSpec((1, CMP_STRIDE * HEAD_DIM, 2 * CMP_HIDDEN), lambda k, b, g: (k, 0, 0)),
            pl.BlockSpec((1, 1, CMP_HIDDEN), lambda k, b, g: (k, 0, 0)),
            pl.BlockSpec((1, CMP_HIDDEN, 128), lambda k, b, g: (k, 0, 0)),
        ],
        out_specs=[pl.BlockSpec((1, 1, 1, nh, 128), lambda k, b, g: (k, b, g, 0, 0)),
                   pl.BlockSpec((1, 1, 1, HEAD_DIM, nh), lambda k, b, g: (k, b, g, 0, 0))],
        compiler_params=_cparams(3, 24),
        name="compress",
    )(hkv, pos, w1, b1, w2)


def _attn_kernel(qt_ref, qn_ref, kc_ref, vct_ref, ks_ref, vst_ref, kw_ref, vwt_ref, g_ref, tri_ref, cend_ref, o_ref, p4_ref,
                 imp_ref, oc_ref, bias_ref, biasm_ref, sbuf_a_ref, sbuf_b_ref, sd_ref, **static):
    for gg in range(GROUPS_PER_STEP):
        one = lambda ref, axis: ref.at[(slice(None),) * axis + (slice(gg, gg + 1),)]
        _attn_group(one(qt_ref, 1), one(qn_ref, 1), one(kc_ref, 2), one(vct_ref, 2), one(ks_ref, 1), one(vst_ref, 1),
                    one(kw_ref, 1), one(vwt_ref, 1), one(g_ref, 1), tri_ref, cend_ref, one(o_ref, 1), p4_ref, imp_ref,
                    oc_ref.at[gg], bias_ref.at[gg], biasm_ref.at[gg], sbuf_a_ref, sbuf_b_ref, sd_ref, **static)


def _attn_group(qt_ref, qn_ref, kc_ref, vct_ref, ks_ref, vst_ref, kw_ref, vwt_ref, g_ref, tri_ref, cend_ref, o_ref, p4_ref,
                imp_ref, oc_ref, bias_ref, biasm_ref, sbuf_a_ref, sbuf_b_ref, sd_ref, *, nh, ns, nst, nq, top_n):
    qi = pl.program_id(2)
    slot = qi & 1
    s0 = qi * Q_TILE
    q64 = qt_ref[0, 0, 0]
    lane_tok = lax.broadcasted_iota(jnp.int32, (1, QCOLS), 1) & (Q_TILE - 1)
    zero_half = jnp.zeros((128 - HEAD_DIM, QCOLS), BF16)
    assert KEY_TILE == 2 * Q_TILE

    def cmp_scores(q64x):
        return jnp.dot(kc_ref[0, 0, 0], jnp.concatenate([q64x, zero_half], axis=0),
                       preferred_element_type=F32)

    def importance(tile, sc, dst):
        t_row = tile * Q_TILE + lane_tok
        sm = jnp.where(cend_ref[...] <= tile * Q_TILE, sc, NEG)
        mc = jnp.max(sm, axis=0, keepdims=True)
        pc = jnp.exp2(sm - mc)
        lc = jnp.sum(pc, axis=0, keepdims=True)
        pn = pc * jnp.where(t_row >= CMP_BLOCK - 1, 1.0 / lc, 0.0)
        oc_ref[dst] = jnp.dot(vct_ref[0, 0, 0], pn.astype(BF16), preferred_element_type=F32)
        p4 = pn[:, 0:Q_TILE]
        for h in range(1, HEADS_PER_GROUP):
            p4 = p4 + pn[:, h * Q_TILE:(h + 1) * Q_TILE]
        p4_ref[0:8, :] = jnp.zeros((8, Q_TILE), F32)
        p4_ref[8:8 + nh, :] = p4
        imp = p4_ref[pl.ds(7, ns, stride=4), :]
        for cc in range(4):
            imp = imp + p4_ref[pl.ds(8 + cc, ns, stride=4), :]
        imp_ref[...] = imp

    rounds = top_n - 3
    blk = lax.broadcasted_iota(jnp.int32, (ns, Q_TILE), 0)

    def future_blocks(tile):
        return blk * SLC_BLOCK > tile * Q_TILE + lax.broadcasted_iota(jnp.int32, (ns, Q_TILE), 1)

    def pick_start(tile):
        tok = tile * Q_TILE + lax.broadcasted_iota(jnp.int32, (ns, Q_TILE), 1)
        cur = lax.shift_right_logical(tok, int(math.log2(SLC_BLOCK)))
        forced = (blk == 0) | (blk == cur) | (blk == cur - 1)
        return jnp.where(forced, -2.0, jnp.where(future_blocks(tile), -1.0, imp_ref[...]))

    def pick_round(val):
        best = jnp.max(val, axis=0, keepdims=True)
        first = jnp.min(jnp.where(val == best, blk, ns), axis=0, keepdims=True)
        return jnp.where(blk == first, -2.0, val)

    def write_tables(tile, val, dst):
        bias = jnp.where((val == -2.0) & jnp.logical_not(future_blocks(tile)), 0.0, NEG)
        full_blocks = lax.shift_right_logical(tile, 1) * (KEY_TILE // SLC_BLOCK)
        bias_main = jnp.where(blk < full_blocks, bias, NEG)
        for st in range(nst):
            for table, ref in ((bias, bias_ref), (bias_main, biasm_ref)):
                rows = table[st * BF16_ROWS:(st + 1) * BF16_ROWS, :]
                ref[dst, st] = jnp.concatenate([rows] * HEADS_PER_GROUP, axis=1).astype(BF16)

    def pick_blocks(tile, dst):
        val = pick_start(tile)
        for _ in range(rounds):
            val = pick_round(val)
        write_tables(tile, val, dst)

    @pl.when(qi == 0)
    def _():
        importance(0, cmp_scores(q64), 0)
        pick_blocks(0, 0)

    next_tile = jnp.minimum(qi + 1, nq - 1)
    sc_next = cmp_scores(qn_ref[0, 0, 0])

    zero_rows = jnp.zeros((128 - HEAD_DIM - BF16_ROWS, QCOLS), BF16)
    eye = jnp.where(lax.broadcasted_iota(jnp.int32, (Q_TILE, Q_TILE), 0)
                    == lax.broadcasted_iota(jnp.int32, (Q_TILE, Q_TILE), 1), 1.0, 0.0).astype(BF16)
    causal_tri = tri_ref[0]
    window_tri = tri_ref[1]

    def edge_scores(k_tile, rows16, tri):
        q_aug = jnp.concatenate([q64, rows16, zero_rows, tri], axis=0)
        return jnp.dot(jnp.concatenate([k_tile, eye], axis=1), q_aug, preferred_element_type=F32)

    n_back = WINDOW // Q_TILE
    row16 = lax.broadcasted_iota(jnp.int32, (BF16_ROWS, QCOLS), 0)
    back = (qi - row16) & (WIN_POS - 1)
    rows_w = jnp.where((row16 < WIN_POS) & (back <= n_back) & (qi - back < 0), NEG, 0.0).astype(BF16)
    win_tile = lambda d: jnp.where(qi - d < 0, qi - d + nq, qi - d)
    sw = jnp.concatenate(
        [edge_scores(kw_ref[0, 0, win_tile(0)], rows_w, causal_tri),
         jnp.dot(jnp.concatenate([kw_ref[0, 0, win_tile(d)] for d in range(1, n_back)], axis=0),
                 jnp.concatenate([q64, rows_w, zero_rows], axis=0), preferred_element_type=F32),
         edge_scores(kw_ref[0, 0, win_tile(n_back)], rows_w, window_tri)], axis=0)
    vw = jnp.concatenate([vwt_ref[0, 0, win_tile(d)] for d in range(n_back + 1)], axis=1)

    n_full = lax.shift_right_logical(qi, 1)

    def produce_tile(stc, j, dst_ref):
        q_aug = jnp.concatenate([q64, biasm_ref[slot, stc], zero_rows], axis=0)
        s = jnp.dot(ks_ref[0, 0, stc * SEL_UNROLL + j], q_aug, preferred_element_type=F32)
        dst_ref[j * KEY_TILE:(j + 1) * KEY_TILE, :] = s
        return jnp.max(s, axis=0, keepdims=True)

    def produce(st, dst_ref):
        stc = jnp.minimum(st, nst - 1)
        mx = produce_tile(stc, 0, dst_ref)
        for j in range(1, SEL_UNROLL):
            mx = jnp.maximum(mx, produce_tile(stc, j, dst_ref))
        return mx

    def consume_tile(kt, j, cur_ref, m_new):
        p = jnp.exp2(cur_ref[j * KEY_TILE:(j + 1) * KEY_TILE, :] - m_new).astype(BF16)
        return jnp.dot(vst_ref[0, 0, kt], p, preferred_element_type=F32)

    def step(i, carry, cur_ref, next_ref):
        m, acc, mx = carry
        m_new = jnp.maximum(m, mx)
        stc = jnp.minimum(i + 1, nst - 1)
        mx_next, pv = None, None
        for j in range(SEL_UNROLL):
            mxj = produce_tile(stc, j, next_ref)
            mx_next = mxj if mx_next is None else jnp.maximum(mx_next, mxj)
            pvj = consume_tile(i * SEL_UNROLL + j, j, cur_ref, m_new)
            pv = pvj if pv is None else pv + pvj
        return m_new, jnp.exp2(m - m_new) * acc + pv, mx_next

    def sel_body(i, carry):
        return lax.cond((i & 1) == 0,
                        lambda c: step(i, c, sbuf_a_ref, sbuf_b_ref),
                        lambda c: step(i, c, sbuf_b_ref, sbuf_a_ref), carry)

    n_iter = lax.shift_right_logical(n_full + (SEL_UNROLL - 1), int(math.log2(SEL_UNROLL)))
    last = jnp.maximum(n_iter - 1, 0)
    init = (jnp.full((1, QCOLS), NEG, F32), jnp.zeros((V_ROWS, QCOLS), F32), produce(0, sbuf_a_ref))

    odd = (qi & 1) == 1
    rows_d = bias_ref[slot, lax.shift_right_logical(n_full, int(math.log2(SEL_UNROLL)))]
    all_neg = jnp.full((Q_TILE, QCOLS), NEG, BF16)
    tri_a = jnp.where(odd, jnp.zeros((Q_TILE, QCOLS), BF16), causal_tri)
    tri_b = jnp.where(odd, causal_tri, all_neg)
    sd_a = edge_scores(ks_ref[0, 0, n_full, 0:Q_TILE, :], rows_d, tri_a)
    sd_b = edge_scores(ks_ref[0, 0, n_full, Q_TILE:KEY_TILE, :], rows_d, tri_b)
    sd_ref[0:Q_TILE, :] = sd_a
    sd_ref[Q_TILE:KEY_TILE, :] = sd_b
    mx_d = jnp.maximum(jnp.max(sd_a, axis=0, keepdims=True), jnp.max(sd_b, axis=0, keepdims=True))

    importance(next_tile, sc_next, 1 - slot)

    pw = jnp.exp2(sw - jnp.max(sw, axis=0, keepdims=True))
    acc_w = jnp.dot(vw, pw.astype(BF16), preferred_element_type=F32)
    o_w = acc_w[:HEAD_DIM, :] * (1.0 / acc_w[HEAD_DIM:HEAD_DIM + 1, :])

    m_s, acc_s, mx_s = lax.fori_loop(0, last, sel_body, init)

    def finish(cur_ref):
        m_new = jnp.maximum(jnp.maximum(m_s, mx_s), mx_d)
        p_diag = jnp.exp2(sd_ref[...] - m_new).astype(BF16)
        pv = jnp.dot(vst_ref[0, 0, n_full], p_diag, preferred_element_type=F32)
        for j in range(SEL_UNROLL):
            pv = pv + consume_tile(last * SEL_UNROLL + j, j, cur_ref, m_new)
        pick_blocks(next_tile, 1 - slot)
        return jnp.exp2(m_s - m_new) * acc_s + pv

    acc_s = lax.cond((last & 1) == 0, lambda: finish(sbuf_a_ref), lambda: finish(sbuf_b_ref))
    o_s = acc_s[:HEAD_DIM, :] * (1.0 / acc_s[HEAD_DIM:HEAD_DIM + 1, :])

    gts = g_ref[0, 0, 0]
    o_ref[0, 0, 0] = (gts[0:1, :] * oc_ref[slot] + gts[1:2, :] * o_s + gts[2:3, :] * o_w).astype(o_ref.dtype)


def _attn(q_t, kc_nat, vc_tr, ks, vst, kw, vwt, gts, s):
    bsz, G, nq = q_t.shape[:3]
    nh = s // CMP_STRIDE
    ns = s // SLC_BLOCK
    nkt = s // KEY_TILE
    nst = nkt // SEL_UNROLL
    assert s % (SEL_UNROLL * KEY_TILE) == 0
    top_n = min(SLC_TOP_N, ns)
    gps = GROUPS_PER_STEP
    assert G % gps == 0
    per_q = lambda blk: pl.BlockSpec(blk, lambda b, g, i: (b, g, i, 0, 0))
    per_bg = lambda blk: pl.BlockSpec(blk, lambda b, g, i: (b, g, 0, 0, 0))
    kk = np.arange(Q_TILE)[:, None]
    tl = np.tile(np.arange(Q_TILE), HEADS_PER_GROUP)[None, :]
    tri = jnp.asarray(np.stack([np.where(kk <= tl, 0.0, NEG), np.where(kk > tl, 0.0, NEG)]), BF16)
    cend = jnp.asarray(np.arange(nh)[:, None] * CMP_STRIDE + (CMP_BLOCK - 1) - tl, jnp.int32)
    return pl.pallas_call(
        functools.partial(_attn_kernel, nh=nh, ns=ns, nst=nst, nq=nq, top_n=top_n),
        out_shape=jax.ShapeDtypeStruct((bsz, G, nq, HEAD_DIM, QCOLS), BF16),
        grid=(bsz, G // gps, nq),
        in_specs=[
            per_q((1, gps, 1, HEAD_DIM, QCOLS)),
            pl.BlockSpec((1, gps, 1, HEAD_DIM, QCOLS), lambda b, g, i: (b, g, jnp.minimum(i + 1, nq - 1), 0, 0)),
            pl.BlockSpec((1, 1, gps, nh, 128), lambda b, g, i: (0, b, g, 0, 0)),
            pl.BlockSpec((1, 1, gps, HEAD_DIM, nh), lambda b, g, i: (1, b, g, 0, 0)),
            per_bg((1, gps, nkt, KEY_TILE, 128)),
            per_bg((1, gps, nkt, V_ROWS, KEY_TILE)),
            per_bg((1, gps, nq, Q_TILE, 128)),
            per_bg((1, gps, nq, V_ROWS, Q_TILE)),
            per_q((1, gps, 1, 8, QCOLS)),
            pl.BlockSpec((2, Q_TILE, QCOLS), lambda b, g, i: (0, 0, 0)),
            pl.BlockSpec((nh, QCOLS), lambda b, g, i: (0, 0)),
        ],
        out_specs=per_q((1, gps, 1, HEAD_DIM, QCOLS)),
        scratch_shapes=[pltpu.VMEM((nh + 8, Q_TILE), F32),
                        pltpu.VMEM((ns, Q_TILE), F32),
                        pltpu.VMEM((gps, 2, HEAD_DIM, QCOLS), F32),
                        pltpu.VMEM((gps, 2, nst, BF16_ROWS, QCOLS), BF16),
                        pltpu.VMEM((gps, 2, nst, BF16_ROWS, QCOLS), BF16),
                        pltpu.VMEM((SEL_UNROLL * KEY_TILE, QCOLS), F32),
                        pltpu.VMEM((SEL_UNROLL * KEY_TILE, QCOLS), F32),
                        pltpu.VMEM((KEY_TILE, QCOLS), F32)],
        compiler_params=_cparams(3, 52),
        name="attn",
    )(q_t, q_t, kc_nat, vc_tr, ks, vst, kw, vwt, gts, tri, cend)


def _final_kernel(x_ref, gs_ref, ot_ref, z_ref, g_ref, wba_ref, wout_ref, fnw_ref, o_ref, *, last):
    halves = TOK_TILE // Q_TILE
    parts = []
    for hf in range(halves):
        rows = []
        for g in range(ATTN_GROUPS):
            blk = ot_ref[0, g, hf].astype(F32)
            for h in range(HEADS_PER_GROUP):
                rows.append(blk[:, h * Q_TILE:(h + 1) * Q_TILE])
        parts.append(jnp.concatenate(rows, axis=0).T)
    o = jnp.concatenate(parts, axis=0)
    y_attn = jnp.dot((o * _silu(z_ref[...])).astype(BF16), wba_ref[...], preferred_element_type=F32)
    merged = gs_ref[...].astype(F32) + jax.nn.sigmoid(g_ref[...]) * y_attn
    xo = x_ref[...] + jnp.dot(merged.astype(BF16), wout_ref[...], preferred_element_type=F32)
    if last:
        ms = jnp.mean(xo * xo, axis=-1, keepdims=True)
        xo = (xo * lax.rsqrt(ms + EPS)) * fnw_ref[...]
    o_ref[...] = xo


def _final(x2, gs, o_t, slab, wba, wout, fnw, bsz, s, last):
    nt = s // TOK_TILE
    halves = TOK_TILE // Q_TILE
    rows = lambda b, i: b * nt + i
    const = lambda shape: pl.BlockSpec(shape, lambda b, i: (0,) * len(shape))
    return pl.pallas_call(
        functools.partial(_final_kernel, last=last),
        out_shape=jax.ShapeDtypeStruct((bsz * s, D_MODEL), F32),
        grid=(bsz, nt),
        in_specs=[
            pl.BlockSpec((TOK_TILE, D_MODEL), lambda b, i: (rows(b, i), 0)),
            pl.BlockSpec((TOK_TILE, D_MODEL), lambda b, i: (rows(b, i), 0)),
            pl.BlockSpec((1, ATTN_GROUPS, halves, HEAD_DIM, QCOLS), lambda b, i: (b, 0, i, 0, 0)),
            pl.BlockSpec((TOK_TILE, ATTN_WIDTH), lambda b, i: (rows(b, i), SLAB_ZATT // ATTN_WIDTH)),
            pl.BlockSpec((TOK_TILE, D_MODEL), lambda b, i: (rows(b, i), SLAB_GATT // D_MODEL)),
            const((ATTN_WIDTH, D_MODEL)), const((D_MODEL, D_MODEL)), const((1, D_MODEL)),
        ],
        out_specs=pl.BlockSpec((TOK_TILE, D_MODEL), lambda b, i: (rows(b, i), 0)),
        compiler_params=_cparams(2, 32),
        name="final",
    )(x2, gs, o_t, slab, slab, wba, wout, fnw)


_W_OFF = np.cumsum([0, SSM_D_INNER, SSM_CONV_DIM, SSM_HEADS, ATTN_WIDTH, 6 * KV_WIDTH, 3 * ATTN_HEADS,
                    ATTN_WIDTH, D_MODEL, D_MODEL]).tolist()


def _wslab_kernel(w_ref, o_ref):
    x = w_ref[...]
    seg = lambda i, j: x[_W_OFF[i]:_W_OFF[j], :]
    pad = jnp.zeros((128 - SSM_HEADS - 3 * ATTN_HEADS, x.shape[1]), x.dtype)
    o_ref[...] = jnp.concatenate(
        [seg(1, 2), seg(0, 1), seg(3, 4), seg(6, 9), seg(4, 5), seg(2, 3), seg(5, 6), pad], axis=0).astype(BF16)


def _slab_weights(w_all, layer):
    w_t = jnp.swapaxes(w_all, 1, 2)
    _, n, d = w_t.shape
    assert all(o % BF16_ROWS == 0 for o in _W_OFF)
    return pl.pallas_call(
        _wslab_kernel,
        out_shape=jax.ShapeDtypeStruct((SLAB_W, d), BF16),
        grid=(d // 128,),
        in_specs=[pl.BlockSpec((None, n, 128), lambda c: (layer, 0, c))],
        out_specs=pl.BlockSpec((SLAB_W, 128), lambda c: (0, c)),
        compiler_params=_cparams(1, 24),
        name="wslab",
    )(w_t)


def _rope_tables(s):
    half = HEAD_DIM // 2
    inv_freq = 1.0 / (ROPE_THETA ** (np.arange(half, dtype=np.float64) * 2.0 / HEAD_DIM))
    ang = np.arange(s, dtype=np.float64)[:, None] * inv_freq[None, :]
    cos, sin = np.cos(ang), np.sin(ang)
    cos_t = np.concatenate([cos, cos] * (128 // HEAD_DIM), axis=1)
    sin_t = np.concatenate([-sin, sin] * (128 // HEAD_DIM), axis=1)
    return jnp.asarray(cos_t, F32), jnp.asarray(sin_t, F32)


def _pad_lanes(v, width=128):
    v = v.reshape(1, -1)
    return jnp.concatenate([v, jnp.zeros((1, width - v.shape[1]), v.dtype)], axis=1)


def kernel(x, norm_w, w_in, conv_w, conv_b, dt_bias, a_log, d_skip, ssm_norm_w, cmp_pos_k, cmp_pos_v, cmp_k_w1,
           cmp_k_b1, cmp_k_w2, cmp_v_w1, cmp_v_b1, cmp_v_w2, w_branch_ssm, w_branch_attn, w_out, final_norm_w):
    bsz, s, _ = x.shape
    depth = norm_w.shape[0]
    cos_t, sin_t = _rope_tables(s)
    x2 = x.reshape(bsz * s, D_MODEL)
    for layer in range(depth):
        slab = _proj(x2, norm_w[layer].reshape(1, -1), _slab_weights(w_in, layer))

        gs = _ssd(slab, bsz, s, conv_w[layer], conv_b[layer].reshape(1, -1), _pad_lanes(dt_bias[layer]),
                  _pad_lanes(a_log[layer]), jnp.repeat(d_skip[layer], SSM_HEAD_DIM).reshape(1, -1),
                  ssm_norm_w[layer].reshape(1, -1), w_branch_ssm[layer].astype(BF16))

        q_t, ks, vst, kw, vwt, hkv, gts = _prep(slab, bsz, s, cos_t, sin_t)

        pos = jnp.stack([cmp_pos_k[layer], cmp_pos_v[layer]]).reshape(2, 2, 1, CMP_STRIDE * HEAD_DIM)
        w1 = jnp.stack([cmp_k_w1[layer], cmp_v_w1[layer]])
        w1 = jnp.concatenate([w1[:, :CMP_STRIDE * HEAD_DIM], w1[:, CMP_STRIDE * HEAD_DIM:]], axis=2).astype(BF16)
        b1 = jnp.stack([cmp_k_b1[layer], cmp_v_b1[layer]]).reshape(2, 1, CMP_HIDDEN)
        w2 = jnp.stack([cmp_k_w2[layer], cmp_v_w2[layer]])
        w2 = jnp.concatenate([w2, jnp.zeros((2, CMP_HIDDEN, 128 - HEAD_DIM), w2.dtype)], axis=2).astype(BF16)
        c_nat, c_tr = _compress(hkv, pos, w1, b1, w2)

        o_t = _attn(q_t, c_nat, c_tr, ks, vst, kw, vwt, gts, s)

        x2 = _final(x2, gs, o_t, slab, w_branch_attn[layer].astype(BF16), w_out[layer].astype(BF16),
                    final_norm_w.reshape(1, -1), bsz, s, layer == depth - 1)
    return x2.reshape(bsz, s, D_MODEL)
```

```python
import functools
import math

import numpy as np
import jax
import jax.numpy as jnp
from jax import lax
from jax.experimental import pallas as pl
from jax.experimental.pallas import tpu as pltpu

F32 = jnp.float32
BF16 = jnp.bfloat16

EPS = 1e-6
ROPE_THETA = 10000.0
NEG = -1e30

D_MODEL = 1024
SSM_HEADS = 16
SSM_HEAD_DIM = 64
SSM_D_INNER = 1024
SSM_GROUPS = 4
SSM_STATE = 128
SSM_CONV = 4
SSM_CHUNK = 256
SSM_CONV_DIM = 2048
ATTN_HEADS = 16
ATTN_GROUPS = 4
HEAD_DIM = 64
ATTN_WIDTH = 1024
KV_WIDTH = 256
CMP_BLOCK = 32
CMP_STRIDE = 16
CMP_HIDDEN = 256
SLC_BLOCK = 64
SLC_TOP_N = 16
WINDOW = 512
Q_TILE = 128
KEY_TILE = 256
TOK_TILE = 256
HEADS_PER_GROUP = ATTN_HEADS // ATTN_GROUPS
QCOLS = HEADS_PER_GROUP * Q_TILE
SEL_UNROLL = 4
GROUPS_PER_STEP = 2
WIN_POS = 8
BF16_ROWS = 16
V_ROWS = HEAD_DIM + BF16_ROWS
LOG2E = math.log2(math.e)

SLAB_XBC = 0
SLAB_ZSSM = 2048
SLAB_Q = 3072
SLAB_ZATT = 4096
SLAB_GSSM = 5120
SLAB_GATT = 6144
SLAB_KV = 7168
SLAB_SMALL = 8704
SLAB_W = 8832
GATE_LANE0 = SSM_HEADS

V7X_VMEM_MIB = 64
MIB = 1024 * 1024


def _cparams(n_grid, vmem_mib):
    assert vmem_mib < V7X_VMEM_MIB
    return pltpu.CompilerParams(dimension_semantics=("arbitrary",) * n_grid,
                                vmem_limit_bytes=vmem_mib * MIB)


def _silu(x):
    h = 0.5 * x
    return h + h * jnp.tanh(h)


def _proj_kernel(x_ref, nw_ref, w_ref, o_ref):
    x = x_ref[...]
    ms = jnp.mean(x * x, axis=-1, keepdims=True)
    h = (x * lax.rsqrt(ms + EPS)) * nw_ref[...]
    o_ref[...] = lax.dot_general(h.astype(BF16), w_ref[...], (((1,), (1,)), ((), ())),
                                 preferred_element_type=F32)


def _proj(x2, norm_w, w_slab_t):
    t = x2.shape[0]
    tm, tn = 1024, SLAB_W // 3
    return pl.pallas_call(
        _proj_kernel,
        out_shape=jax.ShapeDtypeStruct((t, SLAB_W), F32),
        grid=(SLAB_W // tn, t // tm),
        in_specs=[pl.BlockSpec((tm, D_MODEL), lambda j, i: (i, 0)),
                  pl.BlockSpec((1, D_MODEL), lambda j, i: (0, 0)),
                  pl.BlockSpec((tn, D_MODEL), lambda j, i: (j, 0))],
        out_specs=pl.BlockSpec((tm, tn), lambda j, i: (i, j)),
        compiler_params=_cparams(2, 52),
        name="proj",
    )(x2, norm_w, w_slab_t)


def _split3(v):
    v1 = v.astype(BF16)
    r1 = v - v1.astype(F32)
    v2 = r1.astype(BF16)
    v3 = (r1 - v2.astype(F32)).astype(BF16)
    return v1, v2, v3


def _dot3(parts, rhs, lhs_side=True):
    out = None
    for p in parts:
        d = (jnp.dot(p, rhs, preferred_element_type=F32) if lhs_side
             else jnp.dot(rhs, p, preferred_element_type=F32))
        out = d if out is None else out + d
    return out


def _ssd_kernel(xbc_ref, prev_ref, small_ref, z_ref, g_ref, convw_ref, convb_ref, dtb_ref, alog_ref,
                dskip_ref, nw_ref, wb_ref, tri_ref, exp_ref, o_ref, st_ref, ext_ref):
    @pl.when(pl.program_id(0) == 0)
    def _():
        st_ref[...] = jnp.zeros_like(st_ref)

    for b in range(xbc_ref.shape[0]):
        _ssd_chunk(xbc_ref.at[b], prev_ref.at[b], small_ref.at[b], z_ref.at[b], g_ref.at[b], convw_ref, convb_ref,
                   dtb_ref, alog_ref, dskip_ref, nw_ref, wb_ref, tri_ref, exp_ref, o_ref.at[b], st_ref.at[b],
                   ext_ref.at[b])


def _ssd_chunk(xbc_ref, prev_ref, small_ref, z_ref, g_ref, convw_ref, convb_ref, dtb_ref, alog_ref,
               dskip_ref, nw_ref, wb_ref, tri_ref, exp_ref, o_ref, st_ref, ext_ref):
    c = pl.program_id(0)
    L = SSM_CHUNK

    prev = jnp.where(c > 0, prev_ref[...], 0.0)
    cols = []
    for j in range(SSM_CONV_DIM // 128):
        lanes = slice(j * 128, (j + 1) * 128)
        ext_ref[j, 0:8, :] = prev[:, lanes]
        ext_ref[j, 8:8 + L, :] = xbc_ref[:, lanes]
        acc = jnp.broadcast_to(convb_ref[:, lanes], (L, 128))
        for k in range(SSM_CONV):
            r0 = 8 - (SSM_CONV - 1) + k
            acc = acc + convw_ref[k:k + 1, lanes] * ext_ref[j, r0:r0 + L, :]
        cols.append(_silu(acc))
    xbc = jnp.concatenate(cols, axis=1)
    xs = xbc[:, :SSM_D_INNER]
    bm = xbc[:, SSM_D_INNER:SSM_D_INNER + SSM_GROUPS * SSM_STATE]
    cm = xbc[:, SSM_D_INNER + SSM_GROUPS * SSM_STATE:]

    lane = lax.broadcasted_iota(jnp.int32, (L, 128), 1)
    head_lane = lane < SSM_HEADS
    u = small_ref[...] + dtb_ref[...]
    dt = jnp.maximum(u, 0.0) + jnp.log1p(jnp.exp(-jnp.abs(u)))
    dt = jnp.where(head_lane, dt, 0.0)
    a = dt * -jnp.exp(alog_ref[...])
    cs = _dot3(_split3(a), tri_ref[...], lhs_side=False)
    cs_last = cs[L - 1:L, :]
    ecs = jnp.exp(cs)
    ds = jnp.exp(cs_last - cs)
    packed = (dt + pltpu.roll(jnp.where(head_lane, ecs, 0.0), SSM_HEADS, 1)
              + pltpu.roll(jnp.where(head_lane, ds, 0.0), 2 * SSM_HEADS, 1))
    expanded = _dot3(_split3(packed)[:2], exp_ref[...])
    dt_e = expanded[:, :SSM_D_INNER]
    ecs_e = expanded[:, SSM_D_INNER:2 * SSM_D_INNER]
    ds_e = expanded[:, 2 * SSM_D_INNER:]
    xdt = xs * dt_e
    xds = xdt * ds_e
    cs_t = cs.T

    row = lax.broadcasted_iota(jnp.int32, (L, L), 0)
    col = lax.broadcasted_iota(jnp.int32, (L, L), 1)
    tril = row >= col
    half = lax.broadcasted_iota(jnp.int32, (L, 128), 1) < SSM_HEAD_DIM
    gw = SSM_D_INNER // SSM_GROUPS

    y_groups = []
    for g in range(SSM_GROUPS):
        bg = bm[:, g * SSM_STATE:(g + 1) * SSM_STATE]
        cg = cm[:, g * SSM_STATE:(g + 1) * SSM_STATE].astype(BF16)
        bg_t = bg.T.astype(BF16)
        cb = jnp.dot(cg, bg_t, preferred_element_type=F32)
        st = st_ref[g]
        y_off = jnp.dot(cg, st.astype(BF16), preferred_element_type=F32) * ecs_e[:, g * gw:(g + 1) * gw]
        st_ref[g] = (ecs_e[L - 1:L, g * gw:(g + 1) * gw] * st
                     + jnp.dot(bg_t, xds[:, g * gw:(g + 1) * gw].astype(BF16), preferred_element_type=F32))
        pairs = []
        for pr in range(2):
            rhs = xdt[:, g * gw + pr * 128:g * gw + (pr + 1) * 128].astype(BF16)
            ys = []
            for hh in range(2):
                h = g * 4 + pr * 2 + hh
                seg = cs[:, h:h + 1] - cs_t[h:h + 1, :]
                dec = jnp.exp(jnp.where(tril, seg, NEG))
                ys.append(jnp.dot((cb * dec).astype(BF16), rhs, preferred_element_type=F32))
            pairs.append(jnp.where(half, ys[0], ys[1]))
        y_groups.append(jnp.concatenate(pairs, axis=1) + y_off)
    y = jnp.concatenate(y_groups, axis=1) + dskip_ref[...] * xs

    v = y * _silu(z_ref[...])
    ms = jnp.mean(v * v, axis=-1, keepdims=True)
    vn = (v * lax.rsqrt(ms + EPS)) * nw_ref[...]
    ys = jnp.dot(vn.astype(BF16), wb_ref[...], preferred_element_type=F32)
    o_ref[...] = (jax.nn.sigmoid(g_ref[...]) * ys).astype(o_ref.dtype)


def _ssd(slab, bsz, s, conv_w, conv_b, dtb, alog, dskip_e, nw, wb):
    nt = s // SSM_CHUNK
    L = SSM_CHUNK
    tri = jnp.asarray(np.tril(np.ones((L, L), np.float32)), BF16)
    e = np.zeros((128, 3 * SSM_D_INNER), np.float32)
    for k in range(3):
        for h in range(SSM_HEADS):
            e[k * SSM_HEADS + h, k * SSM_D_INNER + h * SSM_HEAD_DIM:k * SSM_D_INNER + (h + 1) * SSM_HEAD_DIM] = 1.0
    e = jnp.asarray(e, BF16)
    const = lambda shape: pl.BlockSpec(shape, lambda c: (0,) * len(shape))
    slab3 = slab.reshape(bsz, s, SLAB_W)
    out = pl.pallas_call(
        _ssd_kernel,
        out_shape=jax.ShapeDtypeStruct((bsz, s, SSM_D_INNER), BF16),
        grid=(nt,),
        in_specs=[
            pl.BlockSpec((bsz, L, SSM_CONV_DIM), lambda c: (0, c, SLAB_XBC // SSM_CONV_DIM)),
            pl.BlockSpec((bsz, 8, SSM_CONV_DIM), lambda c: (0, jnp.maximum(c * (L // 8) - 1, 0), 0)),
            pl.BlockSpec((bsz, L, 128), lambda c: (0, c, SLAB_SMALL // 128)),
            pl.BlockSpec((bsz, L, SSM_D_INNER), lambda c: (0, c, SLAB_ZSSM // SSM_D_INNER)),
            pl.BlockSpec((bsz, L, D_MODEL), lambda c: (0, c, SLAB_GSSM // D_MODEL)),
            const((SSM_CONV, SSM_CONV_DIM)), const((1, SSM_CONV_DIM)), const((1, 128)), const((1, 128)),
            const((1, SSM_D_INNER)), const((1, SSM_D_INNER)), const((SSM_D_INNER, D_MODEL)),
            const((L, L)), const((128, 3 * SSM_D_INNER)),
        ],
        out_specs=pl.BlockSpec((bsz, L, D_MODEL), lambda c: (0, c, 0)),
        scratch_shapes=[pltpu.VMEM((bsz, SSM_GROUPS, SSM_STATE, SSM_D_INNER // SSM_GROUPS), F32),
                        pltpu.VMEM((bsz, SSM_CONV_DIM // 128, L + 8, 128), F32)],
        compiler_params=_cparams(1, 48),
        name="ssd",
    )(slab3, slab3, slab3, slab3, slab3, conv_w, conv_b, dtb, alog, dskip_e, nw, wb, tri, e)
    return out.reshape(bsz * s, SSM_D_INNER)


def _prep_kernel(q_ref, kc_ref, vc_ref, ks_ref, vs_ref, kw_ref, vw_ref, small_ref, cos_ref, sin_ref,
                 qt_ref, kso_ref, vst_ref, kwo_ref, vwt_ref, hkv_ref, gt_ref, stage_ref):
    cos = cos_ref[...]
    sin = sin_ref[...]

    def rope(x):
        w = x.shape[1]
        cf = jnp.concatenate([cos] * (w // 128), axis=1)
        sf = jnp.concatenate([sin] * (w // 128), axis=1)
        lane = lax.broadcasted_iota(jnp.int32, x.shape, 1)
        first = (lane % HEAD_DIM) < (HEAD_DIM // 2)
        swapped = jnp.where(first, pltpu.roll(x, w - HEAD_DIM // 2, 1), pltpu.roll(x, HEAD_DIM // 2, 1))
        return x * cf + swapped * sf

    halves = TOK_TILE // Q_TILE
    lane128 = lax.broadcasted_iota(jnp.int32, (TOK_TILE, 128), 1)
    row128 = lax.broadcasted_iota(jnp.int32, (TOK_TILE, 128), 0)
    low64 = lane128 < HEAD_DIM
    blocks_per_tile = KEY_TILE // SLC_BLOCK
    tile_in_super = pl.program_id(1) & (SEL_UNROLL - 1)
    block_in_super = tile_in_super * blocks_per_tile + lax.shift_right_logical(row128, int(math.log2(SLC_BLOCK)))
    block_onehot = jnp.where(lane128 - HEAD_DIM == block_in_super, 1.0, 0.0)
    win_pos = ((pl.program_id(1) * halves + lax.shift_right_logical(row128, int(math.log2(Q_TILE))))
               & (WIN_POS - 1))
    win_onehot = jnp.where(lane128 - HEAD_DIM == win_pos, 1.0, 0.0)
    assert TOK_TILE == KEY_TILE and SEL_UNROLL * blocks_per_tile == BF16_ROWS

    def group_cols(x, g, rest):
        r = x if g == 0 else pltpu.roll(x, KV_WIDTH - HEAD_DIM * g, 1)
        return jnp.where(low64, r[:, :128], rest)

    def values_t(v_t, g, lo, hi):
        n = hi - lo
        extra = jnp.where(lax.broadcasted_iota(jnp.int32, (V_ROWS - HEAD_DIM, n), 0) == 0, 1.0, 0.0)
        return jnp.concatenate([v_t[g * HEAD_DIM:(g + 1) * HEAD_DIM, lo:hi], extra], axis=0).astype(BF16)

    qr = rope(q_ref[...]) * (HEAD_DIM ** -0.5 * LOG2E)
    for hf in range(halves):
        q_t = qr[hf * Q_TILE:(hf + 1) * Q_TILE, :].T
        for g in range(ATTN_GROUPS):
            base = g * HEADS_PER_GROUP * HEAD_DIM
            blk = jnp.concatenate([q_t[base + h * HEAD_DIM:base + (h + 1) * HEAD_DIM, :]
                                   for h in range(HEADS_PER_GROUP)], axis=1)
            qt_ref[0, g, hf] = blk.astype(BF16)

    ksr = rope(ks_ref[...])
    vs_t = vs_ref[...].T
    for g in range(ATTN_GROUPS):
        kso_ref[0, g, 0] = group_cols(ksr, g, block_onehot).astype(BF16)
        vst_ref[0, g, 0] = values_t(vs_t, g, 0, TOK_TILE)

    kwr = rope(kw_ref[...])
    vw_t = vw_ref[...].T
    for g in range(ATTN_GROUPS):
        kg = group_cols(kwr, g, win_onehot).astype(BF16)
        for hf in range(halves):
            kwo_ref[0, g, hf] = kg[hf * Q_TILE:(hf + 1) * Q_TILE, :]
            vwt_ref[0, g, hf] = values_t(vw_t, g, hf * Q_TILE, (hf + 1) * Q_TILE)

    per_half = CMP_STRIDE
    n_half = TOK_TILE // per_half
    lane_h = lax.broadcasted_iota(jnp.int32, (n_half, 128), 1) < HEAD_DIM
    for kv, x in enumerate((rope(kc_ref[...]), vc_ref[...])):
        for c in range(KV_WIDTH // 128):
            stage_ref[kv, c] = x[:, c * 128:(c + 1) * 128]
        for c in range(KV_WIDTH // 128):
            pieces = [stage_ref[kv, c, pl.ds(l, n_half, stride=per_half), :] for l in range(per_half)]
            swapped = [pltpu.roll(p, HEAD_DIM, 1) for p in pieces]
            for hi in range(128 // HEAD_DIM):
                tiles = []
                for l in range(0, per_half, 2):
                    if hi == 0:
                        tiles.append(jnp.where(lane_h, pieces[l], swapped[l + 1]))
                    else:
                        tiles.append(jnp.where(lane_h, swapped[l], pieces[l + 1]))
                hkv_ref[kv, 0, c * (128 // HEAD_DIM) + hi] = jnp.concatenate(tiles, axis=1)

    gates_t = jax.nn.sigmoid(small_ref[...]).T
    zeros_g = jnp.zeros((8 - 3, QCOLS), F32)
    for hf in range(halves):
        for g in range(ATTN_GROUPS):
            rows_c = []
            for cc in range(3):
                rows_c.append(jnp.concatenate(
                    [gates_t[GATE_LANE0 + (g * HEADS_PER_GROUP + r) * 3 + cc:GATE_LANE0 + (g * HEADS_PER_GROUP + r) * 3 + cc + 1,
                             hf * Q_TILE:(hf + 1) * Q_TILE] for r in range(HEADS_PER_GROUP)], axis=1))
            gt_ref[0, g, hf] = jnp.concatenate(rows_c + [zeros_g], axis=0)


def _prep(slab, bsz, s, cos_t, sin_t):
    nt = s // TOK_TILE
    nq = s // Q_TILE
    halves = TOK_TILE // Q_TILE
    G = ATTN_GROUPS
    rows = lambda b, i: b * nt + i
    kvspec = lambda k: pl.BlockSpec((TOK_TILE, KV_WIDTH), lambda b, i: (rows(b, i), SLAB_KV // KV_WIDTH + k))
    out5 = lambda blk: pl.BlockSpec(blk, lambda b, i: (b, 0, i, 0, 0))
    return pl.pallas_call(
        _prep_kernel,
        out_shape=[
            jax.ShapeDtypeStruct((bsz, G, nq, HEAD_DIM, QCOLS), BF16),
            jax.ShapeDtypeStruct((bsz, G, nt, KEY_TILE, 128), BF16),
            jax.ShapeDtypeStruct((bsz, G, nt, V_ROWS, KEY_TILE), BF16),
            jax.ShapeDtypeStruct((bsz, G, nq, Q_TILE, 128), BF16),
            jax.ShapeDtypeStruct((bsz, G, nq, V_ROWS, Q_TILE), BF16),
            jax.ShapeDtypeStruct((2, bsz, G, s // CMP_STRIDE, CMP_STRIDE * HEAD_DIM), F32),
            jax.ShapeDtypeStruct((bsz, G, nq, 8, QCOLS), F32),
        ],
        grid=(bsz, nt),
        in_specs=[
            pl.BlockSpec((TOK_TILE, ATTN_WIDTH), lambda b, i: (rows(b, i), SLAB_Q // ATTN_WIDTH)),
            kvspec(0), kvspec(1), kvspec(2), kvspec(3), kvspec(4), kvspec(5),
            pl.BlockSpec((TOK_TILE, 128), lambda b, i: (rows(b, i), SLAB_SMALL // 128)),
            pl.BlockSpec((TOK_TILE, 128), lambda b, i: (i, 0)),
            pl.BlockSpec((TOK_TILE, 128), lambda b, i: (i, 0)),
        ],
        out_specs=[
            out5((1, G, halves, HEAD_DIM, QCOLS)),
            out5((1, G, 1, KEY_TILE, 128)),
            out5((1, G, 1, V_ROWS, KEY_TILE)),
            out5((1, G, halves, Q_TILE, 128)),
            out5((1, G, halves, V_ROWS, Q_TILE)),
            pl.BlockSpec((2, 1, G, TOK_TILE // CMP_STRIDE, CMP_STRIDE * HEAD_DIM), lambda b, i: (0, b, 0, i, 0)),
            out5((1, G, halves, 8, QCOLS)),
        ],
        scratch_shapes=[pltpu.VMEM((2, KV_WIDTH // 128, TOK_TILE, 128), F32)],
        compiler_params=_cparams(2, 32),
        name="prep",
    )(slab, slab, slab, slab, slab, slab, slab, slab, cos_t, sin_t)


def _compress_kernel(h_ref, pos_ref, w1_ref, b1_ref, w2_ref, nat_ref, tr_ref):
    h = h_ref[0, 0, 0]
    nh = h.shape[0]
    w1 = w1_ref[0]
    a = jnp.dot((h + pos_ref[0, 0]).astype(BF16), w1[:, :CMP_HIDDEN], preferred_element_type=F32)
    b = jnp.dot((h + pos_ref[0, 1]).astype(BF16), w1[:, CMP_HIDDEN:], preferred_element_type=F32)
    hid = a + pltpu.roll(b, nh - 1, 0) + b1_ref[0]
    out = jnp.dot(jax.nn.gelu(hid).astype(BF16), w2_ref[0], preferred_element_type=F32)
    nat_ref[0, 0, 0] = out.astype(BF16)
    tr_ref[0, 0, 0] = out.T[:HEAD_DIM, :].astype(BF16)


def _compress(hkv, pos, w1, b1, w2):
    _, bsz, G, nh, _ = hkv.shape
    return pl.pallas_call(
        _compress_kernel,
        out_shape=[jax.ShapeDtypeStruct((2, bsz, G, nh, 128), BF16),
                   jax.ShapeDtypeStruct((2, bsz, G, HEAD_DIM, nh), BF16)],
        grid=(2, bsz, G),
        in_specs=[
            pl.BlockSpec((1, 1, 1, nh, CMP_STRIDE * HEAD_DIM), lambda k, b, g: (k, b, g, 0, 0)),
            pl.BlockSpec((1, 2, 1, CMP_STRIDE * HEAD_DIM), lambda k, b, g: (k, 0, 0, 0)),
            pl.BlockSpec((1, CMP_STRIDE * HEAD_DIM, 2 * CMP_HIDDEN), lambda k, b, g: (k, 0, 0)),
            pl.BlockSpec((1, 1, CMP_HIDDEN), lambda k, b, g: (k, 0, 0)),
            pl.BlockSpec((1, CMP_HIDDEN, 128), lambda k, b, g: (k, 0, 0)),
        ],
        out_specs=[pl.BlockSpec((1, 1, 1, nh, 128), lambda k, b, g: (k, b, g, 0, 0)),
                   pl.BlockSpec((1, 1, 1, HEAD_DIM, nh), lambda k, b, g: (k, b, g, 0, 0))],
        compiler_params=_cparams(3, 24),
        name="compress",
    )(hkv, pos, w1, b1, w2)


def _attn_kernel(qt_ref, qn_ref, kc_ref, vct_ref, ks_ref, vst_ref, kw_ref, vwt_ref, g_ref, tri_ref, cend_ref, o_ref, p4_ref,
                 imp_ref, oc_ref, bias_ref, biasm_ref, sbuf_a_ref, sbuf_b_ref, sd_ref, **static):
    groups = []
    for gg in range(GROUPS_PER_STEP):
        one = lambda ref, axis: ref.at[(slice(None),) * axis + (slice(gg, gg + 1),)]
        groups.append(_attn_group(
            one(qt_ref, 1), one(qn_ref, 1), one(kc_ref, 2), one(vct_ref, 2), one(ks_ref, 1), one(vst_ref, 1),
            one(kw_ref, 1), one(vwt_ref, 1), one(g_ref, 1), tri_ref, cend_ref, one(o_ref, 1), p4_ref.at[gg],
            imp_ref.at[gg], oc_ref.at[gg], bias_ref.at[gg], biasm_ref.at[gg], sbuf_a_ref.at[gg], sbuf_b_ref.at[gg],
            sd_ref.at[gg], **static))
    for _ in range(3):
        for group in groups:
            next(group, None)


def _attn_group(qt_ref, qn_ref, kc_ref, vct_ref, ks_ref, vst_ref, kw_ref, vwt_ref, g_ref, tri_ref, cend_ref, o_ref, p4_ref,
                imp_ref, oc_ref, bias_ref, biasm_ref, sbuf_a_ref, sbuf_b_ref, sd_ref, *, nh, ns, nst, nq, top_n):
    qi = pl.program_id(2)
    slot = qi & 1
    q64 = qt_ref[0, 0, 0]
    lane_tok = lax.broadcasted_iota(jnp.int32, (1, QCOLS), 1) & (Q_TILE - 1)
    zero_half = jnp.zeros((128 - HEAD_DIM, QCOLS), BF16)
    assert KEY_TILE == 2 * Q_TILE

    def cmp_scores(q64x):
        return jnp.dot(kc_ref[0, 0, 0], jnp.concatenate([q64x, zero_half], axis=0),
                       preferred_element_type=F32)

    def importance(tile, sc, dst):
        t_row = tile * Q_TILE + lane_tok
        sm = jnp.where(cend_ref[...] <= tile * Q_TILE, sc, NEG)
        mc = jnp.max(sm, axis=0, keepdims=True)
        pc = jnp.exp2(sm - mc)
        lc = jnp.sum(pc, axis=0, keepdims=True)
        pn = pc * jnp.where(t_row >= CMP_BLOCK - 1, 1.0 / lc, 0.0)
        oc_ref[dst] = jnp.dot(vct_ref[0, 0, 0], pn.astype(BF16), preferred_element_type=F32)
        p4 = pn[:, 0:Q_TILE]
        for h in range(1, HEADS_PER_GROUP):
            p4 = p4 + pn[:, h * Q_TILE:(h + 1) * Q_TILE]
        p4_ref[0:8, :] = jnp.zeros((8, Q_TILE), F32)
        p4_ref[8:8 + nh, :] = p4
        imp = p4_ref[pl.ds(7, ns, stride=4), :]
        for cc in range(4):
            imp = imp + p4_ref[pl.ds(8 + cc, ns, stride=4), :]
        imp_ref[...] = imp

    rounds = top_n - 3
    blk = lax.broadcasted_iota(jnp.int32, (ns, Q_TILE), 0)

    def future_blocks(tile):
        return blk * SLC_BLOCK > tile * Q_TILE + lax.broadcasted_iota(jnp.int32, (ns, Q_TILE), 1)

    def pick_start(tile):
        tok = tile * Q_TILE + lax.broadcasted_iota(jnp.int32, (ns, Q_TILE), 1)
        cur = lax.shift_right_logical(tok, int(math.log2(SLC_BLOCK)))
        forced = (blk == 0) | (blk == cur) | (blk == cur - 1)
        return jnp.where(forced, -2.0, jnp.where(future_blocks(tile), -1.0, imp_ref[...]))

    def pick_round(val):
        best = jnp.max(val, axis=0, keepdims=True)
        first = jnp.min(jnp.where(val == best, blk, ns), axis=0, keepdims=True)
        return jnp.where(blk == first, -2.0, val)

    def write_tables(tile, val, dst):
        bias = jnp.where((val == -2.0) & jnp.logical_not(future_blocks(tile)), 0.0, NEG)
        full_blocks = lax.shift_right_logical(tile, 1) * (KEY_TILE // SLC_BLOCK)
        bias_main = jnp.where(blk < full_blocks, bias, NEG)
        for st in range(nst):
            for table, ref in ((bias, bias_ref), (bias_main, biasm_ref)):
                rows = table[st * BF16_ROWS:(st + 1) * BF16_ROWS, :]
                ref[dst, st] = jnp.concatenate([rows] * HEADS_PER_GROUP, axis=1).astype(BF16)

    def pick_blocks(tile, dst):
        val = pick_start(tile)
        for _ in range(rounds):
            val = pick_round(val)
        write_tables(tile, val, dst)

    @pl.when(qi == 0)
    def _():
        importance(0, cmp_scores(q64), 0)
        pick_blocks(0, 0)

    yield

    next_tile = jnp.minimum(qi + 1, nq - 1)
    sc_next = cmp_scores(qn_ref[0, 0, 0])

    zero_rows = jnp.zeros((128 - HEAD_DIM - BF16_ROWS, QCOLS), BF16)
    eye = jnp.where(lax.broadcasted_iota(jnp.int32, (Q_TILE, Q_TILE), 0)
                    == lax.broadcasted_iota(jnp.int32, (Q_TILE, Q_TILE), 1), 1.0, 0.0).astype(BF16)
    causal_tri = tri_ref[0]
    window_tri = tri_ref[1]

    def edge_scores(k_tile, rows16, tri):
        q_aug = jnp.concatenate([q64, rows16, zero_rows, tri], axis=0)
        return jnp.dot(jnp.concatenate([k_tile, eye], axis=1), q_aug, preferred_element_type=F32)

    n_back = WINDOW // Q_TILE
    row16 = lax.broadcasted_iota(jnp.int32, (BF16_ROWS, QCOLS), 0)
    back = (qi - row16) & (WIN_POS - 1)
    rows_w = jnp.where((row16 < WIN_POS) & (back <= n_back) & (qi - back < 0), NEG, 0.0).astype(BF16)
    win_tile = lambda d: jnp.where(qi - d < 0, qi - d + nq, qi - d)
    sw = jnp.concatenate(
        [edge_scores(kw_ref[0, 0, win_tile(0)], rows_w, causal_tri),
         jnp.dot(jnp.concatenate([kw_ref[0, 0, win_tile(d)] for d in range(1, n_back)], axis=0),
                 jnp.concatenate([q64, rows_w, zero_rows], axis=0), preferred_element_type=F32),
         edge_scores(kw_ref[0, 0, win_tile(n_back)], rows_w, window_tri)], axis=0)
    vw = jnp.concatenate([vwt_ref[0, 0, win_tile(d)] for d in range(n_back + 1)], axis=1)

    n_full = lax.shift_right_logical(qi, 1)

    def produce_tile(stc, j, dst_ref):
        q_aug = jnp.concatenate([q64, biasm_ref[slot, stc], zero_rows], axis=0)
        s = jnp.dot(ks_ref[0, 0, stc * SEL_UNROLL + j], q_aug, preferred_element_type=F32)
        dst_ref[j * KEY_TILE:(j + 1) * KEY_TILE, :] = s
        return jnp.max(s, axis=0, keepdims=True)

    def produce(st, dst_ref):
        stc = jnp.minimum(st, nst - 1)
        mx = produce_tile(stc, 0, dst_ref)
        for j in range(1, SEL_UNROLL):
            mx = jnp.maximum(mx, produce_tile(stc, j, dst_ref))
        return mx

    def consume_tile(kt, j, cur_ref, m_new):
        p = jnp.exp2(cur_ref[j * KEY_TILE:(j + 1) * KEY_TILE, :] - m_new).astype(BF16)
        return jnp.dot(vst_ref[0, 0, kt], p, preferred_element_type=F32)

    def step(i, carry, cur_ref, next_ref):
        m, acc, mx = carry
        m_new = jnp.maximum(m, mx)
        stc = jnp.minimum(i + 1, nst - 1)
        mx_next, pv = None, None
        for j in range(SEL_UNROLL):
            mxj = produce_tile(stc, j, next_ref)
            mx_next = mxj if mx_next is None else jnp.maximum(mx_next, mxj)
            pvj = consume_tile(i * SEL_UNROLL + j, j, cur_ref, m_new)
            pv = pvj if pv is None else pv + pvj
        return m_new, jnp.exp2(m - m_new) * acc + pv, mx_next

    def sel_body(i, carry):
        return lax.cond((i & 1) == 0,
                        lambda c: step(i, c, sbuf_a_ref, sbuf_b_ref),
                        lambda c: step(i, c, sbuf_b_ref, sbuf_a_ref), carry)

    n_iter = lax.shift_right_logical(n_full + (SEL_UNROLL - 1), int(math.log2(SEL_UNROLL)))
    last = jnp.maximum(n_iter - 1, 0)
    init = (jnp.full((1, QCOLS), NEG, F32), jnp.zeros((V_ROWS, QCOLS), F32), produce(0, sbuf_a_ref))

    odd = (qi & 1) == 1
    rows_d = bias_ref[slot, lax.shift_right_logical(n_full, int(math.log2(SEL_UNROLL)))]
    all_neg = jnp.full((Q_TILE, QCOLS), NEG, BF16)
    tri_a = jnp.where(odd, jnp.zeros((Q_TILE, QCOLS), BF16), causal_tri)
    tri_b = jnp.where(odd, causal_tri, all_neg)
    sd_a = edge_scores(ks_ref[0, 0, n_full, 0:Q_TILE, :], rows_d, tri_a)
    sd_b = edge_scores(ks_ref[0, 0, n_full, Q_TILE:KEY_TILE, :], rows_d, tri_b)
    sd_ref[0:Q_TILE, :] = sd_a
    sd_ref[Q_TILE:KEY_TILE, :] = sd_b
    mx_d = jnp.maximum(jnp.max(sd_a, axis=0, keepdims=True), jnp.max(sd_b, axis=0, keepdims=True))

    importance(next_tile, sc_next, 1 - slot)

    pw = jnp.exp2(sw - jnp.max(sw, axis=0, keepdims=True))
    acc_w = jnp.dot(vw, pw.astype(BF16), preferred_element_type=F32)
    o_w = acc_w[:HEAD_DIM, :] * (1.0 / acc_w[HEAD_DIM:HEAD_DIM + 1, :])

    yield
    m_s, acc_s, mx_s = lax.fori_loop(0, last, sel_body, init)

    def finish(cur_ref):
        m_new = jnp.maximum(jnp.maximum(m_s, mx_s), mx_d)
        p_diag = jnp.exp2(sd_ref[...] - m_new).astype(BF16)
        pv = jnp.dot(vst_ref[0, 0, n_full], p_diag, preferred_element_type=F32)
        for j in range(SEL_UNROLL):
            pv = pv + consume_tile(last * SEL_UNROLL + j, j, cur_ref, m_new)
        pick_blocks(next_tile, 1 - slot)
        return jnp.exp2(m_s - m_new) * acc_s + pv

    acc_s = lax.cond((last & 1) == 0, lambda: finish(sbuf_a_ref), lambda: finish(sbuf_b_ref))
    o_s = acc_s[:HEAD_DIM, :] * (1.0 / acc_s[HEAD_DIM:HEAD_DIM + 1, :])

    gts = g_ref[0, 0, 0]
    o_ref[0, 0, 0] = (gts[0:1, :] * oc_ref[slot] + gts[1:2, :] * o_s + gts[2:3, :] * o_w).astype(o_ref.dtype)


def _attn(q_t, kc_nat, vc_tr, ks, vst, kw, vwt, gts, s):
    bsz, G, nq = q_t.shape[:3]
    nh = s // CMP_STRIDE
    ns = s // SLC_BLOCK
    nkt = s // KEY_TILE
    nst = nkt // SEL_UNROLL
    assert s % (SEL_UNROLL * KEY_TILE) == 0
    top_n = min(SLC_TOP_N, ns)
    gps = GROUPS_PER_STEP
    assert G % gps == 0
    per_q = lambda blk: pl.BlockSpec(blk, lambda b, g, i: (b, g, i, 0, 0))
    per_bg = lambda blk: pl.BlockSpec(blk, lambda b, g, i: (b, g, 0, 0, 0))
    kk = np.arange(Q_TILE)[:, None]
    tl = np.tile(np.arange(Q_TILE), HEADS_PER_GROUP)[None, :]
    tri = jnp.asarray(np.stack([np.where(kk <= tl, 0.0, NEG), np.where(kk > tl, 0.0, NEG)]), BF16)
    cend = jnp.asarray(np.arange(nh)[:, None] * CMP_STRIDE + (CMP_BLOCK - 1) - tl, jnp.int32)
    return pl.pallas_call(
        functools.partial(_attn_kernel, nh=nh, ns=ns, nst=nst, nq=nq, top_n=top_n),
        out_shape=jax.ShapeDtypeStruct((bsz, G, nq, HEAD_DIM, QCOLS), BF16),
        grid=(bsz, G // gps, nq),
        in_specs=[
            per_q((1, gps, 1, HEAD_DIM, QCOLS)),
            pl.BlockSpec((1, gps, 1, HEAD_DIM, QCOLS), lambda b, g, i: (b, g, jnp.minimum(i + 1, nq - 1), 0, 0)),
            pl.BlockSpec((1, 1, gps, nh, 128), lambda b, g, i: (0, b, g, 0, 0)),
            pl.BlockSpec((1, 1, gps, HEAD_DIM, nh), lambda b, g, i: (1, b, g, 0, 0)),
            per_bg((1, gps, nkt, KEY_TILE, 128)),
            per_bg((1, gps, nkt, V_ROWS, KEY_TILE)),
            per_bg((1, gps, nq, Q_TILE, 128)),
            per_bg((1, gps, nq, V_ROWS, Q_TILE)),
            per_q((1, gps, 1, 8, QCOLS)),
            pl.BlockSpec((2, Q_TILE, QCOLS), lambda b, g, i: (0, 0, 0)),
            pl.BlockSpec((nh, QCOLS), lambda b, g, i: (0, 0)),
        ],
        out_specs=per_q((1, gps, 1, HEAD_DIM, QCOLS)),
        scratch_shapes=[pltpu.VMEM((gps, nh + 8, Q_TILE), F32),
                        pltpu.VMEM((gps, ns, Q_TILE), F32),
                        pltpu.VMEM((gps, 2, HEAD_DIM, QCOLS), F32),
                        pltpu.VMEM((gps, 2, nst, BF16_ROWS, QCOLS), BF16),
                        pltpu.VMEM((gps, 2, nst, BF16_ROWS, QCOLS), BF16),
                        pltpu.VMEM((gps, SEL_UNROLL * KEY_TILE, QCOLS), F32),
                        pltpu.VMEM((gps, SEL_UNROLL * KEY_TILE, QCOLS), F32),
                        pltpu.VMEM((gps, KEY_TILE, QCOLS), F32)],
        compiler_params=_cparams(3, 52),
        name="attn",
    )(q_t, q_t, kc_nat, vc_tr, ks, vst, kw, vwt, gts, tri, cend)


def _final_kernel(x_ref, gs_ref, ot_ref, z_ref, g_ref, wba_ref, wout_ref, fnw_ref, o_ref, *, last):
    halves = TOK_TILE // Q_TILE
    parts = []
    for hf in range(halves):
        rows = []
        for g in range(ATTN_GROUPS):
            blk = ot_ref[0, g, hf].astype(F32)
            for h in range(HEADS_PER_GROUP):
                rows.append(blk[:, h * Q_TILE:(h + 1) * Q_TILE])
        parts.append(jnp.concatenate(rows, axis=0).T)
    o = jnp.concatenate(parts, axis=0)
    y_attn = jnp.dot((o * _silu(z_ref[...])).astype(BF16), wba_ref[...], preferred_element_type=F32)
    merged = gs_ref[...].astype(F32) + jax.nn.sigmoid(g_ref[...]) * y_attn
    xo = x_ref[...] + jnp.dot(merged.astype(BF16), wout_ref[...], preferred_element_type=F32)
    if last:
        ms = jnp.mean(xo * xo, axis=-1, keepdims=True)
        xo = (xo * lax.rsqrt(ms + EPS)) * fnw_ref[...]
    o_ref[...] = xo


def _final(x2, gs, o_t, slab, wba, wout, fnw, bsz, s, last):
    nt = s // TOK_TILE
    halves = TOK_TILE // Q_TILE
    rows = lambda b, i: b * nt + i
    const = lambda shape: pl.BlockSpec(shape, lambda b, i: (0,) * len(shape))
    return pl.pallas_call(
        functools.partial(_final_kernel, last=last),
        out_shape=jax.ShapeDtypeStruct((bsz * s, D_MODEL), F32),
        grid=(bsz, nt),
        in_specs=[
            pl.BlockSpec((TOK_TILE, D_MODEL), lambda b, i: (rows(b, i), 0)),
            pl.BlockSpec((TOK_TILE, D_MODEL), lambda b, i: (rows(b, i), 0)),
            pl.BlockSpec((1, ATTN_GROUPS, halves, HEAD_DIM, QCOLS), lambda b, i: (b, 0, i, 0, 0)),
            pl.BlockSpec((TOK_TILE, ATTN_WIDTH), lambda b, i: (rows(b, i), SLAB_ZATT // ATTN_WIDTH)),
            pl.BlockSpec((TOK_TILE, D_MODEL), lambda b, i: (rows(b, i), SLAB_GATT // D_MODEL)),
            const((ATTN_WIDTH, D_MODEL)), const((D_MODEL, D_MODEL)), const((1, D_MODEL)),
        ],
        out_specs=pl.BlockSpec((TOK_TILE, D_MODEL), lambda b, i: (rows(b, i), 0)),
        compiler_params=_cparams(2, 32),
        name="final",
    )(x2, gs, o_t, slab, slab, wba, wout, fnw)


_W_OFF = np.cumsum([0, SSM_D_INNER, SSM_CONV_DIM, SSM_HEADS, ATTN_WIDTH, 6 * KV_WIDTH, 3 * ATTN_HEADS,
                    ATTN_WIDTH, D_MODEL, D_MODEL]).tolist()


def _wslab_kernel(w_ref, o_ref):
    x = w_ref[...]
    seg = lambda i, j: x[_W_OFF[i]:_W_OFF[j], :]
    pad = jnp.zeros((128 - SSM_HEADS - 3 * ATTN_HEADS, x.shape[1]), x.dtype)
    o_ref[...] = jnp.concatenate(
        [seg(1, 2), seg(0, 1), seg(3, 4), seg(6, 9), seg(4, 5), seg(2, 3), seg(5, 6), pad], axis=0).astype(BF16)


def _slab_weights(w_all, layer):
    w_t = jnp.swapaxes(w_all, 1, 2)
    _, n, d = w_t.shape
    assert all(o % BF16_ROWS == 0 for o in _W_OFF)
    return pl.pallas_call(
        _wslab_kernel,
        out_shape=jax.ShapeDtypeStruct((SLAB_W, d), BF16),
        grid=(d // 128,),
        in_specs=[pl.BlockSpec((None, n, 128), lambda c: (layer, 0, c))],
        out_specs=pl.BlockSpec((SLAB_W, 128), lambda c: (0, c)),
        compiler_params=_cparams(1, 24),
        name="wslab",
    )(w_t)


def _rope_tables(s):
    half = HEAD_DIM // 2
    inv_freq = 1.0 / (ROPE_THETA ** (np.arange(half, dtype=np.float64) * 2.0 / HEAD_DIM))
    ang = np.arange(s, dtype=np.float64)[:, None] * inv_freq[None, :]
    cos, sin = np.cos(ang), np.sin(ang)
    cos_t = np.concatenate([cos, cos] * (128 // HEAD_DIM), axis=1)
    sin_t = np.concatenate([-sin, sin] * (128 // HEAD_DIM), axis=1)
    return jnp.asarray(cos_t, F32), jnp.asarray(sin_t, F32)


def _pad_lanes(v, width=128):
    v = v.reshape(1, -1)
    return jnp.concatenate([v, jnp.zeros((1, width - v.shape[1]), v.dtype)], axis=1)


def kernel(x, norm_w, w_in, conv_w, conv_b, dt_bias, a_log, d_skip, ssm_norm_w, cmp_pos_k, cmp_pos_v, cmp_k_w1,
           cmp_k_b1, cmp_k_w2, cmp_v_w1, cmp_v_b1, cmp_v_w2, w_branch_ssm, w_branch_attn, w_out, final_norm_w):
    bsz, s, _ = x.shape
    depth = norm_w.shape[0]
    cos_t, sin_t = _rope_tables(s)
    x2 = x.reshape(bsz * s, D_MODEL)
    for layer in range(depth):
        slab = _proj(x2, norm_w[layer].reshape(1, -1), _slab_weights(w_in, layer))

        gs = _ssd(slab, bsz, s, conv_w[layer], conv_b[layer].reshape(1, -1), _pad_lanes(dt_bias[layer]),
                  _pad_lanes(a_log[layer]), jnp.repeat(d_skip[layer], SSM_HEAD_DIM).reshape(1, -1),
                  ssm_norm_w[layer].reshape(1, -1), w_branch_ssm[layer].astype(BF16))

        q_t, ks, vst, kw, vwt, hkv, gts = _prep(slab, bsz, s, cos_t, sin_t)

        pos = jnp.stack([cmp_pos_k[layer], cmp_pos_v[layer]]).reshape(2, 2, 1, CMP_STRIDE * HEAD_DIM)
        w1 = jnp.stack([cmp_k_w1[layer], cmp_v_w1[layer]])
        w1 = jnp.concatenate([w1[:, :CMP_STRIDE * HEAD_DIM], w1[:, CMP_STRIDE * HEAD_DIM:]], axis=2).astype(BF16)
        b1 = jnp.stack([cmp_k_b1[layer], cmp_v_b1[layer]]).reshape(2, 1, CMP_HIDDEN)
        w2 = jnp.stack([cmp_k_w2[layer], cmp_v_w2[layer]])
        w2 = jnp.concatenate([w2, jnp.zeros((2, CMP_HIDDEN, 128 - HEAD_DIM), w2.dtype)], axis=2).astype(BF16)
        c_nat, c_tr = _compress(hkv, pos, w1, b1, w2)

        o_t = _attn(q_t, c_nat, c_tr, ks, vst, kw, vwt, gts, s)

        x2 = _final(x2, gs, o_t, slab, w_branch_attn[layer].astype(BF16), w_out[layer].astype(BF16),
                    final_norm_w.reshape(1, -1), bsz, s, layer == depth - 1)
    return x2.reshape(bsz, s, D_MODEL)
```
